```python
import jax, jax.numpy as jnp
from jax import lax
import numpy as np

D_MODEL = 1024
BATCH = 16
SEQ = 2048
DEPTH = 2

CTX_LEN = 256
GRID_W = 64
EXPAND = 2
D_INNER = EXPAND * D_MODEL
HEAD_DIM = 128
A_WIDTH = D_INNER // 2
N_Q_HEADS = A_WIDTH // HEAD_DIM
N_KV_HEADS = 2
Q_PER_KV = N_Q_HEADS // N_KV_HEADS
KV_WIDTH = N_KV_HEADS * HEAD_DIM
B_WIDTH = D_INNER - A_WIDTH
CHUNK = 128
B_GROUP_DIM = 128
B_GROUPS = B_WIDTH // B_GROUP_DIM
C_GROUP_DIM = 128
C_GROUPS = D_INNER // C_GROUP_DIM
Q_BLOCK = 128
ROPE_THETA = 10000.0
EVEN_SPLITS = (A_WIDTH, A_WIDTH + KV_WIDTH, A_WIDTH + 2 * KV_WIDTH,
               A_WIDTH + 2 * KV_WIDTH + B_WIDTH, A_WIDTH + 2 * KV_WIDTH + 2 * B_WIDTH)
EVEN_IN = A_WIDTH + 2 * KV_WIDTH + 2 * B_WIDTH + D_INNER
ODD_IN = 2 * D_INNER
N_EVEN = (DEPTH + 1) // 2
N_ODD = DEPTH // 2
ALPHA = (2 * DEPTH) ** 0.25
BETA = (8 * DEPTH) ** -0.25
EPS = 1e-6
MOD_INIT = 0.5

kernel_name = "hybrid_dit_gqa_gmlp_fnet_prefix"


def layer_norm(x, g=None, b=None):
    x32 = x.astype(jnp.float32)
    mu = jnp.mean(x32, axis=-1, keepdims=True)
    var = jnp.mean(jnp.square(x32 - mu), axis=-1, keepdims=True)
    y = (x32 - mu) * lax.rsqrt(var + EPS)
    if g is not None:
        y = y * g.astype(jnp.float32) + b.astype(jnp.float32)
    return y.astype(x.dtype)


def rms_norm(x, g):
    x32 = x.astype(jnp.float32)
    y = x32 * lax.rsqrt(jnp.mean(jnp.square(x32), axis=-1, keepdims=True) + EPS)
    return (y * g.astype(jnp.float32)).astype(x.dtype)


def adaln(cond, w_mod, b_mod):
    h = jax.nn.silu(cond) @ w_mod + b_mod
    return jnp.split(h, 3, axis=-1)


def modulate(x, shift, scale):
    return layer_norm(x) * (1.0 + scale) + shift


def axial_rope_tables(n_tokens, dtype):
    rows = n_tokens // GRID_W
    r, cl = jnp.meshgrid(jnp.arange(rows), jnp.arange(GRID_W), indexing="ij")
    row = r.reshape(-1).astype(jnp.float32)
    col = cl.reshape(-1).astype(jnp.float32)
    n_pairs_axis = HEAD_DIM // 4
    inv_freq = ROPE_THETA ** (-jnp.arange(n_pairs_axis, dtype=jnp.float32) / n_pairs_axis)
    ang = jnp.concatenate([row[:, None] * inv_freq, col[:, None] * inv_freq], axis=-1)
    return jnp.cos(ang).astype(dtype), jnp.sin(ang).astype(dtype)


def apply_rope(x, cos, sin):
    half = HEAD_DIM // 2
    x1, x2 = x[..., :half], x[..., half:]
    cs, sn = cos[:, None, :], sin[:, None, :]
    return jnp.concatenate([x1 * cs - x2 * sn, x2 * cs + x1 * sn], axis=-1)


def attend_blocked(q, k, v):
    bsz, t = q.shape[0], q.shape[1]
    nblk = t // Q_BLOCK
    qb = q.reshape(bsz, nblk, Q_BLOCK, N_KV_HEADS, Q_PER_KV, HEAD_DIM).transpose(1, 0, 2, 3, 4, 5)
    scale = HEAD_DIM ** -0.5

    def one_block(qblk):
        s = jnp.einsum("bqkgd,bskd->bkgqs", qblk, k).astype(jnp.float32) * scale
        p = jax.nn.softmax(s, axis=-1).astype(v.dtype)
        return jnp.einsum("bkgqs,bskd->bqkgd", p, v)

    o = lax.map(one_block, qb)
    return o.transpose(1, 0, 2, 3, 4, 5).reshape(bsz, t, N_Q_HEADS * HEAD_DIM)


def chunk_spatial_gate(u, v, ln_g, ln_b, w_s, b_s):
    bsz, t = v.shape[0], v.shape[1]
    vn = layer_norm(v, ln_g, ln_b).reshape(bsz, t // CHUNK, CHUNK, B_GROUPS, B_GROUP_DIM)
    mixed = jnp.einsum("gpq,bnqgc->bnpgc", w_s, vn) + b_s.T[:, :, None]
    return u * mixed.reshape(bsz, t, B_WIDTH)


def fourier_mix(h):
    bsz, t, w = h.shape
    hg = h.astype(jnp.float32).reshape(bsz, t, C_GROUPS, C_GROUP_DIM)
    f = jnp.fft.fft2(hg, axes=(1, 3), norm="ortho").real
    return f.reshape(bsz, t, w).astype(h.dtype)


def even_project(h, w_in, q_g, k_g):
    bsz, t = h.shape[0], h.shape[1]
    z = h @ w_in
    q, k, v, bu, bv, g = jnp.split(z, EVEN_SPLITS, axis=-1)
    q = rms_norm(q.reshape(bsz, t, N_Q_HEADS, HEAD_DIM), q_g)
    k = rms_norm(k.reshape(bsz, t, N_KV_HEADS, HEAD_DIM), k_g)
    v = v.reshape(bsz, t, N_KV_HEADS, HEAD_DIM)
    return q, k, v, jax.nn.gelu(bu), jax.nn.gelu(bv), g


def even_layer(x, ctx, c, c_ctx, w_mod, b_mod, post_g, post_b, w_in, q_g, k_g,
               v_ln_g, v_ln_b, w_s, b_s, w_out, cos, sin, update_ctx):
    sh, sc, gt = adaln(c, w_mod, b_mod)
    sh_c, sc_c, gt_c = adaln(c_ctx, w_mod, b_mod)
    m = modulate(x, sh[:, None], sc[:, None])
    mc = modulate(ctx, sh_c, sc_c)
    q, k, v, bu, bv, g = even_project(m, w_in, q_g, k_g)
    qc, kc, vc, buc, bvc, gc = even_project(mc, w_in, q_g, k_g)
    q = apply_rope(q, cos, sin)
    k = apply_rope(k, cos, sin)
    a = attend_blocked(q, jnp.concatenate([kc, k], axis=1), jnp.concatenate([vc, v], axis=1))
    s = chunk_spatial_gate(bu, bv, v_ln_g, v_ln_b, w_s, b_s)
    y = (jnp.concatenate([a, s], axis=-1) * jax.nn.silu(g)) @ w_out
    x_new = layer_norm(ALPHA * x + gt[:, None] * y, post_g, post_b)
    if update_ctx:
        ac = attend_blocked(qc, kc, vc)
        scx = chunk_spatial_gate(buc, bvc, v_ln_g, v_ln_b, w_s, b_s)
        yc = (jnp.concatenate([ac, scx], axis=-1) * jax.nn.silu(gc)) @ w_out
        ctx = layer_norm(ALPHA * ctx + gt_c * yc, post_g, post_b)
    return x_new, ctx


def odd_stream(h_in, shift, scale, gate, w_in, w_out, post_g, post_b):
    m = modulate(h_in, shift, scale)
    hb, g = jnp.split(m @ w_in, 2, axis=-1)
    y = (fourier_mix(hb) * jax.nn.silu(g)) @ w_out
    return layer_norm(ALPHA * h_in + gate * y, post_g, post_b)


def odd_layer(x, ctx, c, c_ctx, w_mod, b_mod, post_g, post_b, w_in, w_out, update_ctx):
    sh, sc, gt = adaln(c, w_mod, b_mod)
    x_new = odd_stream(x, sh[:, None], sc[:, None], gt[:, None], w_in, w_out, post_g, post_b)
    if update_ctx:
        sh_c, sc_c, gt_c = adaln(c_ctx, w_mod, b_mod)
        ctx = odd_stream(ctx, sh_c, sc_c, gt_c, w_in, w_out, post_g, post_b)
    return x_new, ctx


def setup_inputs(seed: int = 0) -> dict:
    key = jax.random.key(seed)
    ks = jax.random.split(key, 20)
    f32 = jnp.float32
    nrm = lambda k, shape, s: jax.random.normal(k, shape, f32) * s
    return {
        "x": nrm(ks[0], (BATCH, SEQ, D_MODEL), 1.0),
        "c": nrm(ks[1], (BATCH, D_MODEL), 1.0),
        "ctx": nrm(ks[2], (BATCH, CTX_LEN, D_MODEL), 1.0),
        "c_ctx": nrm(ks[3], (D_MODEL,), 1.0),
        "w_mod": nrm(ks[4], (DEPTH, D_MODEL, 3 * D_MODEL), MOD_INIT * D_MODEL ** -0.5),
        "b_mod": nrm(ks[5], (DEPTH, 3 * D_MODEL), 0.01),
        "post_ln_g": 1.0 + nrm(ks[6], (DEPTH, D_MODEL), 0.01),
        "post_ln_b": nrm(ks[7], (DEPTH, D_MODEL), 0.01),
        "even_w_in": nrm(ks[8], (N_EVEN, D_MODEL, EVEN_IN), D_MODEL ** -0.5),
        "even_q_norm": 1.0 + nrm(ks[9], (N_EVEN, HEAD_DIM), 0.01),
        "even_k_norm": 1.0 + nrm(ks[10], (N_EVEN, HEAD_DIM), 0.01),
        "even_v_ln_g": 1.0 + nrm(ks[11], (N_EVEN, B_WIDTH), 0.01),
        "even_v_ln_b": nrm(ks[12], (N_EVEN, B_WIDTH), 0.01),
        "even_w_s": nrm(ks[13], (N_EVEN, B_GROUPS, CHUNK, CHUNK), CHUNK ** -0.5),
        "even_b_s": 1.0 + nrm(ks[14], (N_EVEN, B_GROUPS, CHUNK), 0.01),
        "even_w_out": nrm(ks[15], (N_EVEN, D_INNER, D_MODEL), BETA * D_INNER ** -0.5),
        "odd_w_in": nrm(ks[16], (N_ODD, D_MODEL, ODD_IN), D_MODEL ** -0.5),
        "odd_w_out": nrm(ks[17], (N_ODD, D_INNER, D_MODEL), BETA * D_INNER ** -0.5),
    }


def reference(x, c, ctx, c_ctx, w_mod, b_mod, post_ln_g, post_ln_b, even_w_in, even_q_norm,
              even_k_norm, even_v_ln_g, even_v_ln_b, even_w_s, even_b_s, even_w_out,
              odd_w_in, odd_w_out):
    cos, sin = axial_rope_tables(x.shape[1], x.dtype)
    for l in range(DEPTH):
        update_ctx = l < DEPTH - 1
        i = l // 2
        if l % 2 == 0:
            x, ctx = even_layer(x, ctx, c, c_ctx, w_mod[l], b_mod[l], post_ln_g[l], post_ln_b[l],
                                even_w_in[i], even_q_norm[i], even_k_norm[i], even_v_ln_g[i],
                                even_v_ln_b[i], even_w_s[i], even_b_s[i], even_w_out[i],
                                cos, sin, update_ctx)
        else:
            x, ctx = odd_layer(x, ctx, c, c_ctx, w_mod[l], b_mod[l], post_ln_g[l], post_ln_b[l],
                               odd_w_in[i], odd_w_out[i], update_ctx)
    return x
```

```python
import functools

import numpy as np
import jax
import jax.numpy as jnp
from jax import lax
from jax.experimental import pallas as pl
from jax.experimental.pallas import tpu as pltpu

D_MODEL = 1024
DEPTH = 2
GRID_W = 64
D_INNER = 2 * D_MODEL
HEAD_DIM = 128
A_WIDTH = D_INNER // 2
N_Q_HEADS = A_WIDTH // HEAD_DIM
N_KV_HEADS = 2
Q_PER_KV = N_Q_HEADS // N_KV_HEADS
KV_WIDTH = N_KV_HEADS * HEAD_DIM
B_WIDTH = D_INNER - A_WIDTH
CHUNK = 128
B_GROUP_DIM = 128
B_GROUPS = B_WIDTH // B_GROUP_DIM
C_GROUP_DIM = 128
C_GROUPS = D_INNER // C_GROUP_DIM
ROPE_THETA = 10000.0
EVEN_IN = A_WIDTH + 2 * KV_WIDTH + 2 * B_WIDTH + D_INNER
ODD_IN = 2 * D_INNER
ALPHA = (2 * DEPTH) ** 0.25
EPS = 1e-6

_Q0, _K0, _V0 = 0, A_WIDTH, A_WIDTH + KV_WIDTH
_U0 = A_WIDTH + 2 * KV_WIDTH
_BV0 = _U0 + B_WIDTH
_G0 = _BV0 + B_WIDTH

FFT_N1 = 8
FFT_N2 = 256

V7X_VMEM_LIMIT_BYTES = 60000 * 1024

F32 = jnp.float32
BF16 = jnp.bfloat16


def _dot(a, b):
    return jnp.dot(a, b, preferred_element_type=F32)


def _dot_nt(a, b):
    return lax.dot_general(a, b, (((1,), (1,)), ((), ())), preferred_element_type=F32)


def _ln(x):
    mu = jnp.mean(x, axis=-1, keepdims=True)
    xc = x - mu
    var = jnp.mean(xc * xc, axis=-1, keepdims=True)
    return xc * lax.rsqrt(var + EPS)


def _rms_head(z, g):
    return z * lax.rsqrt(jnp.mean(z * z, axis=-1, keepdims=True) + EPS) * g


def _rope(y, cos2, sin2):
    return y * cos2 + pltpu.roll(y, HEAD_DIM // 2, 1) * sin2


def _params(*sem):
    return pltpu.CompilerParams(dimension_semantics=sem, vmem_limit_bytes=V7X_VMEM_LIMIT_BYTES)


def _const_spec(shape):
    nd = len(shape)
    return pl.BlockSpec(shape, lambda *_: (0,) * nd, pipeline_mode=pl.Buffered(1))


def _adaln_kernel(c_ref, w_ref, b_ref, o_ref):
    h = jax.nn.silu(c_ref[...])
    w = w_ref[0]
    h_hi = h.astype(BF16)
    h_lo = (h - h_hi.astype(F32)).astype(BF16)
    w_hi = w.astype(BF16)
    w_lo = (w - w_hi.astype(F32)).astype(BF16)
    o_ref[0] = _dot(h_hi, w_hi) + _dot(h_hi, w_lo) + _dot(h_lo, w_hi) + b_ref[0]


def _adaln(cond, w_mod, b_mod):
    r = cond.shape[0]
    tn = D_MODEL
    return pl.pallas_call(
        _adaln_kernel,
        grid=(DEPTH, 3 * D_MODEL // tn),
        in_specs=[
            pl.BlockSpec((r, D_MODEL), lambda l, j: (0, 0)),
            pl.BlockSpec((1, D_MODEL, tn), lambda l, j: (l, 0, j)),
            pl.BlockSpec((1, 1, tn), lambda l, j: (l, 0, j)),
        ],
        out_specs=pl.BlockSpec((1, r, tn), lambda l, j: (l, 0, j)),
        out_shape=jax.ShapeDtypeStruct((DEPTH, r, 3 * D_MODEL), F32),
        compiler_params=_params("arbitrary", "arbitrary"),
        name="adaln",
    )(cond, w_mod, b_mod.reshape(DEPTH, 1, 3 * D_MODEL))


def _proj_even_kernel(x_ref, sh_ref, sc_ref, w_ref, qg_ref, kg_ref, vg_ref, vb_ref, cos_ref, sin_ref,
                      q_ref, k_ref, v_ref, u_ref, vn_ref, sg_ref):
    m = (_ln(x_ref[0]) * (1.0 + sc_ref[0]) + sh_ref[0]).astype(BF16)
    cos2 = cos_ref[...]
    sin2 = sin_ref[...]
    qg = qg_ref[...] * (HEAD_DIM ** -0.5)
    kg = kg_ref[...]
    cw = 4 * HEAD_DIM
    for c in range(A_WIDTH // cw):
        z = _dot(m, w_ref[:, _Q0 + c * cw:_Q0 + (c + 1) * cw])
        for j in range(cw // HEAD_DIM):
            zh = z[:, j * HEAD_DIM:(j + 1) * HEAD_DIM]
            q_ref[0, c * (cw // HEAD_DIM) + j] = _rope(_rms_head(zh, qg), cos2, sin2).astype(BF16)
    z = _dot(m, w_ref[:, _K0:_K0 + 2 * KV_WIDTH])
    for j in range(N_KV_HEADS):
        zh = z[:, j * HEAD_DIM:(j + 1) * HEAD_DIM]
        k_ref[0, :, j * HEAD_DIM:(j + 1) * HEAD_DIM] = _rope(_rms_head(zh, kg), cos2, sin2).astype(BF16)
    v_ref[0] = z[:, KV_WIDTH:].astype(BF16)
    for c in range(B_WIDTH // cw):
        z = _dot(m, w_ref[:, _U0 + c * cw:_U0 + (c + 1) * cw])
        u_ref[0, :, c * cw:(c + 1) * cw] = jax.nn.gelu(z).astype(BF16)
    gv = jax.nn.gelu(_dot(m, w_ref[:, _BV0:_BV0 + B_WIDTH]))
    vn_ref[0] = (_ln(gv) * vg_ref[...] + vb_ref[...]).astype(BF16)
    for c in range(D_INNER // cw):
        z = _dot(m, w_ref[:, _G0 + c * cw:_G0 + (c + 1) * cw])
        sg_ref[0, :, c * cw:(c + 1) * cw] = jax.nn.silu(z).astype(BF16)


def _proj_even(x, sh, sc, w_in, q_g, k_g, v_g, v_b, cos2, sin2, tm):
    bsz, t, _ = x.shape
    row = lambda b, i: (b, i, 0)
    per_b = pl.BlockSpec((1, 1, D_MODEL), lambda b, i: (b, 0, 0))
    bf = lambda *s: jax.ShapeDtypeStruct(s, BF16)
    return pl.pallas_call(
        _proj_even_kernel,
        grid=(bsz, t // tm),
        in_specs=[
            pl.BlockSpec((1, tm, D_MODEL), row), per_b, per_b,
            _const_spec((D_MODEL, EVEN_IN)),
            _const_spec((1, HEAD_DIM)), _const_spec((1, HEAD_DIM)),
            _const_spec((1, B_WIDTH)), _const_spec((1, B_WIDTH)),
            pl.BlockSpec((tm, HEAD_DIM), lambda b, i: (i, 0)),
            pl.BlockSpec((tm, HEAD_DIM), lambda b, i: (i, 0)),
        ],
        out_specs=[
            pl.BlockSpec((1, N_Q_HEADS, tm, HEAD_DIM), lambda b, i: (b, 0, i, 0)),
            pl.BlockSpec((1, tm, KV_WIDTH), row), pl.BlockSpec((1, tm, KV_WIDTH), row),
            pl.BlockSpec((1, tm, B_WIDTH), row), pl.BlockSpec((1, tm, B_WIDTH), row),
            pl.BlockSpec((1, tm, D_INNER), row),
        ],
        out_shape=[bf(bsz, N_Q_HEADS, t, HEAD_DIM), bf(bsz, t, KV_WIDTH), bf(bsz, t, KV_WIDTH),
                   bf(bsz, t, B_WIDTH), bf(bsz, t, B_WIDTH), bf(bsz, t, D_INNER)],
        compiler_params=_params("arbitrary", "arbitrary"),
        name="proj_even",
    )(x, sh, sc, w_in, q_g, k_g, v_g, v_b, cos2, sin2)


def _ctx_kv_kernel(c_ref, sh_ref, sc_ref, w_ref, kg_ref, kc_ref, vc_ref):
    m = (_ln(c_ref[0]) * (1.0 + sc_ref[...]) + sh_ref[...]).astype(BF16)
    z = _dot(m, w_ref[...])
    for j in range(N_KV_HEADS):
        zh = z[:, j * HEAD_DIM:(j + 1) * HEAD_DIM]
        kc_ref[0, :, j * HEAD_DIM:(j + 1) * HEAD_DIM] = _rms_head(zh, kg_ref[...]).astype(BF16)
    vc_ref[0] = z[:, KV_WIDTH:].astype(BF16)


def _ctx_kv(ctx, sh_c, sc_c, w_in, k_g):
    bsz, s, _ = ctx.shape
    assert _K0 % (2 * KV_WIDTH) == 0
    return pl.pallas_call(
        _ctx_kv_kernel,
        grid=(bsz,),
        in_specs=[
            pl.BlockSpec((1, s, D_MODEL), lambda b: (b, 0, 0)),
            pl.BlockSpec((1, D_MODEL), lambda b: (0, 0)), pl.BlockSpec((1, D_MODEL), lambda b: (0, 0)),
            pl.BlockSpec((D_MODEL, 2 * KV_WIDTH), lambda b: (0, _K0 // (2 * KV_WIDTH))),
            pl.BlockSpec((1, HEAD_DIM), lambda b: (0, 0)),
        ],
        out_specs=[pl.BlockSpec((1, s, KV_WIDTH), lambda b: (b, 0, 0))] * 2,
        out_shape=[jax.ShapeDtypeStruct((bsz, s, KV_WIDTH), BF16)] * 2,
        compiler_params=_params("arbitrary"),
        name="ctx_kv",
    )(ctx, sh_c, sc_c, w_in, k_g)


def _attn_kernel(q_ref, kc_ref, vc_ref, k_ref, v_ref, a_ref):
    g, tq, _ = q_ref.shape[1:]
    q = q_ref[0].reshape(g * tq, HEAD_DIM)
    s_c = _dot_nt(q, kc_ref[0])
    s_x = _dot_nt(q, k_ref[0])
    mx = jnp.maximum(jnp.max(s_c, axis=-1, keepdims=True), jnp.max(s_x, axis=-1, keepdims=True))
    p_c = jnp.exp(s_c - mx)
    p_x = jnp.exp(s_x - mx)
    den = jnp.sum(p_c, axis=-1, keepdims=True) + jnp.sum(p_x, axis=-1, keepdims=True)
    o = (_dot(p_c.astype(BF16), vc_ref[0]) + _dot(p_x.astype(BF16), v_ref[0])) / den
    for j in range(g):
        a_ref[0, :, j * HEAD_DIM:(j + 1) * HEAD_DIM] = o[j * tq:(j + 1) * tq].astype(BF16)


def _attention(q, kc, vc, k, v, tq):
    bsz, _, t, _ = q.shape
    s = kc.shape[1]
    kv_c = pl.BlockSpec((1, s, HEAD_DIM), lambda b, h, i: (b, 0, h))
    kv_x = pl.BlockSpec((1, t, HEAD_DIM), lambda b, h, i: (b, 0, h))
    return pl.pallas_call(
        _attn_kernel,
        grid=(bsz, N_KV_HEADS, t // tq),
        in_specs=[pl.BlockSpec((1, Q_PER_KV, tq, HEAD_DIM), lambda b, h, i: (b, h, i, 0)),
                  kv_c, kv_c, kv_x, kv_x],
        out_specs=pl.BlockSpec((1, tq, Q_PER_KV * HEAD_DIM), lambda b, h, i: (b, i, h)),
        out_shape=jax.ShapeDtypeStruct((bsz, t, A_WIDTH), BF16),
        compiler_params=_params("arbitrary", "arbitrary", "arbitrary"),
        name="attention",
    )(q, kc, vc, k, v)


def _deepnorm(x, gt, y, pg, pb):
    return _ln(ALPHA * x + gt * y) * pg + pb


def _even_out_kernel(a_ref, u_ref, vn_ref, sg_ref, x_ref, gt_ref, ws_ref, bs_ref, wo_ref, pg_ref, pb_ref,
                     o_ref, comb_ref):
    tm = a_ref.shape[1]
    comb_ref[:, :A_WIDTH] = (a_ref[0].astype(F32) * sg_ref[0, :, :A_WIDTH].astype(F32)).astype(BF16)
    for n in range(tm // CHUNK):
        rows = slice(n * CHUNK, (n + 1) * CHUNK)
        for g in range(B_GROUPS):
            cols = slice(g * B_GROUP_DIM, (g + 1) * B_GROUP_DIM)
            mixed = _dot(ws_ref[g], vn_ref[0, rows, cols]) + bs_ref[g]
            gate = sg_ref[0, rows, A_WIDTH + g * B_GROUP_DIM:A_WIDTH + (g + 1) * B_GROUP_DIM].astype(F32)
            comb_ref[rows, A_WIDTH + g * B_GROUP_DIM:A_WIDTH + (g + 1) * B_GROUP_DIM] = (
                u_ref[0, rows, cols].astype(F32) * mixed * gate).astype(BF16)
    y = _dot(comb_ref[...], wo_ref[...])
    o_ref[0] = _deepnorm(x_ref[0], gt_ref[0], y, pg_ref[...], pb_ref[...])


def _even_out(a, u, vn, sg, x, gt, w_s, b_s, w_out, pg, pb, tm):
    bsz, t, _ = x.shape
    row = lambda b, i: (b, i, 0)
    return pl.pallas_call(
        _even_out_kernel,
        grid=(bsz, t // tm),
        in_specs=[
            pl.BlockSpec((1, tm, A_WIDTH), row), pl.BlockSpec((1, tm, B_WIDTH), row),
            pl.BlockSpec((1, tm, B_WIDTH), row), pl.BlockSpec((1, tm, D_INNER), row),
            pl.BlockSpec((1, tm, D_MODEL), row),
            pl.BlockSpec((1, 1, D_MODEL), lambda b, i: (b, 0, 0)),
            _const_spec((B_GROUPS, CHUNK, CHUNK)), _const_spec((B_GROUPS, CHUNK, B_GROUP_DIM)),
            _const_spec((D_INNER, D_MODEL)),
            _const_spec((1, D_MODEL)), _const_spec((1, D_MODEL)),
        ],
        out_specs=pl.BlockSpec((1, tm, D_MODEL), row),
        out_shape=jax.ShapeDtypeStruct((bsz, t, D_MODEL), F32),
        scratch_shapes=[pltpu.VMEM((tm, D_INNER), BF16)],
        compiler_params=_params("arbitrary", "arbitrary"),
        name="even_out",
    )(a, u, vn, sg, x, gt, w_s, b_s, w_out, pg, pb)


def _proj_odd_kernel(x_ref, sh_ref, sc_ref, w_ref, cs_ref, a_ref, b_ref, sg_ref):
    m = (_ln(x_ref[0]) * (1.0 + sc_ref[0]) + sh_ref[0]).astype(BF16)
    cw = 4 * C_GROUP_DIM
    for c in range(D_INNER // cw):
        z = _dot(m, w_ref[:, c * cw:(c + 1) * cw]).astype(BF16)
        for j in range(cw // C_GROUP_DIM):
            cols = slice(c * cw + j * C_GROUP_DIM, c * cw + (j + 1) * C_GROUP_DIM)
            ab = _dot(z[:, j * C_GROUP_DIM:(j + 1) * C_GROUP_DIM], cs_ref[...])
            a_ref[0, :, cols] = ab[:, :C_GROUP_DIM].astype(BF16)
            b_ref[0, :, cols] = ab[:, C_GROUP_DIM:].astype(BF16)
    for c in range(D_INNER // cw):
        z = _dot(m, w_ref[:, D_INNER + c * cw:D_INNER + (c + 1) * cw])
        sg_ref[0, :, c * cw:(c + 1) * cw] = jax.nn.silu(z).astype(BF16)


def _proj_odd(x, sh, sc, w_in, cs128, tm):
    bsz, t, _ = x.shape
    row = lambda b, i: (b, i, 0)
    per_b = pl.BlockSpec((1, 1, D_MODEL), lambda b, i: (b, 0, 0))
    out = jax.ShapeDtypeStruct((bsz, t, D_INNER), BF16)
    return pl.pallas_call(
        _proj_odd_kernel,
        grid=(bsz, t // tm),
        in_specs=[pl.BlockSpec((1, tm, D_MODEL), row), per_b, per_b,
                  _const_spec((D_MODEL, ODD_IN)), _const_spec((C_GROUP_DIM, 2 * C_GROUP_DIM))],
        out_specs=[pl.BlockSpec((1, tm, D_INNER), row)] * 3,
        out_shape=[out, out, out],
        compiler_params=_params("arbitrary", "arbitrary"),
        name="proj_odd",
    )(x, sh, sc, w_in, cs128)


def _dft4(ar, ai):
    s0r, s0i = ar[0] + ar[2], ai[0] + ai[2]
    s1r, s1i = ar[0] - ar[2], ai[0] - ai[2]
    s2r, s2i = ar[1] + ar[3], ai[1] + ai[3]
    s3r, s3i = ar[1] - ar[3], ai[1] - ai[3]
    return ([s0r + s2r, s1r - s3i, s0r - s2r, s1r + s3i],
            [s0i + s2i, s1i + s3r, s0i - s2i, s1i - s3r])


def _dft8(zr, zi):
    er, ei = _dft4(zr[0::2], zi[0::2])
    orr, oi = _dft4(zr[1::2], zi[1::2])
    h = np.float32(np.sqrt(0.5))
    tr = [orr[0], (orr[1] - oi[1]) * h, -oi[2], (-orr[3] - oi[3]) * h]
    ti = [oi[0], (orr[1] + oi[1]) * h, orr[2], (orr[3] - oi[3]) * h]
    xr = [er[k] + tr[k] for k in range(4)] + [er[k] - tr[k] for k in range(4)]
    xi = [ei[k] + ti[k] for k in range(4)] + [ei[k] - ti[k] for k in range(4)]
    return xr, xi


def _fft_t_kernel(a_ref, b_ref, twr_ref, twi_ref, cs_ref, f_ref, z2_ref):
    tn = a_ref.shape[2]
    rb = 16

    def step(i, carry):
        r = pl.multiple_of(i * rb, rb)
        for j in range(tn // 128):
            cols = slice(j * 128, (j + 1) * 128)
            zr = [a_ref[0, pl.ds(t1 * FFT_N2 + r, rb), cols].astype(F32) for t1 in range(FFT_N1)]
            zi = [b_ref[0, pl.ds(t1 * FFT_N2 + r, rb), cols].astype(F32) for t1 in range(FFT_N1)]
            xr, xi = _dft8(zr, zi)
            for k1 in range(FFT_N1):
                if k1 == 0:
                    yr, yi = xr[0], xi[0]
                else:
                    wr = twr_ref[k1, pl.ds(r, rb), :]
                    wi = twi_ref[k1, pl.ds(r, rb), :]
                    yr = xr[k1] * wr - xi[k1] * wi
                    yi = xr[k1] * wi + xi[k1] * wr
                z2_ref[k1, pl.ds(r, rb), cols] = yr.astype(BF16)
                z2_ref[k1, pl.ds(FFT_N2 + r, rb), cols] = yi.astype(BF16)
        return carry

    lax.fori_loop(0, FFT_N2 // rb, step, 0)
    for k1 in range(FFT_N1):
        f_ref[0, k1] = _dot(cs_ref[...], z2_ref[k1]).astype(BF16)


def _fft_t(a, b, twr, twi, cs256, tn):
    bsz, t, w = a.shape
    col = lambda bb, j: (bb, 0, j)
    return pl.pallas_call(
        _fft_t_kernel,
        grid=(bsz, w // tn),
        in_specs=[pl.BlockSpec((1, t, tn), col), pl.BlockSpec((1, t, tn), col),
                  _const_spec((FFT_N1, FFT_N2, 128)), _const_spec((FFT_N1, FFT_N2, 128)),
                  _const_spec((FFT_N2, 2 * FFT_N2))],
        out_specs=pl.BlockSpec((1, FFT_N1, FFT_N2, tn), lambda bb, j: (bb, 0, 0, j)),
        out_shape=jax.ShapeDtypeStruct((bsz, FFT_N1, FFT_N2, w), BF16),
        scratch_shapes=[pltpu.VMEM((FFT_N1, 2 * FFT_N2, tn), BF16)],
        compiler_params=_params("arbitrary", "arbitrary"),
        name="fft_t",
    )(a, b, twr, twi, cs256)


def _odd_out_kernel(f_ref, sg_ref, x_ref, gt_ref, wo_ref, pg_ref, pb_ref, o_ref):
    comb = (f_ref[0, 0].astype(F32) * sg_ref[0].astype(F32)).astype(BF16)
    y = _dot(comb, wo_ref[...])
    o_ref[0] = _deepnorm(x_ref[0], gt_ref[0], y, pg_ref[...], pb_ref[...])


def _odd_out(f, sg, x, gt, w_out, pg, pb, tk):
    bsz, t, _ = x.shape
    n2 = t // FFT_N1
    strided = lambda bb, k1, i: (bb, i, k1)
    out = pl.pallas_call(
        _odd_out_kernel,
        grid=(bsz, FFT_N1, n2 // tk),
        in_specs=[
            pl.BlockSpec((1, 1, tk, D_INNER), lambda bb, k1, i: (bb, k1, i, 0)),
            pl.BlockSpec((1, tk, D_INNER), strided),
            pl.BlockSpec((1, tk, D_MODEL), strided),
            pl.BlockSpec((1, 1, D_MODEL), lambda bb, k1, i: (bb, 0, 0)),
            _const_spec((D_INNER, D_MODEL)), _const_spec((1, D_MODEL)), _const_spec((1, D_MODEL)),
        ],
        out_specs=pl.BlockSpec((1, tk, D_MODEL), strided),
        out_shape=jax.ShapeDtypeStruct((bsz, n2, FFT_N1 * D_MODEL), F32),
        compiler_params=_params("arbitrary", "arbitrary", "arbitrary"),
        name="odd_out",
    )(f, sg.reshape(bsz, n2, FFT_N1 * D_INNER), x.reshape(bsz, n2, FFT_N1 * D_MODEL), gt, w_out, pg, pb)
    return out.reshape(bsz, t, D_MODEL)


def _rope_tables(n_tokens):
    rows = n_tokens // GRID_W
    r, cl = jnp.meshgrid(jnp.arange(rows), jnp.arange(GRID_W), indexing="ij")
    row = r.reshape(-1).astype(F32)
    col = cl.reshape(-1).astype(F32)
    n_pairs_axis = HEAD_DIM // 4
    inv_freq = ROPE_THETA ** (-jnp.arange(n_pairs_axis, dtype=F32) / n_pairs_axis)
    ang = jnp.concatenate([row[:, None] * inv_freq, col[:, None] * inv_freq], axis=-1)
    cs, sn = jnp.cos(ang), jnp.sin(ang)
    return jnp.concatenate([cs, cs], axis=-1), jnp.concatenate([-sn, sn], axis=-1)


def _dft_tables(n_tokens):
    assert n_tokens == FFT_N1 * FFT_N2
    c = np.arange(C_GROUP_DIM)
    ang = 2.0 * np.pi * np.outer(c, c) / C_GROUP_DIM
    norm = 1.0 / np.sqrt(float(n_tokens * C_GROUP_DIM))
    cs128 = np.concatenate([np.cos(ang), np.sin(ang)], axis=1) * norm
    k1 = np.arange(FFT_N1)[:, None]
    t2 = np.arange(FFT_N2)[None, :]
    tw = 2.0 * np.pi * k1 * t2 / n_tokens
    twr = np.broadcast_to(np.cos(tw)[:, :, None], (FFT_N1, FFT_N2, 128))
    twi = np.broadcast_to(np.sin(tw)[:, :, None], (FFT_N1, FFT_N2, 128))
    k2 = np.arange(FFT_N2)
    ang2 = 2.0 * np.pi * np.outer(k2, k2) / FFT_N2
    cs256 = np.concatenate([np.cos(ang2), -np.sin(ang2)], axis=1)
    f = lambda a: jnp.asarray(np.ascontiguousarray(a), dtype=F32)
    return f(cs128).astype(BF16), f(twr), f(twi), f(cs256).astype(BF16)


def kernel(x, c, ctx, c_ctx, w_mod, b_mod, post_ln_g, post_ln_b, even_w_in, even_q_norm, even_k_norm,
           even_v_ln_g, even_v_ln_b, even_w_s, even_b_s, even_w_out, odd_w_in, odd_w_out):
    bsz, t, _ = x.shape
    assert DEPTH == 2 and t % CHUNK == 0
    row1 = lambda v: v.reshape(1, -1)

    n_cond = -(-(bsz + 1) // 8) * 8
    cond = jnp.zeros((n_cond, D_MODEL), F32).at[:bsz].set(c).at[bsz].set(c_ctx)
    mod = _adaln(cond, w_mod, b_mod)
    split = lambda l, rows: [mod[l, rows, i * D_MODEL:(i + 1) * D_MODEL] for i in range(3)]
    sh0, sc0, gt0 = [v[:, None, :] for v in split(0, slice(0, bsz))]
    sh0c, sc0c, _ = split(0, slice(bsz, bsz + 1))
    sh1, sc1, gt1 = [v[:, None, :] for v in split(1, slice(0, bsz))]

    cos2, sin2 = _rope_tables(t)
    cs128, twr, twi, cs256 = _dft_tables(t)

    w_in0 = even_w_in[0].astype(BF16)
    q, k, v, u, vn, sg = _proj_even(x, sh0, sc0, w_in0, row1(even_q_norm[0]), row1(even_k_norm[0]),
                                    row1(even_v_ln_g[0]), row1(even_v_ln_b[0]), cos2, sin2, tm=512)
    kc, vc = _ctx_kv(ctx, sh0c, sc0c, w_in0, row1(even_k_norm[0]))
    a = _attention(q, kc, vc, k, v, tq=128)
    b_s = jnp.broadcast_to(even_b_s[0][:, :, None], (B_GROUPS, CHUNK, B_GROUP_DIM))
    x1 = _even_out(a, u, vn, sg, x, gt0, even_w_s[0].astype(BF16), b_s, even_w_out[0].astype(BF16),
                   row1(post_ln_g[0]), row1(post_ln_b[0]), tm=256)

    fa, fb, sg1 = _proj_odd(x1, sh1, sc1, odd_w_in[0].astype(BF16), cs128, tm=512)
    f = _fft_t(fa, fb, twr, twi, cs256, tn=256)
    return _odd_out(f, sg1, x1, gt1, odd_w_out[0].astype(BF16), row1(post_ln_g[1]), row1(post_ln_b[1]),
                    tk=FFT_N2)
```

```python
import functools

import numpy as np
import jax
import jax.numpy as jnp
from jax import lax
from jax.experimental import pallas as pl
from jax.experimental.pallas import tpu as pltpu

D_MODEL = 1024
DEPTH = 2
GRID_W = 64
D_INNER = 2 * D_MODEL
HEAD_DIM = 128
A_WIDTH = D_INNER // 2
N_Q_HEADS = A_WIDTH // HEAD_DIM
N_KV_HEADS = 2
Q_PER_KV = N_Q_HEADS // N_KV_HEADS
KV_WIDTH = N_KV_HEADS * HEAD_DIM
B_WIDTH = D_INNER - A_WIDTH
CHUNK = 128
B_GROUP_DIM = 128
B_GROUPS = B_WIDTH // B_GROUP_DIM
C_GROUP_DIM = 128
C_GROUPS = D_INNER // C_GROUP_DIM
ROPE_THETA = 10000.0
EVEN_IN = A_WIDTH + 2 * KV_WIDTH + 2 * B_WIDTH + D_INNER
ODD_IN = 2 * D_INNER
ALPHA = (2 * DEPTH) ** 0.25
EPS = 1e-6

_Q0, _K0, _V0 = 0, A_WIDTH, A_WIDTH + KV_WIDTH
_U0 = A_WIDTH + 2 * KV_WIDTH
_BV0 = _U0 + B_WIDTH
_G0 = _BV0 + B_WIDTH

FFT_N1 = 8
FFT_N2 = 256

V7X_VMEM_LIMIT_BYTES = 60000 * 1024

F32 = jnp.float32
BF16 = jnp.bfloat16


def _dot(a, b):
    return jnp.dot(a, b, preferred_element_type=F32)


def _dot_nt(a, b):
    return lax.dot_general(a, b, (((1,), (1,)), ((), ())), preferred_element_type=F32)


def _ln(x):
    mu = jnp.mean(x, axis=-1, keepdims=True)
    xc = x - mu
    var = jnp.mean(xc * xc, axis=-1, keepdims=True)
    return xc * lax.rsqrt(var + EPS)


def _rms_head(z, g):
    return z * lax.rsqrt(jnp.mean(z * z, axis=-1, keepdims=True) + EPS) * g


def _rope(y, cos2, sin2):
    return y * cos2 + pltpu.roll(y, HEAD_DIM // 2, 1) * sin2


def _params(*sem):
    return pltpu.CompilerParams(dimension_semantics=sem, vmem_limit_bytes=V7X_VMEM_LIMIT_BYTES)


def _const_spec(shape):
    nd = len(shape)
    return pl.BlockSpec(shape, lambda *_: (0,) * nd, pipeline_mode=pl.Buffered(1))


def _adaln_kernel(c_ref, w_ref, b_ref, o_ref):
    h = jax.nn.silu(c_ref[...])
    w = w_ref[0]
    h_hi = h.astype(BF16)
    h_lo = (h - h_hi.astype(F32)).astype(BF16)
    w_hi = w.astype(BF16)
    w_lo = (w - w_hi.astype(F32)).astype(BF16)
    o_ref[0] = _dot(h_hi, w_hi) + _dot(h_hi, w_lo) + _dot(h_lo, w_hi) + b_ref[0]


def _adaln(cond, w_mod, b_mod):
    r = cond.shape[0]
    tn = D_MODEL
    return pl.pallas_call(
        _adaln_kernel,
        grid=(DEPTH, 3 * D_MODEL // tn),
        in_specs=[
            pl.BlockSpec((r, D_MODEL), lambda l, j: (0, 0)),
            pl.BlockSpec((1, D_MODEL, tn), lambda l, j: (l, 0, j)),
            pl.BlockSpec((1, 1, tn), lambda l, j: (l, 0, j)),
        ],
        out_specs=pl.BlockSpec((1, r, tn), lambda l, j: (l, 0, j)),
        out_shape=jax.ShapeDtypeStruct((DEPTH, r, 3 * D_MODEL), F32),
        compiler_params=_params("arbitrary", "arbitrary"),
        name="adaln",
    )(cond, w_mod, b_mod.reshape(DEPTH, 1, 3 * D_MODEL))


def _proj_even_kernel(x_ref, sh_ref, sc_ref, w_ref, qg_ref, kg_ref, vg_ref, vb_ref, cos_ref, sin_ref,
                      q_ref, k_ref, v_ref, u_ref, vn_ref, sg_ref):
    m = (_ln(x_ref[0]) * (1.0 + sc_ref[0]) + sh_ref[0]).astype(BF16)
    cos2 = cos_ref[...]
    sin2 = sin_ref[...]
    qg = qg_ref[...] * (HEAD_DIM ** -0.5)
    kg = kg_ref[...]
    cw = 4 * HEAD_DIM
    for c in range(A_WIDTH // cw):
        z = _dot(m, w_ref[:, _Q0 + c * cw:_Q0 + (c + 1) * cw])
        for j in range(cw // HEAD_DIM):
            zh = z[:, j * HEAD_DIM:(j + 1) * HEAD_DIM]
            q_ref[0, c * (cw // HEAD_DIM) + j] = _rope(_rms_head(zh, qg), cos2, sin2).astype(BF16)
    z = _dot(m, w_ref[:, _K0:_K0 + 2 * KV_WIDTH])
    for j in range(N_KV_HEADS):
        zh = z[:, j * HEAD_DIM:(j + 1) * HEAD_DIM]
        k_ref[0, :, j * HEAD_DIM:(j + 1) * HEAD_DIM] = _rope(_rms_head(zh, kg), cos2, sin2).astype(BF16)
    v_ref[0] = z[:, KV_WIDTH:].astype(BF16)
    for c in range(B_WIDTH // cw):
        z = _dot(m, w_ref[:, _U0 + c * cw:_U0 + (c + 1) * cw])
        u_ref[0, :, c * cw:(c + 1) * cw] = jax.nn.gelu(z).astype(BF16)
    gv = jax.nn.gelu(_dot(m, w_ref[:, _BV0:_BV0 + B_WIDTH]))
    vn_ref[0] = (_ln(gv) * vg_ref[...] + vb_ref[...]).astype(BF16)
    for c in range(D_INNER // cw):
        z = _dot(m, w_ref[:, _G0 + c * cw:_G0 + (c + 1) * cw])
        sg_ref[0, :, c * cw:(c + 1) * cw] = jax.nn.silu(z).astype(BF16)


def _proj_even(x, sh, sc, w_in, q_g, k_g, v_g, v_b, cos2, sin2, tm):
    bsz, t, _ = x.shape
    row = lambda b, i: (b, i, 0)
    per_b = pl.BlockSpec((1, 1, D_MODEL), lambda b, i: (b, 0, 0))
    bf = lambda *s: jax.ShapeDtypeStruct(s, BF16)
    return pl.pallas_call(
        _proj_even_kernel,
        grid=(bsz, t // tm),
        in_specs=[
            pl.BlockSpec((1, tm, D_MODEL), row), per_b, per_b,
            _const_spec((D_MODEL, EVEN_IN)),
            _const_spec((1, HEAD_DIM)), _const_spec((1, HEAD_DIM)),
            _const_spec((1, B_WIDTH)), _const_spec((1, B_WIDTH)),
            pl.BlockSpec((tm, HEAD_DIM), lambda b, i: (i, 0)),
            pl.BlockSpec((tm, HEAD_DIM), lambda b, i: (i, 0)),
        ],
        out_specs=[
            pl.BlockSpec((1, N_Q_HEADS, tm, HEAD_DIM), lambda b, i: (b, 0, i, 0)),
            pl.BlockSpec((1, tm, KV_WIDTH), row), pl.BlockSpec((1, tm, KV_WIDTH), row),
            pl.BlockSpec((1, tm, B_WIDTH), row), pl.BlockSpec((1, tm, B_WIDTH), row),
            pl.BlockSpec((1, tm, D_INNER), row),
        ],
        out_shape=[bf(bsz, N_Q_HEADS, t, HEAD_DIM), bf(bsz, t, KV_WIDTH), bf(bsz, t, KV_WIDTH),
                   bf(bsz, t, B_WIDTH), bf(bsz, t, B_WIDTH), bf(bsz, t, D_INNER)],
        compiler_params=_params("arbitrary", "arbitrary"),
        name="proj_even",
    )(x, sh, sc, w_in, q_g, k_g, v_g, v_b, cos2, sin2)


def _ctx_kv_kernel(c_ref, sh_ref, sc_ref, w_ref, kg_ref, kc_ref, vc_ref):
    m = (_ln(c_ref[0]) * (1.0 + sc_ref[...]) + sh_ref[...]).astype(BF16)
    z = _dot(m, w_ref[...])
    for j in range(N_KV_HEADS):
        zh = z[:, j * HEAD_DIM:(j + 1) * HEAD_DIM]
        kc_ref[0, :, j * HEAD_DIM:(j + 1) * HEAD_DIM] = _rms_head(zh, kg_ref[...]).astype(BF16)
    vc_ref[0] = z[:, KV_WIDTH:].astype(BF16)


def _ctx_kv(ctx, sh_c, sc_c, w_in, k_g):
    bsz, s, _ = ctx.shape
    assert _K0 % (2 * KV_WIDTH) == 0
    return pl.pallas_call(
        _ctx_kv_kernel,
        grid=(bsz,),
        in_specs=[
            pl.BlockSpec((1, s, D_MODEL), lambda b: (b, 0, 0)),
            pl.BlockSpec((1, D_MODEL), lambda b: (0, 0)), pl.BlockSpec((1, D_MODEL), lambda b: (0, 0)),
            pl.BlockSpec((D_MODEL, 2 * KV_WIDTH), lambda b: (0, _K0 // (2 * KV_WIDTH))),
            pl.BlockSpec((1, HEAD_DIM), lambda b: (0, 0)),
        ],
        out_specs=[pl.BlockSpec((1, s, KV_WIDTH), lambda b: (b, 0, 0))] * 2,
        out_shape=[jax.ShapeDtypeStruct((bsz, s, KV_WIDTH), BF16)] * 2,
        compiler_params=_params("arbitrary"),
        name="ctx_kv",
    )(ctx, sh_c, sc_c, w_in, k_g)


def _attn_kernel(q_ref, kc_ref, vc_ref, k_ref, v_ref, a_ref):
    for j in range(q_ref.shape[1]):
        q = q_ref[0, j]
        s_c = _dot_nt(q, kc_ref[0])
        s_x = _dot_nt(q, k_ref[0])
        mx = jnp.maximum(jnp.max(s_c, axis=-1, keepdims=True), jnp.max(s_x, axis=-1, keepdims=True))
        p_c = jnp.exp(s_c - mx)
        p_x = jnp.exp(s_x - mx)
        den = jnp.sum(p_c, axis=-1, keepdims=True) + jnp.sum(p_x, axis=-1, keepdims=True)
        o = (_dot(p_c.astype(BF16), vc_ref[0]) + _dot(p_x.astype(BF16), v_ref[0])) / den
        a_ref[0, :, j * HEAD_DIM:(j + 1) * HEAD_DIM] = o.astype(BF16)


def _attention(q, kc, vc, k, v, tq):
    bsz, _, t, _ = q.shape
    s = kc.shape[1]
    kv_c = pl.BlockSpec((1, s, HEAD_DIM), lambda b, h, i: (b, 0, h))
    kv_x = pl.BlockSpec((1, t, HEAD_DIM), lambda b, h, i: (b, 0, h))
    return pl.pallas_call(
        _attn_kernel,
        grid=(bsz, N_KV_HEADS, t // tq),
        in_specs=[pl.BlockSpec((1, Q_PER_KV, tq, HEAD_DIM), lambda b, h, i: (b, h, i, 0)),
                  kv_c, kv_c, kv_x, kv_x],
        out_specs=pl.BlockSpec((1, tq, Q_PER_KV * HEAD_DIM), lambda b, h, i: (b, i, h)),
        out_shape=jax.ShapeDtypeStruct((bsz, t, A_WIDTH), BF16),
        compiler_params=_params("arbitrary", "arbitrary", "arbitrary"),
        name="attention",
    )(q, kc, vc, k, v)


def _deepnorm(x, gt, y, pg, pb):
    return _ln(ALPHA * x + gt * y) * pg + pb


def _even_out_kernel(a_ref, u_ref, vn_ref, sg_ref, x_ref, gt_ref, ws_ref, bs_ref, wo_ref, pg_ref, pb_ref,
                     o_ref, comb_ref):
    tm = a_ref.shape[1]
    comb_ref[:, :A_WIDTH] = (a_ref[0].astype(F32) * sg_ref[0, :, :A_WIDTH].astype(F32)).astype(BF16)
    for n in range(tm // CHUNK):
        rows = slice(n * CHUNK, (n + 1) * CHUNK)
        for g in range(B_GROUPS):
            cols = slice(g * B_GROUP_DIM, (g + 1) * B_GROUP_DIM)
            mixed = _dot(ws_ref[g], vn_ref[0, rows, cols]) + bs_ref[g]
            gate = sg_ref[0, rows, A_WIDTH + g * B_GROUP_DIM:A_WIDTH + (g + 1) * B_GROUP_DIM].astype(F32)
            comb_ref[rows, A_WIDTH + g * B_GROUP_DIM:A_WIDTH + (g + 1) * B_GROUP_DIM] = (
                u_ref[0, rows, cols].astype(F32) * mixed * gate).astype(BF16)
    y = _dot(comb_ref[...], wo_ref[...])
    o_ref[0] = _deepnorm(x_ref[0], gt_ref[0], y, pg_ref[...], pb_ref[...])


def _even_out(a, u, vn, sg, x, gt, w_s, b_s, w_out, pg, pb, tm):
    bsz, t, _ = x.shape
    row = lambda b, i: (b, i, 0)
    return pl.pallas_call(
        _even_out_kernel,
        grid=(bsz, t // tm),
        in_specs=[
            pl.BlockSpec((1, tm, A_WIDTH), row), pl.BlockSpec((1, tm, B_WIDTH), row),
            pl.BlockSpec((1, tm, B_WIDTH), row), pl.BlockSpec((1, tm, D_INNER), row),
            pl.BlockSpec((1, tm, D_MODEL), row),
            pl.BlockSpec((1, 1, D_MODEL), lambda b, i: (b, 0, 0)),
            _const_spec((B_GROUPS, CHUNK, CHUNK)), _const_spec((B_GROUPS, CHUNK, B_GROUP_DIM)),
            _const_spec((D_INNER, D_MODEL)),
            _const_spec((1, D_MODEL)), _const_spec((1, D_MODEL)),
        ],
        out_specs=pl.BlockSpec((1, tm, D_MODEL), row),
        out_shape=jax.ShapeDtypeStruct((bsz, t, D_MODEL), F32),
        scratch_shapes=[pltpu.VMEM((tm, D_INNER), BF16)],
        compiler_params=_params("arbitrary", "arbitrary"),
        name="even_out",
    )(a, u, vn, sg, x, gt, w_s, b_s, w_out, pg, pb)


def _proj_odd_kernel(x_ref, sh_ref, sc_ref, w_ref, cs_ref, a_ref, b_ref, sg_ref):
    m = (_ln(x_ref[0]) * (1.0 + sc_ref[0]) + sh_ref[0]).astype(BF16)
    cw = 4 * C_GROUP_DIM
    for c in range(D_INNER // cw):
        z = _dot(m, w_ref[:, c * cw:(c + 1) * cw]).astype(BF16)
        for j in range(cw // C_GROUP_DIM):
            cols = slice(c * cw + j * C_GROUP_DIM, c * cw + (j + 1) * C_GROUP_DIM)
            ab = _dot(z[:, j * C_GROUP_DIM:(j + 1) * C_GROUP_DIM], cs_ref[...])
            a_ref[0, :, cols] = ab[:, :C_GROUP_DIM].astype(BF16)
            b_ref[0, :, cols] = ab[:, C_GROUP_DIM:].astype(BF16)
    for c in range(D_INNER // cw):
        z = _dot(m, w_ref[:, D_INNER + c * cw:D_INNER + (c + 1) * cw])
        sg_ref[0, :, c * cw:(c + 1) * cw] = jax.nn.silu(z).astype(BF16)


def _proj_odd(x, sh, sc, w_in, cs128, tm):
    bsz, t, _ = x.shape
    row = lambda b, i: (b, i, 0)
    per_b = pl.BlockSpec((1, 1, D_MODEL), lambda b, i: (b, 0, 0))
    out = jax.ShapeDtypeStruct((bsz, t, D_INNER), BF16)
    return pl.pallas_call(
        _proj_odd_kernel,
        grid=(bsz, t // tm),
        in_specs=[pl.BlockSpec((1, tm, D_MODEL), row), per_b, per_b,
                  _const_spec((D_MODEL, ODD_IN)), _const_spec((C_GROUP_DIM, 2 * C_GROUP_DIM))],
        out_specs=[pl.BlockSpec((1, tm, D_INNER), row)] * 3,
        out_shape=[out, out, out],
        compiler_params=_params("arbitrary", "arbitrary"),
        name="proj_odd",
    )(x, sh, sc, w_in, cs128)


def _dft4(ar, ai):
    s0r, s0i = ar[0] + ar[2], ai[0] + ai[2]
    s1r, s1i = ar[0] - ar[2], ai[0] - ai[2]
    s2r, s2i = ar[1] + ar[3], ai[1] + ai[3]
    s3r, s3i = ar[1] - ar[3], ai[1] - ai[3]
    return ([s0r + s2r, s1r - s3i, s0r - s2r, s1r + s3i],
            [s0i + s2i, s1i + s3r, s0i - s2i, s1i - s3r])


def _dft8(zr, zi):
    er, ei = _dft4(zr[0::2], zi[0::2])
    orr, oi = _dft4(zr[1::2], zi[1::2])
    h = np.float32(np.sqrt(0.5))
    tr = [orr[0], (orr[1] - oi[1]) * h, -oi[2], (-orr[3] - oi[3]) * h]
    ti = [oi[0], (orr[1] + oi[1]) * h, orr[2], (orr[3] - oi[3]) * h]
    xr = [er[k] + tr[k] for k in range(4)] + [er[k] - tr[k] for k in range(4)]
    xi = [ei[k] + ti[k] for k in range(4)] + [ei[k] - ti[k] for k in range(4)]
    return xr, xi


def _fft_t_kernel(a_ref, b_ref, twr_ref, twi_ref, cs_ref, f_ref, z2_ref):
    tn = a_ref.shape[2]
    rb = 16

    def step(i, carry):
        r = pl.multiple_of(i * rb, rb)
        for j in range(tn // 128):
            cols = slice(j * 128, (j + 1) * 128)
            zr = [a_ref[0, pl.ds(t1 * FFT_N2 + r, rb), cols].astype(F32) for t1 in range(FFT_N1)]
            zi = [b_ref[0, pl.ds(t1 * FFT_N2 + r, rb), cols].astype(F32) for t1 in range(FFT_N1)]
            xr, xi = _dft8(zr, zi)
            for k1 in range(FFT_N1):
                if k1 == 0:
                    yr, yi = xr[0], xi[0]
                else:
                    wr = twr_ref[k1, pl.ds(r, rb), :]
                    wi = twi_ref[k1, pl.ds(r, rb), :]
                    yr = xr[k1] * wr - xi[k1] * wi
                    yi = xr[k1] * wi + xi[k1] * wr
                z2_ref[k1, pl.ds(r, rb), cols] = yr.astype(BF16)
                z2_ref[k1, pl.ds(FFT_N2 + r, rb), cols] = yi.astype(BF16)
        return carry

    lax.fori_loop(0, FFT_N2 // rb, step, 0)
    for k1 in range(FFT_N1):
        f_ref[0, k1] = _dot(cs_ref[...], z2_ref[k1]).astype(BF16)


def _fft_t(a, b, twr, twi, cs256, tn):
    bsz, t, w = a.shape
    col = lambda bb, j: (bb, 0, j)
    return pl.pallas_call(
        _fft_t_kernel,
        grid=(bsz, w // tn),
        in_specs=[pl.BlockSpec((1, t, tn), col), pl.BlockSpec((1, t, tn), col),
                  _const_spec((FFT_N1, FFT_N2, 128)), _const_spec((FFT_N1, FFT_N2, 128)),
                  _const_spec((FFT_N2, 2 * FFT_N2))],
        out_specs=pl.BlockSpec((1, FFT_N1, FFT_N2, tn), lambda bb, j: (bb, 0, 0, j)),
        out_shape=jax.ShapeDtypeStruct((bsz, FFT_N1, FFT_N2, w), BF16),
        scratch_shapes=[pltpu.VMEM((FFT_N1, 2 * FFT_N2, tn), BF16)],
        compiler_params=_params("arbitrary", "arbitrary"),
        name="fft_t",
    )(a, b, twr, twi, cs256)


def _odd_out_kernel(f_ref, sg_ref, x_ref, gt_ref, wo_ref, pg_ref, pb_ref, o_ref):
    comb = (f_ref[0, 0].astype(F32) * sg_ref[0].astype(F32)).astype(BF16)
    y = _dot(comb, wo_ref[...])
    o_ref[0] = _deepnorm(x_ref[0], gt_ref[0], y, pg_ref[...], pb_ref[...])


def _odd_out(f, sg, x, gt, w_out, pg, pb, tk):
    bsz, t, _ = x.shape
    n2 = t // FFT_N1
    strided = lambda bb, k1, i: (bb, i, k1)
    out = pl.pallas_call(
        _odd_out_kernel,
        grid=(bsz, FFT_N1, n2 // tk),
        in_specs=[
            pl.BlockSpec((1, 1, tk, D_INNER), lambda bb, k1, i: (bb, k1, i, 0)),
            pl.BlockSpec((1, tk, D_INNER), strided),
            pl.BlockSpec((1, tk, D_MODEL), strided),
            pl.BlockSpec((1, 1, D_MODEL), lambda bb, k1, i: (bb, 0, 0)),
            _const_spec((D_INNER, D_MODEL)), _const_spec((1, D_MODEL)), _const_spec((1, D_MODEL)),
        ],
        out_specs=pl.BlockSpec((1, tk, D_MODEL), strided),
        out_shape=jax.ShapeDtypeStruct((bsz, n2, FFT_N1 * D_MODEL), F32),
        compiler_params=_params("arbitrary", "arbitrary", "arbitrary"),
        name="odd_out",
    )(f, sg.reshape(bsz, n2, FFT_N1 * D_INNER), x.reshape(bsz, n2, FFT_N1 * D_MODEL), gt, w_out, pg, pb)
    return out.reshape(bsz, t, D_MODEL)


def _rope_tables(n_tokens):
    rows = n_tokens // GRID_W
    r, cl = jnp.meshgrid(jnp.arange(rows), jnp.arange(GRID_W), indexing="ij")
    row = r.reshape(-1).astype(F32)
    col = cl.reshape(-1).astype(F32)
    n_pairs_axis = HEAD_DIM // 4
    inv_freq = ROPE_THETA ** (-jnp.arange(n_pairs_axis, dtype=F32) / n_pairs_axis)
    ang = jnp.concatenate([row[:, None] * inv_freq, col[:, None] * inv_freq], axis=-1)
    cs, sn = jnp.cos(ang), jnp.sin(ang)
    return jnp.concatenate([cs, cs], axis=-1), jnp.concatenate([-sn, sn], axis=-1)


def _dft_tables(n_tokens):
    assert n_tokens == FFT_N1 * FFT_N2
    c = np.arange(C_GROUP_DIM)
    ang = 2.0 * np.pi * np.outer(c, c) / C_GROUP_DIM
    norm = 1.0 / np.sqrt(float(n_tokens * C_GROUP_DIM))
    cs128 = np.concatenate([np.cos(ang), np.sin(ang)], axis=1) * norm
    k1 = np.arange(FFT_N1)[:, None]
    t2 = np.arange(FFT_N2)[None, :]
    tw = 2.0 * np.pi * k1 * t2 / n_tokens
    twr = np.broadcast_to(np.cos(tw)[:, :, None], (FFT_N1, FFT_N2, 128))
    twi = np.broadcast_to(np.sin(tw)[:, :, None], (FFT_N1, FFT_N2, 128))
    k2 = np.arange(FFT_N2)
    ang2 = 2.0 * np.pi * np.outer(k2, k2) / FFT_N2
    cs256 = np.concatenate([np.cos(ang2), -np.sin(ang2)], axis=1)
    f = lambda a: jnp.asarray(np.ascontiguousarray(a), dtype=F32)
    return f(cs128).astype(BF16), f(twr), f(twi), f(cs256).astype(BF16)


def kernel(x, c, ctx, c_ctx, w_mod, b_mod, post_ln_g, post_ln_b, even_w_in, even_q_norm, even_k_norm,
           even_v_ln_g, even_v_ln_b, even_w_s, even_b_s, even_w_out, odd_w_in, odd_w_out):
    bsz, t, _ = x.shape
    assert DEPTH == 2 and t % CHUNK == 0
    row1 = lambda v: v.reshape(1, -1)

    n_cond = -(-(bsz + 1) // 8) * 8
    cond = jnp.zeros((n_cond, D_MODEL), F32).at[:bsz].set(c).at[bsz].set(c_ctx)
    mod = _adaln(cond, w_mod, b_mod)
    split = lambda l, rows: [mod[l, rows, i * D_MODEL:(i + 1) * D_MODEL] for i in range(3)]
    sh0, sc0, gt0 = [v[:, None, :] for v in split(0, slice(0, bsz))]
    sh0c, sc0c, _ = split(0, slice(bsz, bsz + 1))
    sh1, sc1, gt1 = [v[:, None, :] for v in split(1, slice(0, bsz))]

    cos2, sin2 = _rope_tables(t)
    cs128, twr, twi, cs256 = _dft_tables(t)

    w_in0 = even_w_in[0].astype(BF16)
    q, k, v, u, vn, sg = _proj_even(x, sh0, sc0, w_in0, row1(even_q_norm[0]), row1(even_k_norm[0]),
                                    row1(even_v_ln_g[0]), row1(even_v_ln_b[0]), cos2, sin2, tm=512)
    kc, vc = _ctx_kv(ctx, sh0c, sc0c, w_in0, row1(even_k_norm[0]))
    a = _attention(q, kc, vc, k, v, tq=512)
    b_s = jnp.broadcast_to(even_b_s[0][:, :, None], (B_GROUPS, CHUNK, B_GROUP_DIM))
    x1 = _even_out(a, u, vn, sg, x, gt0, even_w_s[0].astype(BF16), b_s, even_w_out[0].astype(BF16),
                   row1(post_ln_g[0]), row1(post_ln_b[0]), tm=256)

    fa, fb, sg1 = _proj_odd(x1, sh1, sc1, odd_w_in[0].astype(BF16), cs128, tm=512)
    f = _fft_t(fa, fb, twr, twi, cs256, tn=256)
    return _odd_out(f, sg1, x1, gt1, odd_w_out[0].astype(BF16), row1(post_ln_g[1]), row1(post_ln_b[1]),
                    tk=FFT_N2)
```

```python
import functools

import numpy as np
import jax
import jax.numpy as jnp
from jax import lax
from jax.experimental import pallas as pl
from jax.experimental.pallas import tpu as pltpu

D_MODEL = 1024
DEPTH = 2
GRID_W = 64
D_INNER = 2 * D_MODEL
HEAD_DIM = 128
A_WIDTH = D_INNER // 2
N_Q_HEADS = A_WIDTH // HEAD_DIM
N_KV_HEADS = 2
Q_PER_KV = N_Q_HEADS // N_KV_HEADS
KV_WIDTH = N_KV_HEADS * HEAD_DIM
B_WIDTH = D_INNER - A_WIDTH
CHUNK = 128
B_GROUP_DIM = 128
B_GROUPS = B_WIDTH // B_GROUP_DIM
C_GROUP_DIM = 128
C_GROUPS = D_INNER // C_GROUP_DIM
ROPE_THETA = 10000.0
EVEN_IN = A_WIDTH + 2 * KV_WIDTH + 2 * B_WIDTH + D_INNER
ODD_IN = 2 * D_INNER
ALPHA = (2 * DEPTH) ** 0.25
EPS = 1e-6

_Q0, _K0, _V0 = 0, A_WIDTH, A_WIDTH + KV_WIDTH
_U0 = A_WIDTH + 2 * KV_WIDTH
_BV0 = _U0 + B_WIDTH
_G0 = _BV0 + B_WIDTH

FFT_N1 = 8
FFT_N2 = 256

V7X_VMEM_LIMIT_BYTES = 60000 * 1024

F32 = jnp.float32
BF16 = jnp.bfloat16


def _dot(a, b):
    return jnp.dot(a, b, preferred_element_type=F32)


def _dot_nt(a, b):
    return lax.dot_general(a, b, (((1,), (1,)), ((), ())), preferred_element_type=F32)


def _ln(x):
    mu = jnp.mean(x, axis=-1, keepdims=True)
    xc = x - mu
    var = jnp.mean(xc * xc, axis=-1, keepdims=True)
    return xc * lax.rsqrt(var + EPS)


def _rms_head(z, g):
    return z * lax.rsqrt(jnp.mean(z * z, axis=-1, keepdims=True) + EPS) * g


def _rope(y, cos2, sin2):
    return y * cos2 + pltpu.roll(y, HEAD_DIM // 2, 1) * sin2


def _params(*sem):
    return pltpu.CompilerParams(dimension_semantics=sem, vmem_limit_bytes=V7X_VMEM_LIMIT_BYTES)


def _const_spec(shape):
    nd = len(shape)
    return pl.BlockSpec(shape, lambda *_: (0,) * nd, pipeline_mode=pl.Buffered(1))


def _adaln_kernel(c_ref, w_ref, b_ref, o_ref):
    h = jax.nn.silu(c_ref[...])
    w = w_ref[0]
    h_hi = h.astype(BF16)
    h_lo = (h - h_hi.astype(F32)).astype(BF16)
    w_hi = w.astype(BF16)
    w_lo = (w - w_hi.astype(F32)).astype(BF16)
    o_ref[0] = _dot(h_hi, w_hi) + _dot(h_hi, w_lo) + _dot(h_lo, w_hi) + b_ref[0]


def _adaln(cond, w_mod, b_mod):
    r = cond.shape[0]
    tn = D_MODEL
    return pl.pallas_call(
        _adaln_kernel,
        grid=(DEPTH, 3 * D_MODEL // tn),
        in_specs=[
            pl.BlockSpec((r, D_MODEL), lambda l, j: (0, 0)),
            pl.BlockSpec((1, D_MODEL, tn), lambda l, j: (l, 0, j)),
            pl.BlockSpec((1, 1, tn), lambda l, j: (l, 0, j)),
        ],
        out_specs=pl.BlockSpec((1, r, tn), lambda l, j: (l, 0, j)),
        out_shape=jax.ShapeDtypeStruct((DEPTH, r, 3 * D_MODEL), F32),
        compiler_params=_params("arbitrary", "arbitrary"),
        name="adaln",
    )(cond, w_mod, b_mod.reshape(DEPTH, 1, 3 * D_MODEL))


def _proj_even_kernel(x_ref, sh_ref, sc_ref, w_ref, qg_ref, kg_ref, vg_ref, vb_ref, cos_ref, sin_ref,
                      q_ref, k_ref, v_ref, u_ref, vn_ref, sg_ref):
    m = (_ln(x_ref[0]) * (1.0 + sc_ref[0]) + sh_ref[0]).astype(BF16)
    cos2 = cos_ref[...]
    sin2 = sin_ref[...]
    qg = qg_ref[...] * (HEAD_DIM ** -0.5)
    kg = kg_ref[...]
    cw = 4 * HEAD_DIM
    for c in range(A_WIDTH // cw):
        z = _dot(m, w_ref[:, _Q0 + c * cw:_Q0 + (c + 1) * cw])
        for j in range(cw // HEAD_DIM):
            zh = z[:, j * HEAD_DIM:(j + 1) * HEAD_DIM]
            q_ref[0, c * (cw // HEAD_DIM) + j] = _rope(_rms_head(zh, qg), cos2, sin2).astype(BF16)
    z = _dot(m, w_ref[:, _K0:_K0 + 2 * KV_WIDTH])
    for j in range(N_KV_HEADS):
        zh = z[:, j * HEAD_DIM:(j + 1) * HEAD_DIM]
        k_ref[0, :, j * HEAD_DIM:(j + 1) * HEAD_DIM] = _rope(_rms_head(zh, kg), cos2, sin2).astype(BF16)
    v_ref[0] = z[:, KV_WIDTH:].astype(BF16)
    for c in range(B_WIDTH // cw):
        z = _dot(m, w_ref[:, _U0 + c * cw:_U0 + (c + 1) * cw])
        u_ref[0, :, c * cw:(c + 1) * cw] = jax.nn.gelu(z).astype(BF16)
    gv = jax.nn.gelu(_dot(m, w_ref[:, _BV0:_BV0 + B_WIDTH]))
    vn_ref[0] = (_ln(gv) * vg_ref[...] + vb_ref[...]).astype(BF16)
    for c in range(D_INNER // cw):
        z = _dot(m, w_ref[:, _G0 + c * cw:_G0 + (c + 1) * cw])
        sg_ref[0, :, c * cw:(c + 1) * cw] = jax.nn.silu(z).astype(BF16)


def _proj_even(x, sh, sc, w_in, q_g, k_g, v_g, v_b, cos2, sin2, tm):
    bsz, t, _ = x.shape
    row = lambda b, i: (b, i, 0)
    per_b = pl.BlockSpec((1, 1, D_MODEL), lambda b, i: (b, 0, 0))
    bf = lambda *s: jax.ShapeDtypeStruct(s, BF16)
    return pl.pallas_call(
        _proj_even_kernel,
        grid=(bsz, t // tm),
        in_specs=[
            pl.BlockSpec((1, tm, D_MODEL), row), per_b, per_b,
            _const_spec((D_MODEL, EVEN_IN)),
            _const_spec((1, HEAD_DIM)), _const_spec((1, HEAD_DIM)),
            _const_spec((1, B_WIDTH)), _const_spec((1, B_WIDTH)),
            pl.BlockSpec((tm, HEAD_DIM), lambda b, i: (i, 0)),
            pl.BlockSpec((tm, HEAD_DIM), lambda b, i: (i, 0)),
        ],
        out_specs=[
            pl.BlockSpec((1, N_Q_HEADS, tm, HEAD_DIM), lambda b, i: (b, 0, i, 0)),
            pl.BlockSpec((1, tm, KV_WIDTH), row), pl.BlockSpec((1, tm, KV_WIDTH), row),
            pl.BlockSpec((1, tm, B_WIDTH), row), pl.BlockSpec((1, tm, B_WIDTH), row),
            pl.BlockSpec((1, tm, D_INNER), row),
        ],
        out_shape=[bf(bsz, N_Q_HEADS, t, HEAD_DIM), bf(bsz, t, KV_WIDTH), bf(bsz, t, KV_WIDTH),
                   bf(bsz, t, B_WIDTH), bf(bsz, t, B_WIDTH), bf(bsz, t, D_INNER)],
        compiler_params=_params("arbitrary", "arbitrary"),
        name="proj_even",
    )(x, sh, sc, w_in, q_g, k_g, v_g, v_b, cos2, sin2)


def _ctx_kv_kernel(c_ref, sh_ref, sc_ref, w_ref, kg_ref, kc_ref, vc_ref):
    m = (_ln(c_ref[0]) * (1.0 + sc_ref[...]) + sh_ref[...]).astype(BF16)
    z = _dot(m, w_ref[...])
    for j in range(N_KV_HEADS):
        zh = z[:, j * HEAD_DIM:(j + 1) * HEAD_DIM]
        kc_ref[0, :, j * HEAD_DIM:(j + 1) * HEAD_DIM] = _rms_head(zh, kg_ref[...]).astype(BF16)
    vc_ref[0] = z[:, KV_WIDTH:].astype(BF16)


def _ctx_kv(ctx, sh_c, sc_c, w_in, k_g):
    bsz, s, _ = ctx.shape
    assert _K0 % (2 * KV_WIDTH) == 0
    return pl.pallas_call(
        _ctx_kv_kernel,
        grid=(bsz,),
        in_specs=[
            pl.BlockSpec((1, s, D_MODEL), lambda b: (b, 0, 0)),
            pl.BlockSpec((1, D_MODEL), lambda b: (0, 0)), pl.BlockSpec((1, D_MODEL), lambda b: (0, 0)),
            pl.BlockSpec((D_MODEL, 2 * KV_WIDTH), lambda b: (0, _K0 // (2 * KV_WIDTH))),
            pl.BlockSpec((1, HEAD_DIM), lambda b: (0, 0)),
        ],
        out_specs=[pl.BlockSpec((1, s, KV_WIDTH), lambda b: (b, 0, 0))] * 2,
        out_shape=[jax.ShapeDtypeStruct((bsz, s, KV_WIDTH), BF16)] * 2,
        compiler_params=_params("arbitrary"),
        name="ctx_kv",
    )(ctx, sh_c, sc_c, w_in, k_g)


def _attn_kernel(q_ref, kc_ref, vc_ref, k_ref, v_ref, a_ref):
    for j in range(q_ref.shape[1]):
        q = q_ref[0, j]
        s_c = _dot_nt(q, kc_ref[0])
        s_x = _dot_nt(q, k_ref[0])
        mx = jnp.maximum(jnp.max(s_c, axis=-1, keepdims=True), jnp.max(s_x, axis=-1, keepdims=True))
        p_c = jnp.exp(s_c - mx)
        p_x = jnp.exp(s_x - mx)
        den = jnp.sum(p_c, axis=-1, keepdims=True) + jnp.sum(p_x, axis=-1, keepdims=True)
        o = (_dot(p_c.astype(BF16), vc_ref[0]) + _dot(p_x.astype(BF16), v_ref[0])) / den
        a_ref[0, :, j * HEAD_DIM:(j + 1) * HEAD_DIM] = o.astype(BF16)


def _attention(q, kc, vc, k, v, tq):
    bsz, _, t, _ = q.shape
    s = kc.shape[1]
    kv_c = pl.BlockSpec((1, s, HEAD_DIM), lambda b, h, i: (b, 0, h))
    kv_x = pl.BlockSpec((1, t, HEAD_DIM), lambda b, h, i: (b, 0, h))
    return pl.pallas_call(
        _attn_kernel,
        grid=(bsz, N_KV_HEADS, t // tq),
        in_specs=[pl.BlockSpec((1, Q_PER_KV, tq, HEAD_DIM), lambda b, h, i: (b, h, i, 0)),
                  kv_c, kv_c, kv_x, kv_x],
        out_specs=pl.BlockSpec((1, tq, Q_PER_KV * HEAD_DIM), lambda b, h, i: (b, i, h)),
        out_shape=jax.ShapeDtypeStruct((bsz, t, A_WIDTH), BF16),
        compiler_params=_params("arbitrary", "arbitrary", "arbitrary"),
        name="attention",
    )(q, kc, vc, k, v)


def _deepnorm(x, gt, y, pg, pb):
    return _ln(ALPHA * x + gt * y) * pg + pb


def _even_out_kernel(a_ref, u_ref, vn_ref, sg_ref, x_ref, gt_ref, ws_ref, bs_ref, wo_ref, pg_ref, pb_ref,
                     o_ref, comb_ref):
    tm = a_ref.shape[1]
    comb_ref[:, :A_WIDTH] = (a_ref[0].astype(F32) * sg_ref[0, :, :A_WIDTH].astype(F32)).astype(BF16)
    for n in range(tm // CHUNK):
        rows = slice(n * CHUNK, (n + 1) * CHUNK)
        for g in range(B_GROUPS):
            cols = slice(g * B_GROUP_DIM, (g + 1) * B_GROUP_DIM)
            mixed = _dot(ws_ref[g], vn_ref[0, rows, cols]) + bs_ref[g]
            gate = sg_ref[0, rows, A_WIDTH + g * B_GROUP_DIM:A_WIDTH + (g + 1) * B_GROUP_DIM].astype(F32)
            comb_ref[rows, A_WIDTH + g * B_GROUP_DIM:A_WIDTH + (g + 1) * B_GROUP_DIM] = (
                u_ref[0, rows, cols].astype(F32) * mixed * gate).astype(BF16)
    y = _dot(comb_ref[...], wo_ref[...])
    o_ref[0] = _deepnorm(x_ref[0], gt_ref[0], y, pg_ref[...], pb_ref[...])


def _even_out(a, u, vn, sg, x, gt, w_s, b_s, w_out, pg, pb, tm):
    bsz, t, _ = x.shape
    row = lambda b, i: (b, i, 0)
    return pl.pallas_call(
        _even_out_kernel,
        grid=(bsz, t // tm),
        in_specs=[
            pl.BlockSpec((1, tm, A_WIDTH), row), pl.BlockSpec((1, tm, B_WIDTH), row),
            pl.BlockSpec((1, tm, B_WIDTH), row), pl.BlockSpec((1, tm, D_INNER), row),
            pl.BlockSpec((1, tm, D_MODEL), row),
            pl.BlockSpec((1, 1, D_MODEL), lambda b, i: (b, 0, 0)),
            _const_spec((B_GROUPS, CHUNK, CHUNK)), _const_spec((B_GROUPS, CHUNK, B_GROUP_DIM)),
            _const_spec((D_INNER, D_MODEL)),
            _const_spec((1, D_MODEL)), _const_spec((1, D_MODEL)),
        ],
        out_specs=pl.BlockSpec((1, tm, D_MODEL), row),
        out_shape=jax.ShapeDtypeStruct((bsz, t, D_MODEL), F32),
        scratch_shapes=[pltpu.VMEM((tm, D_INNER), BF16)],
        compiler_params=_params("arbitrary", "arbitrary"),
        name="even_out",
    )(a, u, vn, sg, x, gt, w_s, b_s, w_out, pg, pb)


def _proj_odd_kernel(x_ref, sh_ref, sc_ref, w_ref, cs_ref, a_ref, b_ref, sg_ref, m_ref):
    tm = x_ref.shape[1]
    n = tm // FFT_N1
    m32 = _ln(x_ref[0]) * (1.0 + sc_ref[0]) + sh_ref[0]
    m = m32.astype(BF16)
    n_slab = D_MODEL // 128
    for j in range(n_slab):
        m_ref[j] = m32[:, j * 128:(j + 1) * 128]
    mp = jnp.concatenate(
        [jnp.concatenate([m_ref[j, pl.ds(t1, n, stride=FFT_N1), :] for j in range(n_slab)], axis=1)
         for t1 in range(FFT_N1)], axis=0).astype(BF16)
    cw = 4 * C_GROUP_DIM
    for c in range(D_INNER // cw):
        z = _dot(mp, w_ref[:, c * cw:(c + 1) * cw]).astype(BF16)
        for j in range(cw // C_GROUP_DIM):
            cols = slice(c * cw + j * C_GROUP_DIM, c * cw + (j + 1) * C_GROUP_DIM)
            ab = _dot(z[:, j * C_GROUP_DIM:(j + 1) * C_GROUP_DIM], cs_ref[...])
            for t1 in range(FFT_N1):
                a_ref[0, t1, :, cols] = ab[t1 * n:(t1 + 1) * n, :C_GROUP_DIM].astype(BF16)
                b_ref[0, t1, :, cols] = ab[t1 * n:(t1 + 1) * n, C_GROUP_DIM:].astype(BF16)
    for c in range(D_INNER // cw):
        z = _dot(m, w_ref[:, D_INNER + c * cw:D_INNER + (c + 1) * cw])
        sg_ref[0, :, c * cw:(c + 1) * cw] = jax.nn.silu(z).astype(BF16)


def _proj_odd(x, sh, sc, w_in, cs128, tm):
    bsz, t, _ = x.shape
    row = lambda b, i: (b, i, 0)
    per_b = pl.BlockSpec((1, 1, D_MODEL), lambda b, i: (b, 0, 0))
    n = tm // FFT_N1
    perm_spec = pl.BlockSpec((1, FFT_N1, n, D_INNER), lambda b, i: (b, 0, i, 0))
    perm = jax.ShapeDtypeStruct((bsz, FFT_N1, t // FFT_N1, D_INNER), BF16)
    return pl.pallas_call(
        _proj_odd_kernel,
        grid=(bsz, t // tm),
        in_specs=[pl.BlockSpec((1, tm, D_MODEL), row), per_b, per_b,
                  _const_spec((D_MODEL, ODD_IN)), _const_spec((C_GROUP_DIM, 2 * C_GROUP_DIM))],
        out_specs=[perm_spec, perm_spec, pl.BlockSpec((1, tm, D_INNER), row)],
        out_shape=[perm, perm, jax.ShapeDtypeStruct((bsz, t, D_INNER), BF16)],
        scratch_shapes=[pltpu.VMEM((D_MODEL // 128, tm, 128), F32)],
        compiler_params=_params("arbitrary", "arbitrary"),
        name="proj_odd",
    )(x, sh, sc, w_in, cs128)


def _dft4(ar, ai):
    s0r, s0i = ar[0] + ar[2], ai[0] + ai[2]
    s1r, s1i = ar[0] - ar[2], ai[0] - ai[2]
    s2r, s2i = ar[1] + ar[3], ai[1] + ai[3]
    s3r, s3i = ar[1] - ar[3], ai[1] - ai[3]
    return ([s0r + s2r, s1r - s3i, s0r - s2r, s1r + s3i],
            [s0i + s2i, s1i + s3r, s0i - s2i, s1i - s3r])


def _dft8(zr, zi):
    er, ei = _dft4(zr[0::2], zi[0::2])
    orr, oi = _dft4(zr[1::2], zi[1::2])
    h = np.float32(np.sqrt(0.5))
    tr = [orr[0], (orr[1] - oi[1]) * h, -oi[2], (-orr[3] - oi[3]) * h]
    ti = [oi[0], (orr[1] + oi[1]) * h, orr[2], (orr[3] - oi[3]) * h]
    xr = [er[k] + tr[k] for k in range(4)] + [er[k] - tr[k] for k in range(4)]
    xi = [ei[k] + ti[k] for k in range(4)] + [ei[k] - ti[k] for k in range(4)]
    return xr, xi


FFT_COLS = 256
FFT_ROWS = 16


def _fft_t_kernel(a_ref, b_ref, twr_ref, twi_ref, w_ref, f_ref, g_ref):
    half = FFT_N2
    for ch in range(a_ref.shape[3] // FFT_COLS):
        c0 = ch * FFT_COLS
        for t1 in range(FFT_N1):
            g_ref[ch, t1] = (_dot(w_ref[:, :half], a_ref[0, t1, :, c0:c0 + FFT_COLS])
                             + _dot(w_ref[:, half:], b_ref[0, t1, :, c0:c0 + FFT_COLS]))
        for r in range(0, FFT_N2, FFT_ROWS):
            rows = slice(r, r + FFT_ROWS)
            rows_im = slice(half + r, half + r + FFT_ROWS)
            for j in range(FFT_COLS // 128):
                lanes = slice(j * 128, (j + 1) * 128)
                zr = [g_ref[ch, 0, rows, lanes]]
                zi = [g_ref[ch, 0, rows_im, lanes]]
                for t1 in range(1, FFT_N1):
                    gr = g_ref[ch, t1, rows, lanes]
                    gi = g_ref[ch, t1, rows_im, lanes]
                    wr = twr_ref[t1, rows, :]
                    wi = twi_ref[t1, rows, :]
                    zr.append(gr * wr - gi * wi)
                    zi.append(gr * wi + gi * wr)
                xr, _ = _dft8(zr, zi)
                for k1 in range(FFT_N1):
                    f_ref[0, k1 * FFT_N2 + r:k1 * FFT_N2 + r + FFT_ROWS,
                          c0 + j * 128:c0 + (j + 1) * 128] = xr[k1].astype(BF16)


def _fft_t(a, b, twr, twi, w512, tn):
    bsz, _, n2, w = a.shape
    t = FFT_N1 * n2
    blk = pl.BlockSpec((1, FFT_N1, n2, tn), lambda bb, j: (bb, 0, 0, j))
    return pl.pallas_call(
        _fft_t_kernel,
        grid=(bsz, w // tn),
        in_specs=[blk, blk,
                  _const_spec((FFT_N1, FFT_N2, 128)), _const_spec((FFT_N1, FFT_N2, 128)),
                  _const_spec((2 * FFT_N2, 2 * FFT_N2))],
        out_specs=pl.BlockSpec((1, t, tn), lambda bb, j: (bb, 0, j)),
        out_shape=jax.ShapeDtypeStruct((bsz, t, w), BF16),
        scratch_shapes=[pltpu.VMEM((tn // FFT_COLS, FFT_N1, 2 * FFT_N2, FFT_COLS), F32)],
        compiler_params=_params("arbitrary", "arbitrary"),
        name="fft_t",
    )(a, b, twr, twi, w512)


def _odd_out_kernel(f_ref, sg_ref, x_ref, gt_ref, wo_ref, pg_ref, pb_ref, o_ref):
    comb = (f_ref[0].astype(F32) * sg_ref[0].astype(F32)).astype(BF16)
    y = _dot(comb, wo_ref[...])
    o_ref[0] = _deepnorm(x_ref[0], gt_ref[0], y, pg_ref[...], pb_ref[...])


def _odd_out(f, sg, x, gt, w_out, pg, pb, tm):
    bsz, t, _ = x.shape
    row = lambda b, i: (b, i, 0)
    return pl.pallas_call(
        _odd_out_kernel,
        grid=(bsz, t // tm),
        in_specs=[
            pl.BlockSpec((1, tm, D_INNER), row), pl.BlockSpec((1, tm, D_INNER), row),
            pl.BlockSpec((1, tm, D_MODEL), row),
            pl.BlockSpec((1, 1, D_MODEL), lambda b, i: (b, 0, 0)),
            _const_spec((D_INNER, D_MODEL)), _const_spec((1, D_MODEL)), _const_spec((1, D_MODEL)),
        ],
        out_specs=pl.BlockSpec((1, tm, D_MODEL), row),
        out_shape=jax.ShapeDtypeStruct((bsz, t, D_MODEL), F32),
        compiler_params=_params("arbitrary", "arbitrary"),
        name="odd_out",
    )(f, sg, x, gt, w_out, pg, pb)


def _rope_tables(n_tokens):
    rows = n_tokens // GRID_W
    r, cl = jnp.meshgrid(jnp.arange(rows), jnp.arange(GRID_W), indexing="ij")
    row = r.reshape(-1).astype(F32)
    col = cl.reshape(-1).astype(F32)
    n_pairs_axis = HEAD_DIM // 4
    inv_freq = ROPE_THETA ** (-jnp.arange(n_pairs_axis, dtype=F32) / n_pairs_axis)
    ang = jnp.concatenate([row[:, None] * inv_freq, col[:, None] * inv_freq], axis=-1)
    cs, sn = jnp.cos(ang), jnp.sin(ang)
    return jnp.concatenate([cs, cs], axis=-1), jnp.concatenate([-sn, sn], axis=-1)


def _dft_tables(n_tokens):
    assert n_tokens == FFT_N1 * FFT_N2
    c = np.arange(C_GROUP_DIM)
    ang = 2.0 * np.pi * np.outer(c, c) / C_GROUP_DIM
    norm = 1.0 / np.sqrt(float(n_tokens * C_GROUP_DIM))
    cs128 = np.concatenate([np.cos(ang), np.sin(ang)], axis=1) * norm
    t1 = np.arange(FFT_N1)[:, None]
    k2 = np.arange(FFT_N2)[None, :]
    tw = 2.0 * np.pi * t1 * k2 / n_tokens
    twr = np.broadcast_to(np.cos(tw)[:, :, None], (FFT_N1, FFT_N2, 128))
    twi = np.broadcast_to(np.sin(tw)[:, :, None], (FFT_N1, FFT_N2, 128))
    kk = np.arange(FFT_N2)
    ang2 = 2.0 * np.pi * np.outer(kk, kk) / FFT_N2
    c256, s256 = np.cos(ang2), np.sin(ang2)
    w512 = np.block([[c256, -s256], [s256, c256]])
    f = lambda a: jnp.asarray(np.ascontiguousarray(a), dtype=F32)
    return f(cs128).astype(BF16), f(twr), f(twi), f(w512).astype(BF16)


def kernel(x, c, ctx, c_ctx, w_mod, b_mod, post_ln_g, post_ln_b, even_w_in, even_q_norm, even_k_norm,
           even_v_ln_g, even_v_ln_b, even_w_s, even_b_s, even_w_out, odd_w_in, odd_w_out):
    bsz, t, _ = x.shape
    assert DEPTH == 2 and t % CHUNK == 0
    row1 = lambda v: v.reshape(1, -1)

    n_cond = -(-(bsz + 1) // 8) * 8
    cond = jnp.zeros((n_cond, D_MODEL), F32).at[:bsz].set(c).at[bsz].set(c_ctx)
    mod = _adaln(cond, w_mod, b_mod)
    split = lambda l, rows: [mod[l, rows, i * D_MODEL:(i + 1) * D_MODEL] for i in range(3)]
    sh0, sc0, gt0 = [v[:, None, :] for v in split(0, slice(0, bsz))]
    sh0c, sc0c, _ = split(0, slice(bsz, bsz + 1))
    sh1, sc1, gt1 = [v[:, None, :] for v in split(1, slice(0, bsz))]

    cos2, sin2 = _rope_tables(t)
    cs128, twr, twi, w512 = _dft_tables(t)

    w_in0 = even_w_in[0].astype(BF16)
    q, k, v, u, vn, sg = _proj_even(x, sh0, sc0, w_in0, row1(even_q_norm[0]), row1(even_k_norm[0]),
                                    row1(even_v_ln_g[0]), row1(even_v_ln_b[0]), cos2, sin2, tm=512)
    kc, vc = _ctx_kv(ctx, sh0c, sc0c, w_in0, row1(even_k_norm[0]))
    a = _attention(q, kc, vc, k, v, tq=512)
    b_s = jnp.broadcast_to(even_b_s[0][:, :, None], (B_GROUPS, CHUNK, B_GROUP_DIM))
    x1 = _even_out(a, u, vn, sg, x, gt0, even_w_s[0].astype(BF16), b_s, even_w_out[0].astype(BF16),
                   row1(post_ln_g[0]), row1(post_ln_b[0]), tm=256)

    fa, fb, sg1 = _proj_odd(x1, sh1, sc1, odd_w_in[0].astype(BF16), cs128, tm=512)
    f = _fft_t(fa, fb, twr, twi, w512, tn=1024)
    return _odd_out(f, sg1, x1, gt1, odd_w_out[0].astype(BF16), row1(post_ln_g[1]), row1(post_ln_b[1]),
                    tm=512)
```

```python
import functools

import numpy as np
import jax
import jax.numpy as jnp
from jax import lax
from jax.experimental import pallas as pl
from jax.experimental.pallas import tpu as pltpu

D_MODEL = 1024
DEPTH = 2
GRID_W = 64
D_INNER = 2 * D_MODEL
HEAD_DIM = 128
A_WIDTH = D_INNER // 2
N_Q_HEADS = A_WIDTH // HEAD_DIM
N_KV_HEADS = 2
Q_PER_KV = N_Q_HEADS // N_KV_HEADS
KV_WIDTH = N_KV_HEADS * HEAD_DIM
B_WIDTH = D_INNER - A_WIDTH
CHUNK = 128
B_GROUP_DIM = 128
B_GROUPS = B_WIDTH // B_GROUP_DIM
C_GROUP_DIM = 128
C_GROUPS = D_INNER // C_GROUP_DIM
ROPE_THETA = 10000.0
EVEN_IN = A_WIDTH + 2 * KV_WIDTH + 2 * B_WIDTH + D_INNER
ODD_IN = 2 * D_INNER
ALPHA = (2 * DEPTH) ** 0.25
EPS = 1e-6

_Q0, _K0, _V0 = 0, A_WIDTH, A_WIDTH + KV_WIDTH
_U0 = A_WIDTH + 2 * KV_WIDTH
_BV0 = _U0 + B_WIDTH
_G0 = _BV0 + B_WIDTH

FFT_N1 = 8
FFT_N2 = 256

V7X_VMEM_LIMIT_BYTES = 60000 * 1024

F32 = jnp.float32
BF16 = jnp.bfloat16


def _dot(a, b):
    return jnp.dot(a, b, preferred_element_type=F32)


def _dot_nt(a, b):
    return lax.dot_general(a, b, (((1,), (1,)), ((), ())), preferred_element_type=F32)


def _ln(x):
    mu = jnp.mean(x, axis=-1, keepdims=True)
    xc = x - mu
    var = jnp.mean(xc * xc, axis=-1, keepdims=True)
    return xc * lax.rsqrt(var + EPS)


def _rms_head(z, g):
    return z * lax.rsqrt(jnp.mean(z * z, axis=-1, keepdims=True) + EPS) * g


def _rope(y, cos2, sin2):
    return y * cos2 + pltpu.roll(y, HEAD_DIM // 2, 1) * sin2


def _params(*sem):
    return pltpu.CompilerParams(dimension_semantics=sem, vmem_limit_bytes=V7X_VMEM_LIMIT_BYTES)


def _const_spec(shape):
    nd = len(shape)
    return pl.BlockSpec(shape, lambda *_: (0,) * nd, pipeline_mode=pl.Buffered(1))


def _adaln_kernel(c_ref, w_ref, b_ref, o_ref):
    h = jax.nn.silu(c_ref[...])
    w = w_ref[0]
    h_hi = h.astype(BF16)
    h_lo = (h - h_hi.astype(F32)).astype(BF16)
    w_hi = w.astype(BF16)
    w_lo = (w - w_hi.astype(F32)).astype(BF16)
    o_ref[0] = _dot(h_hi, w_hi) + _dot(h_hi, w_lo) + _dot(h_lo, w_hi) + b_ref[0]


def _adaln(cond, w_mod, b_mod):
    r = cond.shape[0]
    tn = D_MODEL
    return pl.pallas_call(
        _adaln_kernel,
        grid=(DEPTH, 3 * D_MODEL // tn),
        in_specs=[
            pl.BlockSpec((r, D_MODEL), lambda l, j: (0, 0)),
            pl.BlockSpec((1, D_MODEL, tn), lambda l, j: (l, 0, j)),
            pl.BlockSpec((1, 1, tn), lambda l, j: (l, 0, j)),
        ],
        out_specs=pl.BlockSpec((1, r, tn), lambda l, j: (l, 0, j)),
        out_shape=jax.ShapeDtypeStruct((DEPTH, r, 3 * D_MODEL), F32),
        compiler_params=_params("arbitrary", "arbitrary"),
        name="adaln",
    )(cond, w_mod, b_mod.reshape(DEPTH, 1, 3 * D_MODEL))


def _proj_even_kernel(x_ref, sh_ref, sc_ref, w_ref, qg_ref, kg_ref, vg_ref, vb_ref, cos_ref, sin_ref,
                      q_ref, k_ref, v_ref, u_ref, vn_ref, sg_ref):
    qg = qg_ref[...] * (HEAD_DIM ** -0.5)
    kg = kg_ref[...]
    cw = 4 * HEAD_DIM
    for r0 in range(0, x_ref.shape[1], ROW_CHAIN):
        rows = slice(r0, r0 + ROW_CHAIN)
        m = (_ln(x_ref[0, rows]) * (1.0 + sc_ref[0]) + sh_ref[0]).astype(BF16)
        cos2 = cos_ref[rows]
        sin2 = sin_ref[rows]
        for c in range(A_WIDTH // cw):
            z = _dot(m, w_ref[:, _Q0 + c * cw:_Q0 + (c + 1) * cw])
            for j in range(cw // HEAD_DIM):
                zh = z[:, j * HEAD_DIM:(j + 1) * HEAD_DIM]
                q_ref[0, c * (cw // HEAD_DIM) + j, rows] = _rope(_rms_head(zh, qg), cos2, sin2).astype(BF16)
        z = _dot(m, w_ref[:, _K0:_K0 + 2 * KV_WIDTH])
        for j in range(N_KV_HEADS):
            zh = z[:, j * HEAD_DIM:(j + 1) * HEAD_DIM]
            k_ref[0, rows, j * HEAD_DIM:(j + 1) * HEAD_DIM] = _rope(_rms_head(zh, kg), cos2, sin2).astype(BF16)
        v_ref[0, rows] = z[:, KV_WIDTH:].astype(BF16)
        for c in range(B_WIDTH // cw):
            z = _dot(m, w_ref[:, _U0 + c * cw:_U0 + (c + 1) * cw])
            u_ref[0, rows, c * cw:(c + 1) * cw] = jax.nn.gelu(z).astype(BF16)
        gv = jax.nn.gelu(_dot(m, w_ref[:, _BV0:_BV0 + B_WIDTH]))
        vn_ref[0, rows] = (_ln(gv) * vg_ref[...] + vb_ref[...]).astype(BF16)
        for c in range(D_INNER // cw):
            z = _dot(m, w_ref[:, _G0 + c * cw:_G0 + (c + 1) * cw])
            sg_ref[0, rows, c * cw:(c + 1) * cw] = jax.nn.silu(z).astype(BF16)


def _proj_even(x, sh, sc, w_in, q_g, k_g, v_g, v_b, cos2, sin2, tm):
    bsz, t, _ = x.shape
    row = lambda b, i: (b, i, 0)
    per_b = pl.BlockSpec((1, 1, D_MODEL), lambda b, i: (b, 0, 0))
    bf = lambda *s: jax.ShapeDtypeStruct(s, BF16)
    return pl.pallas_call(
        _proj_even_kernel,
        grid=(bsz, t // tm),
        in_specs=[
            pl.BlockSpec((1, tm, D_MODEL), row), per_b, per_b,
            _const_spec((D_MODEL, EVEN_IN)),
            _const_spec((1, HEAD_DIM)), _const_spec((1, HEAD_DIM)),
            _const_spec((1, B_WIDTH)), _const_spec((1, B_WIDTH)),
            pl.BlockSpec((tm, HEAD_DIM), lambda b, i: (i, 0)),
            pl.BlockSpec((tm, HEAD_DIM), lambda b, i: (i, 0)),
        ],
        out_specs=[
            pl.BlockSpec((1, N_Q_HEADS, tm, HEAD_DIM), lambda b, i: (b, 0, i, 0)),
            pl.BlockSpec((1, tm, KV_WIDTH), row), pl.BlockSpec((1, tm, KV_WIDTH), row),
            pl.BlockSpec((1, tm, B_WIDTH), row), pl.BlockSpec((1, tm, B_WIDTH), row),
            pl.BlockSpec((1, tm, D_INNER), row),
        ],
        out_shape=[bf(bsz, N_Q_HEADS, t, HEAD_DIM), bf(bsz, t, KV_WIDTH), bf(bsz, t, KV_WIDTH),
                   bf(bsz, t, B_WIDTH), bf(bsz, t, B_WIDTH), bf(bsz, t, D_INNER)],
        compiler_params=_params("arbitrary", "arbitrary"),
        name="proj_even",
    )(x, sh, sc, w_in, q_g, k_g, v_g, v_b, cos2, sin2)


def _ctx_kv_kernel(c_ref, sh_ref, sc_ref, w_ref, kg_ref, kc_ref, vc_ref):
    m = (_ln(c_ref[0]) * (1.0 + sc_ref[...]) + sh_ref[...]).astype(BF16)
    z = _dot(m, w_ref[...])
    for j in range(N_KV_HEADS):
        zh = z[:, j * HEAD_DIM:(j + 1) * HEAD_DIM]
        kc_ref[0, :, j * HEAD_DIM:(j + 1) * HEAD_DIM] = _rms_head(zh, kg_ref[...]).astype(BF16)
    vc_ref[0] = z[:, KV_WIDTH:].astype(BF16)


def _ctx_kv(ctx, sh_c, sc_c, w_in, k_g):
    bsz, s, _ = ctx.shape
    assert _K0 % (2 * KV_WIDTH) == 0
    return pl.pallas_call(
        _ctx_kv_kernel,
        grid=(bsz,),
        in_specs=[
            pl.BlockSpec((1, s, D_MODEL), lambda b: (b, 0, 0)),
            pl.BlockSpec((1, D_MODEL), lambda b: (0, 0)), pl.BlockSpec((1, D_MODEL), lambda b: (0, 0)),
            pl.BlockSpec((D_MODEL, 2 * KV_WIDTH), lambda b: (0, _K0 // (2 * KV_WIDTH))),
            pl.BlockSpec((1, HEAD_DIM), lambda b: (0, 0)),
        ],
        out_specs=[pl.BlockSpec((1, s, KV_WIDTH), lambda b: (b, 0, 0))] * 2,
        out_shape=[jax.ShapeDtypeStruct((bsz, s, KV_WIDTH), BF16)] * 2,
        compiler_params=_params("arbitrary"),
        name="ctx_kv",
    )(ctx, sh_c, sc_c, w_in, k_g)


def _attn_kernel(q_ref, kc_ref, vc_ref, k_ref, v_ref, a_ref):
    for r0 in range(0, q_ref.shape[2], ATTN_ROWS):
        rows = slice(r0, r0 + ATTN_ROWS)
        for j in range(q_ref.shape[1]):
            q = q_ref[0, j, rows]
            s_c = _dot_nt(q, kc_ref[0])
            s_x = _dot_nt(q, k_ref[0])
            mx = jnp.maximum(jnp.max(s_c, axis=-1, keepdims=True), jnp.max(s_x, axis=-1, keepdims=True))
            p_c = jnp.exp(s_c - mx)
            p_x = jnp.exp(s_x - mx)
            den = jnp.sum(p_c, axis=-1, keepdims=True) + jnp.sum(p_x, axis=-1, keepdims=True)
            o = (_dot(p_c.astype(BF16), vc_ref[0]) + _dot(p_x.astype(BF16), v_ref[0])) / den
            a_ref[0, rows, j * HEAD_DIM:(j + 1) * HEAD_DIM] = o.astype(BF16)


def _attention(q, kc, vc, k, v, tq):
    bsz, _, t, _ = q.shape
    s = kc.shape[1]
    kv_c = pl.BlockSpec((1, s, HEAD_DIM), lambda b, h, i: (b, 0, h))
    kv_x = pl.BlockSpec((1, t, HEAD_DIM), lambda b, h, i: (b, 0, h))
    return pl.pallas_call(
        _attn_kernel,
        grid=(bsz, N_KV_HEADS, t // tq),
        in_specs=[pl.BlockSpec((1, Q_PER_KV, tq, HEAD_DIM), lambda b, h, i: (b, h, i, 0)),
                  kv_c, kv_c, kv_x, kv_x],
        out_specs=pl.BlockSpec((1, tq, Q_PER_KV * HEAD_DIM), lambda b, h, i: (b, i, h)),
        out_shape=jax.ShapeDtypeStruct((bsz, t, A_WIDTH), BF16),
        compiler_params=_params("arbitrary", "arbitrary", "arbitrary"),
        name="attention",
    )(q, kc, vc, k, v)


def _deepnorm(x, gt, y, pg, pb):
    return _ln(ALPHA * x + gt * y) * pg + pb


def _even_out_kernel(a_ref, u_ref, vn_ref, sg_ref, x_ref, gt_ref, ws_ref, bs_ref, wo_ref, pg_ref, pb_ref,
                     o_ref, comb_ref):
    for r0 in range(0, a_ref.shape[1], ROW_CHAIN):
        rc = slice(r0, r0 + ROW_CHAIN)
        comb_ref[rc, :A_WIDTH] = a_ref[0, rc] * sg_ref[0, rc, :A_WIDTH]
        for n in range(ROW_CHAIN // CHUNK):
            rows = slice(r0 + n * CHUNK, r0 + (n + 1) * CHUNK)
            for g in range(B_GROUPS):
                cols = slice(g * B_GROUP_DIM, (g + 1) * B_GROUP_DIM)
                gcols = slice(A_WIDTH + g * B_GROUP_DIM, A_WIDTH + (g + 1) * B_GROUP_DIM)
                mixed = _dot(ws_ref[g], vn_ref[0, rows, cols]) + bs_ref[g]
                comb_ref[rows, gcols] = (u_ref[0, rows, cols].astype(F32) * mixed
                                         * sg_ref[0, rows, gcols].astype(F32)).astype(BF16)
        y = _dot(comb_ref[rc], wo_ref[...])
        o_ref[0, rc] = _deepnorm(x_ref[0, rc], gt_ref[0], y, pg_ref[...], pb_ref[...])


def _even_out(a, u, vn, sg, x, gt, w_s, b_s, w_out, pg, pb, tm):
    bsz, t, _ = x.shape
    row = lambda b, i: (b, i, 0)
    return pl.pallas_call(
        _even_out_kernel,
        grid=(bsz, t // tm),
        in_specs=[
            pl.BlockSpec((1, tm, A_WIDTH), row), pl.BlockSpec((1, tm, B_WIDTH), row),
            pl.BlockSpec((1, tm, B_WIDTH), row), pl.BlockSpec((1, tm, D_INNER), row),
            pl.BlockSpec((1, tm, D_MODEL), row),
            pl.BlockSpec((1, 1, D_MODEL), lambda b, i: (b, 0, 0)),
            _const_spec((B_GROUPS, CHUNK, CHUNK)), _const_spec((B_GROUPS, CHUNK, B_GROUP_DIM)),
            _const_spec((D_INNER, D_MODEL)),
            _const_spec((1, D_MODEL)), _const_spec((1, D_MODEL)),
        ],
        out_specs=pl.BlockSpec((1, tm, D_MODEL), row),
        out_shape=jax.ShapeDtypeStruct((bsz, t, D_MODEL), F32),
        scratch_shapes=[pltpu.VMEM((tm, D_INNER), BF16)],
        compiler_params=_params("arbitrary", "arbitrary"),
        name="even_out",
    )(a, u, vn, sg, x, gt, w_s, b_s, w_out, pg, pb)


def _proj_odd_kernel(x_ref, sh_ref, sc_ref, w_ref, cs_ref, a_ref, b_ref, sg_ref, m_ref):
    tm = x_ref.shape[1]
    n = tm // FFT_N1
    m32 = _ln(x_ref[0]) * (1.0 + sc_ref[0]) + sh_ref[0]
    m = m32.astype(BF16)
    n_slab = D_MODEL // 128
    for j in range(n_slab):
        m_ref[j] = m32[:, j * 128:(j + 1) * 128]
    mp = jnp.concatenate(
        [jnp.concatenate([m_ref[j, pl.ds(t1, n, stride=FFT_N1), :] for j in range(n_slab)], axis=1)
         for t1 in range(FFT_N1)], axis=0).astype(BF16)
    cw = 4 * C_GROUP_DIM
    for c in range(D_INNER // cw):
        z = _dot(mp, w_ref[:, c * cw:(c + 1) * cw]).astype(BF16)
        for j in range(cw // C_GROUP_DIM):
            cols = slice(c * cw + j * C_GROUP_DIM, c * cw + (j + 1) * C_GROUP_DIM)
            ab = _dot(z[:, j * C_GROUP_DIM:(j + 1) * C_GROUP_DIM], cs_ref[...])
            for t1 in range(FFT_N1):
                a_ref[0, t1, :, cols] = ab[t1 * n:(t1 + 1) * n, :C_GROUP_DIM].astype(BF16)
                b_ref[0, t1, :, cols] = ab[t1 * n:(t1 + 1) * n, C_GROUP_DIM:].astype(BF16)
    for c in range(D_INNER // cw):
        z = _dot(m, w_ref[:, D_INNER + c * cw:D_INNER + (c + 1) * cw])
        sg_ref[0, :, c * cw:(c + 1) * cw] = jax.nn.silu(z).astype(BF16)


def _proj_odd(x, sh, sc, w_in, cs128, tm):
    bsz, t, _ = x.shape
    row = lambda b, i: (b, i, 0)
    per_b = pl.BlockSpec((1, 1, D_MODEL), lambda b, i: (b, 0, 0))
    n = tm // FFT_N1
    perm_spec = pl.BlockSpec((1, FFT_N1, n, D_INNER), lambda b, i: (b, 0, i, 0))
    perm = jax.ShapeDtypeStruct((bsz, FFT_N1, t // FFT_N1, D_INNER), BF16)
    return pl.pallas_call(
        _proj_odd_kernel,
        grid=(bsz, t // tm),
        in_specs=[pl.BlockSpec((1, tm, D_MODEL), row), per_b, per_b,
                  _const_spec((D_MODEL, ODD_IN)), _const_spec((C_GROUP_DIM, 2 * C_GROUP_DIM))],
        out_specs=[perm_spec, perm_spec, pl.BlockSpec((1, tm, D_INNER), row)],
        out_shape=[perm, perm, jax.ShapeDtypeStruct((bsz, t, D_INNER), BF16)],
        scratch_shapes=[pltpu.VMEM((D_MODEL // 128, tm, 128), F32)],
        compiler_params=_params("arbitrary", "arbitrary"),
        name="proj_odd",
    )(x, sh, sc, w_in, cs128)


def _dft4(ar, ai):
    s0r, s0i = ar[0] + ar[2], ai[0] + ai[2]
    s1r, s1i = ar[0] - ar[2], ai[0] - ai[2]
    s2r, s2i = ar[1] + ar[3], ai[1] + ai[3]
    s3r, s3i = ar[1] - ar[3], ai[1] - ai[3]
    return ([s0r + s2r, s1r - s3i, s0r - s2r, s1r + s3i],
            [s0i + s2i, s1i + s3r, s0i - s2i, s1i - s3r])


def _dft8(zr, zi):
    er, ei = _dft4(zr[0::2], zi[0::2])
    orr, oi = _dft4(zr[1::2], zi[1::2])
    h = np.float32(np.sqrt(0.5))
    tr = [orr[0], (orr[1] - oi[1]) * h, -oi[2], (-orr[3] - oi[3]) * h]
    ti = [oi[0], (orr[1] + oi[1]) * h, orr[2], (orr[3] - oi[3]) * h]
    xr = [er[k] + tr[k] for k in range(4)] + [er[k] - tr[k] for k in range(4)]
    xi = [ei[k] + ti[k] for k in range(4)] + [ei[k] - ti[k] for k in range(4)]
    return xr, xi


ROW_CHAIN = 256
ATTN_ROWS = 512
FFT_COLS = 256
FFT_ROWS = 16


def _fft_t_kernel(a_ref, b_ref, twr_ref, twi_ref, w_ref, f_ref, g_ref):
    half = FFT_N2
    for ch in range(a_ref.shape[3] // FFT_COLS):
        c0 = ch * FFT_COLS
        for t1 in range(FFT_N1):
            g_ref[ch, t1] = (_dot(w_ref[:, :half], a_ref[0, t1, :, c0:c0 + FFT_COLS])
                             + _dot(w_ref[:, half:], b_ref[0, t1, :, c0:c0 + FFT_COLS]))
        for r in range(0, FFT_N2, FFT_ROWS):
            rows = slice(r, r + FFT_ROWS)
            rows_im = slice(half + r, half + r + FFT_ROWS)
            for j in range(FFT_COLS // 128):
                lanes = slice(j * 128, (j + 1) * 128)
                zr = [g_ref[ch, 0, rows, lanes]]
                zi = [g_ref[ch, 0, rows_im, lanes]]
                for t1 in range(1, FFT_N1):
                    gr = g_ref[ch, t1, rows, lanes]
                    gi = g_ref[ch, t1, rows_im, lanes]
                    wr = twr_ref[t1, rows, :]
                    wi = twi_ref[t1, rows, :]
                    zr.append(gr * wr - gi * wi)
                    zi.append(gr * wi + gi * wr)
                xr, _ = _dft8(zr, zi)
                for k1 in range(FFT_N1):
                    f_ref[0, k1 * FFT_N2 + r:k1 * FFT_N2 + r + FFT_ROWS,
                          c0 + j * 128:c0 + (j + 1) * 128] = xr[k1].astype(BF16)


def _fft_t(a, b, twr, twi, w512, tn):
    bsz, _, n2, w = a.shape
    t = FFT_N1 * n2
    blk = pl.BlockSpec((1, FFT_N1, n2, tn), lambda bb, j: (bb, 0, 0, j))
    return pl.pallas_call(
        _fft_t_kernel,
        grid=(bsz, w // tn),
        in_specs=[blk, blk,
                  _const_spec((FFT_N1, FFT_N2, 128)), _const_spec((FFT_N1, FFT_N2, 128)),
                  _const_spec((2 * FFT_N2, 2 * FFT_N2))],
        out_specs=pl.BlockSpec((1, t, tn), lambda bb, j: (bb, 0, j)),
        out_shape=jax.ShapeDtypeStruct((bsz, t, w), BF16),
        scratch_shapes=[pltpu.VMEM((tn // FFT_COLS, FFT_N1, 2 * FFT_N2, FFT_COLS), F32)],
        compiler_params=_params("arbitrary", "arbitrary"),
        name="fft_t",
    )(a, b, twr, twi, w512)


def _odd_out_kernel(f_ref, sg_ref, x_ref, gt_ref, wo_ref, pg_ref, pb_ref, o_ref):
    for r0 in range(0, x_ref.shape[1], ROW_CHAIN):
        rows = slice(r0, r0 + ROW_CHAIN)
        y = _dot(f_ref[0, rows] * sg_ref[0, rows], wo_ref[...])
        o_ref[0, rows] = _deepnorm(x_ref[0, rows], gt_ref[0], y, pg_ref[...], pb_ref[...])


def _odd_out(f, sg, x, gt, w_out, pg, pb, tm):
    bsz, t, _ = x.shape
    row = lambda b, i: (b, i, 0)
    return pl.pallas_call(
        _odd_out_kernel,
        grid=(bsz, t // tm),
        in_specs=[
            pl.BlockSpec((1, tm, D_INNER), row), pl.BlockSpec((1, tm, D_INNER), row),
            pl.BlockSpec((1, tm, D_MODEL), row),
            pl.BlockSpec((1, 1, D_MODEL), lambda b, i: (b, 0, 0)),
            _const_spec((D_INNER, D_MODEL)), _const_spec((1, D_MODEL)), _const_spec((1, D_MODEL)),
        ],
        out_specs=pl.BlockSpec((1, tm, D_MODEL), row),
        out_shape=jax.ShapeDtypeStruct((bsz, t, D_MODEL), F32),
        compiler_params=_params("arbitrary", "arbitrary"),
        name="odd_out",
    )(f, sg, x, gt, w_out, pg, pb)


def _rope_tables(n_tokens):
    rows = n_tokens // GRID_W
    r, cl = jnp.meshgrid(jnp.arange(rows), jnp.arange(GRID_W), indexing="ij")
    row = r.reshape(-1).astype(F32)
    col = cl.reshape(-1).astype(F32)
    n_pairs_axis = HEAD_DIM // 4
    inv_freq = ROPE_THETA ** (-jnp.arange(n_pairs_axis, dtype=F32) / n_pairs_axis)
    ang = jnp.concatenate([row[:, None] * inv_freq, col[:, None] * inv_freq], axis=-1)
    cs, sn = jnp.cos(ang), jnp.sin(ang)
    return jnp.concatenate([cs, cs], axis=-1), jnp.concatenate([-sn, sn], axis=-1)


def _dft_tables(n_tokens):
    assert n_tokens == FFT_N1 * FFT_N2
    c = np.arange(C_GROUP_DIM)
    ang = 2.0 * np.pi * np.outer(c, c) / C_GROUP_DIM
    norm = 1.0 / np.sqrt(float(n_tokens * C_GROUP_DIM))
    cs128 = np.concatenate([np.cos(ang), np.sin(ang)], axis=1) * norm
    t1 = np.arange(FFT_N1)[:, None]
    k2 = np.arange(FFT_N2)[None, :]
    tw = 2.0 * np.pi * t1 * k2 / n_tokens
    twr = np.broadcast_to(np.cos(tw)[:, :, None], (FFT_N1, FFT_N2, 128))
    twi = np.broadcast_to(np.sin(tw)[:, :, None], (FFT_N1, FFT_N2, 128))
    kk = np.arange(FFT_N2)
    ang2 = 2.0 * np.pi * np.outer(kk, kk) / FFT_N2
    c256, s256 = np.cos(ang2), np.sin(ang2)
    w512 = np.block([[c256, -s256], [s256, c256]])
    f = lambda a: jnp.asarray(np.ascontiguousarray(a), dtype=F32)
    return f(cs128).astype(BF16), f(twr), f(twi), f(w512).astype(BF16)


def kernel(x, c, ctx, c_ctx, w_mod, b_mod, post_ln_g, post_ln_b, even_w_in, even_q_norm, even_k_norm,
           even_v_ln_g, even_v_ln_b, even_w_s, even_b_s, even_w_out, odd_w_in, odd_w_out):
    bsz, t, _ = x.shape
    assert DEPTH == 2 and t % CHUNK == 0
    row1 = lambda v: v.reshape(1, -1)

    n_cond = -(-(bsz + 1) // 8) * 8
    cond = jnp.zeros((n_cond, D_MODEL), F32).at[:bsz].set(c).at[bsz].set(c_ctx)
    mod = _adaln(cond, w_mod, b_mod)
    split = lambda l, rows: [mod[l, rows, i * D_MODEL:(i + 1) * D_MODEL] for i in range(3)]
    sh0, sc0, gt0 = [v[:, None, :] for v in split(0, slice(0, bsz))]
    sh0c, sc0c, _ = split(0, slice(bsz, bsz + 1))
    sh1, sc1, gt1 = [v[:, None, :] for v in split(1, slice(0, bsz))]

    cos2, sin2 = _rope_tables(t)
    cs128, twr, twi, w512 = _dft_tables(t)

    w_in0 = even_w_in[0].astype(BF16)
    q, k, v, u, vn, sg = _proj_even(x, sh0, sc0, w_in0, row1(even_q_norm[0]), row1(even_k_norm[0]),
                                    row1(even_v_ln_g[0]), row1(even_v_ln_b[0]), cos2, sin2, tm=512)
    kc, vc = _ctx_kv(ctx, sh0c, sc0c, w_in0, row1(even_k_norm[0]))
    a = _attention(q, kc, vc, k, v, tq=1024)
    b_s = jnp.broadcast_to(even_b_s[0][:, :, None], (B_GROUPS, CHUNK, B_GROUP_DIM))
    x1 = _even_out(a, u, vn, sg, x, gt0, even_w_s[0].astype(BF16), b_s, even_w_out[0].astype(BF16),
                   row1(post_ln_g[0]), row1(post_ln_b[0]), tm=512)

    fa, fb, sg1 = _proj_odd(x1, sh1, sc1, odd_w_in[0].astype(BF16), cs128, tm=512)
    f = _fft_t(fa, fb, twr, twi, w512, tn=1024)
    return _odd_out(f, sg1, x1, gt1, odd_w_out[0].astype(BF16), row1(post_ln_g[1]), row1(post_ln_b[1]),
                    tm=512)
```

```python
import functools

import numpy as np
import jax
import jax.numpy as jnp
from jax import lax
from jax.experimental import pallas as pl
from jax.experimental.pallas import tpu as pltpu

D_MODEL = 1024
DEPTH = 2
GRID_W = 64
D_INNER = 2 * D_MODEL
HEAD_DIM = 128
A_WIDTH = D_INNER // 2
N_Q_HEADS = A_WIDTH // HEAD_DIM
N_KV_HEADS = 2
Q_PER_KV = N_Q_HEADS // N_KV_HEADS
KV_WIDTH = N_KV_HEADS * HEAD_DIM
B_WIDTH = D_INNER - A_WIDTH
CHUNK = 128
B_GROUP_DIM = 128
B_GROUPS = B_WIDTH // B_GROUP_DIM
C_GROUP_DIM = 128
C_GROUPS = D_INNER // C_GROUP_DIM
ROPE_THETA = 10000.0
EVEN_IN = A_WIDTH + 2 * KV_WIDTH + 2 * B_WIDTH + D_INNER
ODD_IN = 2 * D_INNER
ALPHA = (2 * DEPTH) ** 0.25
EPS = 1e-6

_Q0, _K0, _V0 = 0, A_WIDTH, A_WIDTH + KV_WIDTH
_U0 = A_WIDTH + 2 * KV_WIDTH
_BV0 = _U0 + B_WIDTH
_G0 = _BV0 + B_WIDTH

FFT_N1 = 8
FFT_N2 = 256

V7X_VMEM_LIMIT_BYTES = 60000 * 1024

F32 = jnp.float32
BF16 = jnp.bfloat16


def _dot(a, b):
    return jnp.dot(a, b, preferred_element_type=F32)


def _dot_nt(a, b):
    return lax.dot_general(a, b, (((1,), (1,)), ((), ())), preferred_element_type=F32)


def _ln(x):
    mu = jnp.mean(x, axis=-1, keepdims=True)
    xc = x - mu
    var = jnp.mean(xc * xc, axis=-1, keepdims=True)
    return xc * lax.rsqrt(var + EPS)


def _rms_head(z, g):
    return z * lax.rsqrt(jnp.mean(z * z, axis=-1, keepdims=True) + EPS) * g


def _rope(y, cos2, sin2):
    return y * cos2 + pltpu.roll(y, HEAD_DIM // 2, 1) * sin2


def _params(*sem):
    return pltpu.CompilerParams(dimension_semantics=sem, vmem_limit_bytes=V7X_VMEM_LIMIT_BYTES)


def _const_spec(shape):
    nd = len(shape)
    return pl.BlockSpec(shape, lambda *_: (0,) * nd, pipeline_mode=pl.Buffered(1))


def _adaln_kernel(c_ref, w_ref, b_ref, o_ref):
    h = jax.nn.silu(c_ref[...])
    w = w_ref[0]
    h_hi = h.astype(BF16)
    h_lo = (h - h_hi.astype(F32)).astype(BF16)
    w_hi = w.astype(BF16)
    w_lo = (w - w_hi.astype(F32)).astype(BF16)
    o_ref[0] = _dot(h_hi, w_hi) + _dot(h_hi, w_lo) + _dot(h_lo, w_hi) + b_ref[0]


def _adaln(cond, w_mod, b_mod):
    r = cond.shape[0]
    tn = D_MODEL
    return pl.pallas_call(
        _adaln_kernel,
        grid=(DEPTH, 3 * D_MODEL // tn),
        in_specs=[
            pl.BlockSpec((r, D_MODEL), lambda l, j: (0, 0)),
            pl.BlockSpec((1, D_MODEL, tn), lambda l, j: (l, 0, j)),
            pl.BlockSpec((1, 1, tn), lambda l, j: (l, 0, j)),
        ],
        out_specs=pl.BlockSpec((1, r, tn), lambda l, j: (l, 0, j)),
        out_shape=jax.ShapeDtypeStruct((DEPTH, r, 3 * D_MODEL), F32),
        compiler_params=_params("arbitrary", "arbitrary"),
        name="adaln",
    )(cond, w_mod, b_mod.reshape(DEPTH, 1, 3 * D_MODEL))


def _proj_even_kernel(x_ref, sh_ref, sc_ref, w_ref, qg_ref, kg_ref, vg_ref, vb_ref, cos_ref, sin_ref,
                      q_ref, k_ref, v_ref, u_ref, vn_ref, sg_ref):
    qg = qg_ref[...] * (HEAD_DIM ** -0.5)
    kg = kg_ref[...]
    cw = 4 * HEAD_DIM
    for r0 in range(0, x_ref.shape[1], ROW_CHAIN):
        rows = slice(r0, r0 + ROW_CHAIN)
        m = (_ln(x_ref[0, rows]) * (1.0 + sc_ref[0]) + sh_ref[0]).astype(BF16)
        cos2 = cos_ref[rows]
        sin2 = sin_ref[rows]
        for c in range(A_WIDTH // cw):
            z = _dot(m, w_ref[:, _Q0 + c * cw:_Q0 + (c + 1) * cw])
            for j in range(cw // HEAD_DIM):
                zh = z[:, j * HEAD_DIM:(j + 1) * HEAD_DIM]
                q_ref[0, c * (cw // HEAD_DIM) + j, rows] = _rope(_rms_head(zh, qg), cos2, sin2).astype(BF16)
        z = _dot(m, w_ref[:, _K0:_K0 + 2 * KV_WIDTH])
        for j in range(N_KV_HEADS):
            zh = z[:, j * HEAD_DIM:(j + 1) * HEAD_DIM]
            k_ref[0, rows, j * HEAD_DIM:(j + 1) * HEAD_DIM] = _rope(_rms_head(zh, kg), cos2, sin2).astype(BF16)
        v_ref[0, rows] = z[:, KV_WIDTH:].astype(BF16)
        for c in range(B_WIDTH // cw):
            z = _dot(m, w_ref[:, _U0 + c * cw:_U0 + (c + 1) * cw])
            u_ref[0, rows, c * cw:(c + 1) * cw] = jax.nn.gelu(z).astype(BF16)
        gv = jax.nn.gelu(_dot(m, w_ref[:, _BV0:_BV0 + B_WIDTH]))
        vn_ref[0, rows] = (_ln(gv) * vg_ref[...] + vb_ref[...]).astype(BF16)
        for c in range(D_INNER // cw):
            z = _dot(m, w_ref[:, _G0 + c * cw:_G0 + (c + 1) * cw])
            sg_ref[0, rows, c * cw:(c + 1) * cw] = jax.nn.silu(z).astype(BF16)


def _proj_even(x, sh, sc, w_in, q_g, k_g, v_g, v_b, cos2, sin2, tm):
    bsz, t, _ = x.shape
    row = lambda b, i: (b, i, 0)
    per_b = pl.BlockSpec((1, 1, D_MODEL), lambda b, i: (b, 0, 0))
    bf = lambda *s: jax.ShapeDtypeStruct(s, BF16)
    return pl.pallas_call(
        _proj_even_kernel,
        grid=(bsz, t // tm),
        in_specs=[
            pl.BlockSpec((1, tm, D_MODEL), row), per_b, per_b,
            _const_spec((D_MODEL, EVEN_IN)),
            _const_spec((1, HEAD_DIM)), _const_spec((1, HEAD_DIM)),
            _const_spec((1, B_WIDTH)), _const_spec((1, B_WIDTH)),
            pl.BlockSpec((tm, HEAD_DIM), lambda b, i: (i, 0)),
            pl.BlockSpec((tm, HEAD_DIM), lambda b, i: (i, 0)),
        ],
        out_specs=[
            pl.BlockSpec((1, N_Q_HEADS, tm, HEAD_DIM), lambda b, i: (b, 0, i, 0)),
            pl.BlockSpec((1, tm, KV_WIDTH), row), pl.BlockSpec((1, tm, KV_WIDTH), row),
            pl.BlockSpec((1, tm, B_WIDTH), row), pl.BlockSpec((1, tm, B_WIDTH), row),
            pl.BlockSpec((1, tm, D_INNER), row),
        ],
        out_shape=[bf(bsz, N_Q_HEADS, t, HEAD_DIM), bf(bsz, t, KV_WIDTH), bf(bsz, t, KV_WIDTH),
                   bf(bsz, t, B_WIDTH), bf(bsz, t, B_WIDTH), bf(bsz, t, D_INNER)],
        compiler_params=_params("arbitrary", "arbitrary"),
        name="proj_even",
    )(x, sh, sc, w_in, q_g, k_g, v_g, v_b, cos2, sin2)


def _ctx_kv_kernel(c_ref, sh_ref, sc_ref, w_ref, kg_ref, kc_ref, vc_ref):
    m = (_ln(c_ref[0]) * (1.0 + sc_ref[...]) + sh_ref[...]).astype(BF16)
    z = _dot(m, w_ref[...])
    for j in range(N_KV_HEADS):
        zh = z[:, j * HEAD_DIM:(j + 1) * HEAD_DIM]
        kc_ref[0, :, j * HEAD_DIM:(j + 1) * HEAD_DIM] = _rms_head(zh, kg_ref[...]).astype(BF16)
    vc_ref[0] = z[:, KV_WIDTH:].astype(BF16)


def _ctx_kv(ctx, sh_c, sc_c, w_in, k_g):
    bsz, s, _ = ctx.shape
    assert _K0 % (2 * KV_WIDTH) == 0
    return pl.pallas_call(
        _ctx_kv_kernel,
        grid=(bsz,),
        in_specs=[
            pl.BlockSpec((1, s, D_MODEL), lambda b: (b, 0, 0)),
            pl.BlockSpec((1, D_MODEL), lambda b: (0, 0)), pl.BlockSpec((1, D_MODEL), lambda b: (0, 0)),
            pl.BlockSpec((D_MODEL, 2 * KV_WIDTH), lambda b: (0, _K0 // (2 * KV_WIDTH))),
            pl.BlockSpec((1, HEAD_DIM), lambda b: (0, 0)),
        ],
        out_specs=[pl.BlockSpec((1, s, KV_WIDTH), lambda b: (b, 0, 0))] * 2,
        out_shape=[jax.ShapeDtypeStruct((bsz, s, KV_WIDTH), BF16)] * 2,
        compiler_params=_params("arbitrary"),
        name="ctx_kv",
    )(ctx, sh_c, sc_c, w_in, k_g)


def _attn_scores(q, kc_ref, k_ref):
    return _dot_nt(q, kc_ref[0]), _dot_nt(q, k_ref[0])


def _attn_kernel(q_ref, kc_ref, vc_ref, k_ref, v_ref, qn_ref, kcn_ref, kn_ref, a_ref, s0_ref):
    nc = kc_ref.shape[1]
    first = (pl.program_id(0) == 0) & (pl.program_id(1) == 0) & (pl.program_id(2) == 0)

    @pl.when(first)
    def _():
        s_c, s_x = _attn_scores(q_ref[0, 0, :ATTN_ROWS], kc_ref, k_ref)
        s0_ref[:, :nc] = s_c
        s0_ref[:, nc:] = s_x

    for r0 in range(0, q_ref.shape[2], ATTN_ROWS):
        rows = slice(r0, r0 + ATTN_ROWS)
        for j in range(q_ref.shape[1]):
            if r0 == 0 and j == 0:
                s_c, s_x = s0_ref[:, :nc], s0_ref[:, nc:]
            else:
                s_c, s_x = _attn_scores(q_ref[0, j, rows], kc_ref, k_ref)
            mx = jnp.maximum(jnp.max(s_c, axis=-1, keepdims=True), jnp.max(s_x, axis=-1, keepdims=True))
            p_c = jnp.exp(s_c - mx)
            p_x = jnp.exp(s_x - mx)
            den = jnp.sum(p_c, axis=-1, keepdims=True) + jnp.sum(p_x, axis=-1, keepdims=True)
            o = (_dot(p_c.astype(BF16), vc_ref[0]) + _dot(p_x.astype(BF16), v_ref[0])) / den
            a_ref[0, rows, j * HEAD_DIM:(j + 1) * HEAD_DIM] = o.astype(BF16)
    s_c, s_x = _attn_scores(qn_ref[0, 0], kcn_ref, kn_ref)
    s0_ref[:, :nc] = s_c
    s0_ref[:, nc:] = s_x


def _attention(q, kc, vc, k, v, tq):
    bsz, _, t, _ = q.shape
    s = kc.shape[1]
    nt = t // tq
    last = bsz * N_KV_HEADS * nt - 1

    def nxt(b, h, i):
        n = jnp.minimum((b * N_KV_HEADS + h) * nt + i + 1, last)
        return n // (N_KV_HEADS * nt), (n // nt) % N_KV_HEADS, n % nt

    def qn_map(b, h, i):
        b2, h2, i2 = nxt(b, h, i)
        return b2, h2 * Q_PER_KV, i2 * (tq // ATTN_ROWS), 0

    def kvn_map(b, h, i):
        b2, h2, _ = nxt(b, h, i)
        return b2, 0, h2

    kv_c = pl.BlockSpec((1, s, HEAD_DIM), lambda b, h, i: (b, 0, h))
    kv_x = pl.BlockSpec((1, t, HEAD_DIM), lambda b, h, i: (b, 0, h))
    return pl.pallas_call(
        _attn_kernel,
        grid=(bsz, N_KV_HEADS, nt),
        in_specs=[pl.BlockSpec((1, Q_PER_KV, tq, HEAD_DIM), lambda b, h, i: (b, h, i, 0)),
                  kv_c, kv_c, kv_x, kv_x,
                  pl.BlockSpec((1, 1, ATTN_ROWS, HEAD_DIM), qn_map),
                  pl.BlockSpec((1, s, HEAD_DIM), kvn_map), pl.BlockSpec((1, t, HEAD_DIM), kvn_map)],
        out_specs=pl.BlockSpec((1, tq, Q_PER_KV * HEAD_DIM), lambda b, h, i: (b, i, h)),
        out_shape=jax.ShapeDtypeStruct((bsz, t, A_WIDTH), BF16),
        scratch_shapes=[pltpu.VMEM((ATTN_ROWS, s + t), F32)],
        compiler_params=_params("arbitrary", "arbitrary", "arbitrary"),
        name="attention",
    )(q, kc, vc, k, v, q, kc, k)


def _deepnorm(x, gt, y, pg, pb):
    return _ln(ALPHA * x + gt * y) * pg + pb


def _even_out_kernel(a_ref, u_ref, vn_ref, sg_ref, x_ref, gt_ref, ws_ref, bs_ref, wo_ref, pg_ref, pb_ref,
                     o_ref, comb_ref):
    for r0 in range(0, a_ref.shape[1], ROW_CHAIN):
        rc = slice(r0, r0 + ROW_CHAIN)
        comb_ref[rc, :A_WIDTH] = a_ref[0, rc] * sg_ref[0, rc, :A_WIDTH]
        for n in range(ROW_CHAIN // CHUNK):
            rows = slice(r0 + n * CHUNK, r0 + (n + 1) * CHUNK)
            for g in range(B_GROUPS):
                cols = slice(g * B_GROUP_DIM, (g + 1) * B_GROUP_DIM)
                gcols = slice(A_WIDTH + g * B_GROUP_DIM, A_WIDTH + (g + 1) * B_GROUP_DIM)
                mixed = _dot(ws_ref[g], vn_ref[0, rows, cols]) + bs_ref[g]
                comb_ref[rows, gcols] = (u_ref[0, rows, cols].astype(F32) * mixed
                                         * sg_ref[0, rows, gcols].astype(F32)).astype(BF16)
        y = _dot(comb_ref[rc], wo_ref[...])
        o_ref[0, rc] = _deepnorm(x_ref[0, rc], gt_ref[0], y, pg_ref[...], pb_ref[...])


def _even_out(a, u, vn, sg, x, gt, w_s, b_s, w_out, pg, pb, tm):
    bsz, t, _ = x.shape
    row = lambda b, i: (b, i, 0)
    return pl.pallas_call(
        _even_out_kernel,
        grid=(bsz, t // tm),
        in_specs=[
            pl.BlockSpec((1, tm, A_WIDTH), row), pl.BlockSpec((1, tm, B_WIDTH), row),
            pl.BlockSpec((1, tm, B_WIDTH), row), pl.BlockSpec((1, tm, D_INNER), row),
            pl.BlockSpec((1, tm, D_MODEL), row),
            pl.BlockSpec((1, 1, D_MODEL), lambda b, i: (b, 0, 0)),
            _const_spec((B_GROUPS, CHUNK, CHUNK)), _const_spec((B_GROUPS, CHUNK, B_GROUP_DIM)),
            _const_spec((D_INNER, D_MODEL)),
            _const_spec((1, D_MODEL)), _const_spec((1, D_MODEL)),
        ],
        out_specs=pl.BlockSpec((1, tm, D_MODEL), row),
        out_shape=jax.ShapeDtypeStruct((bsz, t, D_MODEL), F32),
        scratch_shapes=[pltpu.VMEM((tm, D_INNER), BF16)],
        compiler_params=_params("arbitrary", "arbitrary"),
        name="even_out",
    )(a, u, vn, sg, x, gt, w_s, b_s, w_out, pg, pb)


def _proj_odd_kernel(x_ref, sh_ref, sc_ref, w_ref, cs_ref, a_ref, b_ref, m_ref):
    tm = x_ref.shape[1]
    n = tm // FFT_N1
    m32 = _ln(x_ref[0]) * (1.0 + sc_ref[0]) + sh_ref[0]
    n_slab = D_MODEL // 128
    for j in range(n_slab):
        m_ref[j] = m32[:, j * 128:(j + 1) * 128]
    mp = jnp.concatenate(
        [jnp.concatenate([m_ref[j, pl.ds(t1, n, stride=FFT_N1), :] for j in range(n_slab)], axis=1)
         for t1 in range(FFT_N1)], axis=0).astype(BF16)
    cw = 4 * C_GROUP_DIM
    for c in range(D_INNER // cw):
        z = _dot(mp, w_ref[:, c * cw:(c + 1) * cw]).astype(BF16)
        for j in range(cw // C_GROUP_DIM):
            cols = slice(c * cw + j * C_GROUP_DIM, c * cw + (j + 1) * C_GROUP_DIM)
            ab = _dot(z[:, j * C_GROUP_DIM:(j + 1) * C_GROUP_DIM], cs_ref[...])
            for t1 in range(FFT_N1):
                a_ref[0, t1, :, cols] = ab[t1 * n:(t1 + 1) * n, :C_GROUP_DIM].astype(BF16)
                b_ref[0, t1, :, cols] = ab[t1 * n:(t1 + 1) * n, C_GROUP_DIM:].astype(BF16)


def _proj_odd(x, sh, sc, w_hb, cs128, tm):
    bsz, t, _ = x.shape
    row = lambda b, i: (b, i, 0)
    per_b = pl.BlockSpec((1, 1, D_MODEL), lambda b, i: (b, 0, 0))
    n = tm // FFT_N1
    perm_spec = pl.BlockSpec((1, FFT_N1, n, D_INNER), lambda b, i: (b, 0, i, 0))
    perm = jax.ShapeDtypeStruct((bsz, FFT_N1, t // FFT_N1, D_INNER), BF16)
    return pl.pallas_call(
        _proj_odd_kernel,
        grid=(bsz, t // tm),
        in_specs=[pl.BlockSpec((1, tm, D_MODEL), row), per_b, per_b,
                  _const_spec((D_MODEL, D_INNER)), _const_spec((C_GROUP_DIM, 2 * C_GROUP_DIM))],
        out_specs=[perm_spec, perm_spec],
        out_shape=[perm, perm],
        scratch_shapes=[pltpu.VMEM((D_MODEL // 128, tm, 128), F32)],
        compiler_params=_params("arbitrary", "arbitrary"),
        name="proj_odd",
    )(x, sh, sc, w_hb, cs128)


def _dft4(ar, ai):
    s0r, s0i = ar[0] + ar[2], ai[0] + ai[2]
    s1r, s1i = ar[0] - ar[2], ai[0] - ai[2]
    s2r, s2i = ar[1] + ar[3], ai[1] + ai[3]
    s3r, s3i = ar[1] - ar[3], ai[1] - ai[3]
    return ([s0r + s2r, s1r - s3i, s0r - s2r, s1r + s3i],
            [s0i + s2i, s1i + s3r, s0i - s2i, s1i - s3r])


def _dft8(zr, zi):
    er, ei = _dft4(zr[0::2], zi[0::2])
    orr, oi = _dft4(zr[1::2], zi[1::2])
    h = np.float32(np.sqrt(0.5))
    tr = [orr[0], (orr[1] - oi[1]) * h, -oi[2], (-orr[3] - oi[3]) * h]
    ti = [oi[0], (orr[1] + oi[1]) * h, orr[2], (orr[3] - oi[3]) * h]
    xr = [er[k] + tr[k] for k in range(4)] + [er[k] - tr[k] for k in range(4)]
    xi = [ei[k] + ti[k] for k in range(4)] + [ei[k] - ti[k] for k in range(4)]
    return xr, xi


ROW_CHAIN = 256
ATTN_ROWS = 512
FFT_COLS = 256
FFT_ROWS = 16


def _fft_t_kernel(a_ref, b_ref, twr_ref, twi_ref, w_ref, f_ref, g_ref):
    half = FFT_N2
    for ch in range(a_ref.shape[3] // FFT_COLS):
        c0 = ch * FFT_COLS
        for t1 in range(FFT_N1):
            g_ref[ch, t1] = (_dot(w_ref[:, :half], a_ref[0, t1, :, c0:c0 + FFT_COLS])
                             + _dot(w_ref[:, half:], b_ref[0, t1, :, c0:c0 + FFT_COLS]))
        for r in range(0, FFT_N2, FFT_ROWS):
            rows = slice(r, r + FFT_ROWS)
            rows_im = slice(half + r, half + r + FFT_ROWS)
            for j in range(FFT_COLS // 128):
                lanes = slice(j * 128, (j + 1) * 128)
                zr = [g_ref[ch, 0, rows, lanes]]
                zi = [g_ref[ch, 0, rows_im, lanes]]
                for t1 in range(1, FFT_N1):
                    gr = g_ref[ch, t1, rows, lanes]
                    gi = g_ref[ch, t1, rows_im, lanes]
                    wr = twr_ref[t1, rows, :]
                    wi = twi_ref[t1, rows, :]
                    zr.append(gr * wr - gi * wi)
                    zi.append(gr * wi + gi * wr)
                xr, _ = _dft8(zr, zi)
                for k1 in range(FFT_N1):
                    f_ref[0, k1 * FFT_N2 + r:k1 * FFT_N2 + r + FFT_ROWS,
                          c0 + j * 128:c0 + (j + 1) * 128] = xr[k1].astype(BF16)


def _fft_t(a, b, twr, twi, w512, tn):
    bsz, _, n2, w = a.shape
    t = FFT_N1 * n2
    blk = pl.BlockSpec((1, FFT_N1, n2, tn), lambda bb, j: (bb, 0, 0, j))
    return pl.pallas_call(
        _fft_t_kernel,
        grid=(bsz, w // tn),
        in_specs=[blk, blk,
                  _const_spec((FFT_N1, FFT_N2, 128)), _const_spec((FFT_N1, FFT_N2, 128)),
                  _const_spec((2 * FFT_N2, 2 * FFT_N2))],
        out_specs=pl.BlockSpec((1, t, tn), lambda bb, j: (bb, 0, j)),
        out_shape=jax.ShapeDtypeStruct((bsz, t, w), BF16),
        scratch_shapes=[pltpu.VMEM((tn // FFT_COLS, FFT_N1, 2 * FFT_N2, FFT_COLS), F32)],
        compiler_params=_params("arbitrary", "arbitrary"),
        name="fft_t",
    )(a, b, twr, twi, w512)


def _odd_out_kernel(f_ref, x_ref, sh_ref, sc_ref, gt_ref, wg_ref, wo_ref, pg_ref, pb_ref, o_ref):
    for r0 in range(0, x_ref.shape[1], ROW_CHAIN):
        rows = slice(r0, r0 + ROW_CHAIN)
        x = x_ref[0, rows]
        m = (_ln(x) * (1.0 + sc_ref[0]) + sh_ref[0]).astype(BF16)
        gate = jax.nn.silu(_dot(m, wg_ref[...]))
        y = _dot((f_ref[0, rows].astype(F32) * gate).astype(BF16), wo_ref[...])
        o_ref[0, rows] = _deepnorm(x, gt_ref[0], y, pg_ref[...], pb_ref[...])


def _odd_out(f, x, sh, sc, gt, w_g, w_out, pg, pb, tm):
    bsz, t, _ = x.shape
    row = lambda b, i: (b, i, 0)
    per_b = pl.BlockSpec((1, 1, D_MODEL), lambda b, i: (b, 0, 0))
    return pl.pallas_call(
        _odd_out_kernel,
        grid=(bsz, t // tm),
        in_specs=[
            pl.BlockSpec((1, tm, D_INNER), row), pl.BlockSpec((1, tm, D_MODEL), row),
            per_b, per_b, per_b,
            _const_spec((D_MODEL, D_INNER)), _const_spec((D_INNER, D_MODEL)),
            _const_spec((1, D_MODEL)), _const_spec((1, D_MODEL)),
        ],
        out_specs=pl.BlockSpec((1, tm, D_MODEL), row),
        out_shape=jax.ShapeDtypeStruct((bsz, t, D_MODEL), F32),
        compiler_params=_params("arbitrary", "arbitrary"),
        name="odd_out",
    )(f, x, sh, sc, gt, w_g, w_out, pg, pb)


def _rope_tables(n_tokens):
    rows = n_tokens // GRID_W
    r, cl = jnp.meshgrid(jnp.arange(rows), jnp.arange(GRID_W), indexing="ij")
    row = r.reshape(-1).astype(F32)
    col = cl.reshape(-1).astype(F32)
    n_pairs_axis = HEAD_DIM // 4
    inv_freq = ROPE_THETA ** (-jnp.arange(n_pairs_axis, dtype=F32) / n_pairs_axis)
    ang = jnp.concatenate([row[:, None] * inv_freq, col[:, None] * inv_freq], axis=-1)
    cs, sn = jnp.cos(ang), jnp.sin(ang)
    return jnp.concatenate([cs, cs], axis=-1), jnp.concatenate([-sn, sn], axis=-1)


def _dft_tables(n_tokens):
    assert n_tokens == FFT_N1 * FFT_N2
    c = np.arange(C_GROUP_DIM)
    ang = 2.0 * np.pi * np.outer(c, c) / C_GROUP_DIM
    norm = 1.0 / np.sqrt(float(n_tokens * C_GROUP_DIM))
    cs128 = np.concatenate([np.cos(ang), np.sin(ang)], axis=1) * norm
    t1 = np.arange(FFT_N1)[:, None]
    k2 = np.arange(FFT_N2)[None, :]
    tw = 2.0 * np.pi * t1 * k2 / n_tokens
    twr = np.broadcast_to(np.cos(tw)[:, :, None], (FFT_N1, FFT_N2, 128))
    twi = np.broadcast_to(np.sin(tw)[:, :, None], (FFT_N1, FFT_N2, 128))
    kk = np.arange(FFT_N2)
    ang2 = 2.0 * np.pi * np.outer(kk, kk) / FFT_N2
    c256, s256 = np.cos(ang2), np.sin(ang2)
    w512 = np.block([[c256, -s256], [s256, c256]])
    f = lambda a: jnp.asarray(np.ascontiguousarray(a), dtype=F32)
    return f(cs128).astype(BF16), f(twr), f(twi), f(w512).astype(BF16)


def kernel(x, c, ctx, c_ctx, w_mod, b_mod, post_ln_g, post_ln_b, even_w_in, even_q_norm, even_k_norm,
           even_v_ln_g, even_v_ln_b, even_w_s, even_b_s, even_w_out, odd_w_in, odd_w_out):
    bsz, t, _ = x.shape
    assert DEPTH == 2 and t % CHUNK == 0
    row1 = lambda v: v.reshape(1, -1)

    n_cond = -(-(bsz + 1) // 8) * 8
    cond = jnp.zeros((n_cond, D_MODEL), F32).at[:bsz].set(c).at[bsz].set(c_ctx)
    mod = _adaln(cond, w_mod, b_mod)
    split = lambda l, rows: [mod[l, rows, i * D_MODEL:(i + 1) * D_MODEL] for i in range(3)]
    sh0, sc0, gt0 = [v[:, None, :] for v in split(0, slice(0, bsz))]
    sh0c, sc0c, _ = split(0, slice(bsz, bsz + 1))
    sh1, sc1, gt1 = [v[:, None, :] for v in split(1, slice(0, bsz))]

    cos2, sin2 = _rope_tables(t)
    cs128, twr, twi, w512 = _dft_tables(t)

    w_in0 = even_w_in[0].astype(BF16)
    q, k, v, u, vn, sg = _proj_even(x, sh0, sc0, w_in0, row1(even_q_norm[0]), row1(even_k_norm[0]),
                                    row1(even_v_ln_g[0]), row1(even_v_ln_b[0]), cos2, sin2, tm=512)
    kc, vc = _ctx_kv(ctx, sh0c, sc0c, w_in0, row1(even_k_norm[0]))
    a = _attention(q, kc, vc, k, v, tq=1024)
    b_s = jnp.broadcast_to(even_b_s[0][:, :, None], (B_GROUPS, CHUNK, B_GROUP_DIM))
    x1 = _even_out(a, u, vn, sg, x, gt0, even_w_s[0].astype(BF16), b_s, even_w_out[0].astype(BF16),
                   row1(post_ln_g[0]), row1(post_ln_b[0]), tm=512)

    w_in1 = odd_w_in[0].astype(BF16)
    fa, fb = _proj_odd(x1, sh1, sc1, w_in1[:, :D_INNER], cs128, tm=512)
    f = _fft_t(fa, fb, twr, twi, w512, tn=1024)
    return _odd_out(f, x1, sh1, sc1, gt1, w_in1[:, D_INNER:], odd_w_out[0].astype(BF16),
                    row1(post_ln_g[1]), row1(post_ln_b[1]), tm=512)
```

```python
import functools

import numpy as np
import jax
import jax.numpy as jnp
from jax import lax
from jax.experimental import pallas as pl
from jax.experimental.pallas import tpu as pltpu

D_MODEL = 1024
DEPTH = 2
GRID_W = 64
D_INNER = 2 * D_MODEL
HEAD_DIM = 128
A_WIDTH = D_INNER // 2
N_Q_HEADS = A_WIDTH // HEAD_DIM
N_KV_HEADS = 2
Q_PER_KV = N_Q_HEADS // N_KV_HEADS
KV_WIDTH = N_KV_HEADS * HEAD_DIM
B_WIDTH = D_INNER - A_WIDTH
CHUNK = 128
B_GROUP_DIM = 128
B_GROUPS = B_WIDTH // B_GROUP_DIM
C_GROUP_DIM = 128
C_GROUPS = D_INNER // C_GROUP_DIM
ROPE_THETA = 10000.0
EVEN_IN = A_WIDTH + 2 * KV_WIDTH + 2 * B_WIDTH + D_INNER
ODD_IN = 2 * D_INNER
ALPHA = (2 * DEPTH) ** 0.25
EPS = 1e-6

_Q0, _K0, _V0 = 0, A_WIDTH, A_WIDTH + KV_WIDTH
_U0 = A_WIDTH + 2 * KV_WIDTH
_BV0 = _U0 + B_WIDTH
_G0 = _BV0 + B_WIDTH

FFT_N1 = 8
FFT_N2 = 256

V7X_VMEM_LIMIT_BYTES = 60000 * 1024

F32 = jnp.float32
BF16 = jnp.bfloat16


def _dot(a, b):
    return jnp.dot(a, b, preferred_element_type=F32)


def _dot_nt(a, b):
    return lax.dot_general(a, b, (((1,), (1,)), ((), ())), preferred_element_type=F32)


def _ln(x):
    mu = jnp.mean(x, axis=-1, keepdims=True)
    xc = x - mu
    var = jnp.mean(xc * xc, axis=-1, keepdims=True)
    return xc * lax.rsqrt(var + EPS)


def _rms_head(z, g):
    return z * lax.rsqrt(jnp.mean(z * z, axis=-1, keepdims=True) + EPS) * g


def _rope(y, cos2, sin2):
    return y * cos2 + pltpu.roll(y, HEAD_DIM // 2, 1) * sin2


def _params(*sem):
    return pltpu.CompilerParams(dimension_semantics=sem, vmem_limit_bytes=V7X_VMEM_LIMIT_BYTES)


def _const_spec(shape):
    nd = len(shape)
    return pl.BlockSpec(shape, lambda *_: (0,) * nd, pipeline_mode=pl.Buffered(1))


def _adaln_kernel(c_ref, w_ref, b_ref, o_ref):
    h = jax.nn.silu(c_ref[...])
    w = w_ref[0]
    h_hi = h.astype(BF16)
    h_lo = (h - h_hi.astype(F32)).astype(BF16)
    w_hi = w.astype(BF16)
    w_lo = (w - w_hi.astype(F32)).astype(BF16)
    o_ref[0] = _dot(h_hi, w_hi) + _dot(h_hi, w_lo) + _dot(h_lo, w_hi) + b_ref[0]


def _adaln(cond, w_mod, b_mod):
    r = cond.shape[0]
    tn = D_MODEL
    return pl.pallas_call(
        _adaln_kernel,
        grid=(DEPTH, 3 * D_MODEL // tn),
        in_specs=[
            pl.BlockSpec((r, D_MODEL), lambda l, j: (0, 0)),
            pl.BlockSpec((1, D_MODEL, tn), lambda l, j: (l, 0, j)),
            pl.BlockSpec((1, 1, tn), lambda l, j: (l, 0, j)),
        ],
        out_specs=pl.BlockSpec((1, r, tn), lambda l, j: (l, 0, j)),
        out_shape=jax.ShapeDtypeStruct((DEPTH, r, 3 * D_MODEL), F32),
        compiler_params=_params("arbitrary", "arbitrary"),
        name="adaln",
    )(cond, w_mod, b_mod.reshape(DEPTH, 1, 3 * D_MODEL))


def _proj_even_kernel(x_ref, sh_ref, sc_ref, w_ref, qg_ref, kg_ref, vg_ref, vb_ref, cos_ref, sin_ref,
                      q_ref, k_ref, v_ref, u_ref, vn_ref, sg_ref):
    qg = qg_ref[...] * (HEAD_DIM ** -0.5)
    kg = kg_ref[...]
    cw = 4 * HEAD_DIM
    for r0 in range(0, x_ref.shape[1], ROW_CHAIN):
        rows = slice(r0, r0 + ROW_CHAIN)
        m = (_ln(x_ref[0, rows]) * (1.0 + sc_ref[0]) + sh_ref[0]).astype(BF16)
        cos2 = cos_ref[rows]
        sin2 = sin_ref[rows]
        for c in range(A_WIDTH // cw):
            z = _dot(m, w_ref[:, _Q0 + c * cw:_Q0 + (c + 1) * cw])
            for j in range(cw // HEAD_DIM):
                zh = z[:, j * HEAD_DIM:(j + 1) * HEAD_DIM]
                q_ref[0, c * (cw // HEAD_DIM) + j, rows] = _rope(_rms_head(zh, qg), cos2, sin2).astype(BF16)
        z = _dot(m, w_ref[:, _K0:_K0 + 2 * KV_WIDTH])
        for j in range(N_KV_HEADS):
            zh = z[:, j * HEAD_DIM:(j + 1) * HEAD_DIM]
            k_ref[0, rows, j * HEAD_DIM:(j + 1) * HEAD_DIM] = _rope(_rms_head(zh, kg), cos2, sin2).astype(BF16)
        v_ref[0, rows] = z[:, KV_WIDTH:].astype(BF16)
        for c in range(B_WIDTH // cw):
            z = _dot(m, w_ref[:, _U0 + c * cw:_U0 + (c + 1) * cw])
            u_ref[0, rows, c * cw:(c + 1) * cw] = jax.nn.gelu(z).astype(BF16)
        gv = jax.nn.gelu(_dot(m, w_ref[:, _BV0:_BV0 + B_WIDTH]))
        vn_ref[0, rows] = (_ln(gv) * vg_ref[...] + vb_ref[...]).astype(BF16)
        for c in range(D_INNER // cw):
            z = _dot(m, w_ref[:, _G0 + c * cw:_G0 + (c + 1) * cw])
            sg_ref[0, rows, c * cw:(c + 1) * cw] = jax.nn.silu(z).astype(BF16)


def _proj_even(x, sh, sc, w_in, q_g, k_g, v_g, v_b, cos2, sin2, tm):
    bsz, t, _ = x.shape
    row = lambda b, i: (b, i, 0)
    per_b = pl.BlockSpec((1, 1, D_MODEL), lambda b, i: (b, 0, 0))
    bf = lambda *s: jax.ShapeDtypeStruct(s, BF16)
    return pl.pallas_call(
        _proj_even_kernel,
        grid=(bsz, t // tm),
        in_specs=[
            pl.BlockSpec((1, tm, D_MODEL), row), per_b, per_b,
            _const_spec((D_MODEL, EVEN_IN)),
            _const_spec((1, HEAD_DIM)), _const_spec((1, HEAD_DIM)),
            _const_spec((1, B_WIDTH)), _const_spec((1, B_WIDTH)),
            pl.BlockSpec((tm, HEAD_DIM), lambda b, i: (i, 0)),
            pl.BlockSpec((tm, HEAD_DIM), lambda b, i: (i, 0)),
        ],
        out_specs=[
            pl.BlockSpec((1, N_Q_HEADS, tm, HEAD_DIM), lambda b, i: (b, 0, i, 0)),
            pl.BlockSpec((1, tm, KV_WIDTH), row), pl.BlockSpec((1, tm, KV_WIDTH), row),
            pl.BlockSpec((1, tm, B_WIDTH), row), pl.BlockSpec((1, tm, B_WIDTH), row),
            pl.BlockSpec((1, tm, D_INNER), row),
        ],
        out_shape=[bf(bsz, N_Q_HEADS, t, HEAD_DIM), bf(bsz, t, KV_WIDTH), bf(bsz, t, KV_WIDTH),
                   bf(bsz, t, B_WIDTH), bf(bsz, t, B_WIDTH), bf(bsz, t, D_INNER)],
        compiler_params=_params("arbitrary", "arbitrary"),
        name="proj_even",
    )(x, sh, sc, w_in, q_g, k_g, v_g, v_b, cos2, sin2)


def _ctx_kv_kernel(c_ref, sh_ref, sc_ref, w_ref, kg_ref, kc_ref, vc_ref):
    m = (_ln(c_ref[0]) * (1.0 + sc_ref[...]) + sh_ref[...]).astype(BF16)
    z = _dot(m, w_ref[...])
    for j in range(N_KV_HEADS):
        zh = z[:, j * HEAD_DIM:(j + 1) * HEAD_DIM]
        kc_ref[0, :, j * HEAD_DIM:(j + 1) * HEAD_DIM] = _rms_head(zh, kg_ref[...]).astype(BF16)
    vc_ref[0] = z[:, KV_WIDTH:].astype(BF16)


def _ctx_kv(ctx, sh_c, sc_c, w_in, k_g):
    bsz, s, _ = ctx.shape
    assert _K0 % (2 * KV_WIDTH) == 0
    return pl.pallas_call(
        _ctx_kv_kernel,
        grid=(bsz,),
        in_specs=[
            pl.BlockSpec((1, s, D_MODEL), lambda b: (b, 0, 0)),
            pl.BlockSpec((1, D_MODEL), lambda b: (0, 0)), pl.BlockSpec((1, D_MODEL), lambda b: (0, 0)),
            pl.BlockSpec((D_MODEL, 2 * KV_WIDTH), lambda b: (0, _K0 // (2 * KV_WIDTH))),
            pl.BlockSpec((1, HEAD_DIM), lambda b: (0, 0)),
        ],
        out_specs=[pl.BlockSpec((1, s, KV_WIDTH), lambda b: (b, 0, 0))] * 2,
        out_shape=[jax.ShapeDtypeStruct((bsz, s, KV_WIDTH), BF16)] * 2,
        compiler_params=_params("arbitrary"),
        name="ctx_kv",
    )(ctx, sh_c, sc_c, w_in, k_g)


def _attn_scores(q, kc_ref, k_ref):
    return _dot_nt(q, kc_ref[0]), _dot_nt(q, k_ref[0])


def _attn_kernel(q_ref, kc_ref, vc_ref, k_ref, v_ref, qn_ref, kcn_ref, kn_ref, a_ref, s0_ref):
    nc = kc_ref.shape[1]
    first = (pl.program_id(0) == 0) & (pl.program_id(1) == 0) & (pl.program_id(2) == 0)

    @pl.when(first)
    def _():
        s_c, s_x = _attn_scores(q_ref[0, 0, :ATTN_ROWS], kc_ref, k_ref)
        s0_ref[:, :nc] = s_c
        s0_ref[:, nc:] = s_x

    for r0 in range(0, q_ref.shape[2], ATTN_ROWS):
        rows = slice(r0, r0 + ATTN_ROWS)
        for j in range(q_ref.shape[1]):
            if r0 == 0 and j == 0:
                s_c, s_x = s0_ref[:, :nc], s0_ref[:, nc:]
            else:
                s_c, s_x = _attn_scores(q_ref[0, j, rows], kc_ref, k_ref)
            mx = jnp.maximum(jnp.max(s_c, axis=-1, keepdims=True), jnp.max(s_x, axis=-1, keepdims=True))
            p_c = jnp.exp(s_c - mx)
            p_x = jnp.exp(s_x - mx)
            den = jnp.sum(p_c, axis=-1, keepdims=True) + jnp.sum(p_x, axis=-1, keepdims=True)
            o = (_dot(p_c.astype(BF16), vc_ref[0]) + _dot(p_x.astype(BF16), v_ref[0])) / den
            a_ref[0, rows, j * HEAD_DIM:(j + 1) * HEAD_DIM] = o.astype(BF16)
    s_c, s_x = _attn_scores(qn_ref[0, 0], kcn_ref, kn_ref)
    s0_ref[:, :nc] = s_c
    s0_ref[:, nc:] = s_x


def _attention(q, kc, vc, k, v, tq):
    bsz, _, t, _ = q.shape
    s = kc.shape[1]
    nt = t // tq
    last = bsz * N_KV_HEADS * nt - 1

    def nxt(b, h, i):
        n = jnp.minimum((b * N_KV_HEADS + h) * nt + i + 1, last)
        return n // (N_KV_HEADS * nt), (n // nt) % N_KV_HEADS, n % nt

    def qn_map(b, h, i):
        b2, h2, i2 = nxt(b, h, i)
        return b2, h2 * Q_PER_KV, i2 * (tq // ATTN_ROWS), 0

    def kvn_map(b, h, i):
        b2, h2, _ = nxt(b, h, i)
        return b2, 0, h2

    kv_c = pl.BlockSpec((1, s, HEAD_DIM), lambda b, h, i: (b, 0, h))
    kv_x = pl.BlockSpec((1, t, HEAD_DIM), lambda b, h, i: (b, 0, h))
    return pl.pallas_call(
        _attn_kernel,
        grid=(bsz, N_KV_HEADS, nt),
        in_specs=[pl.BlockSpec((1, Q_PER_KV, tq, HEAD_DIM), lambda b, h, i: (b, h, i, 0)),
                  kv_c, kv_c, kv_x, kv_x,
                  pl.BlockSpec((1, 1, ATTN_ROWS, HEAD_DIM), qn_map),
                  pl.BlockSpec((1, s, HEAD_DIM), kvn_map), pl.BlockSpec((1, t, HEAD_DIM), kvn_map)],
        out_specs=pl.BlockSpec((1, tq, Q_PER_KV * HEAD_DIM), lambda b, h, i: (b, i, h)),
        out_shape=jax.ShapeDtypeStruct((bsz, t, A_WIDTH), BF16),
        scratch_shapes=[pltpu.VMEM((ATTN_ROWS, s + t), F32)],
        compiler_params=_params("arbitrary", "arbitrary", "arbitrary"),
        name="attention",
    )(q, kc, vc, k, v, q, kc, k)


def _deepnorm(x, gt, y, pg, pb):
    return _ln(ALPHA * x + gt * y) * pg + pb


def _even_out_kernel(a_ref, u_ref, vn_ref, sg_ref, x_ref, gt_ref, ws_ref, bs_ref, wo_ref, pg_ref, pb_ref,
                     o_ref, comb_ref):
    for r0 in range(0, a_ref.shape[1], ROW_CHAIN):
        rc = slice(r0, r0 + ROW_CHAIN)
        comb_ref[rc, :A_WIDTH] = a_ref[0, rc] * sg_ref[0, rc, :A_WIDTH]
        for n in range(ROW_CHAIN // CHUNK):
            rows = slice(r0 + n * CHUNK, r0 + (n + 1) * CHUNK)
            for g in range(B_GROUPS):
                cols = slice(g * B_GROUP_DIM, (g + 1) * B_GROUP_DIM)
                gcols = slice(A_WIDTH + g * B_GROUP_DIM, A_WIDTH + (g + 1) * B_GROUP_DIM)
                mixed = _dot(ws_ref[g], vn_ref[0, rows, cols]) + bs_ref[g]
                comb_ref[rows, gcols] = (u_ref[0, rows, cols].astype(F32) * mixed
                                         * sg_ref[0, rows, gcols].astype(F32)).astype(BF16)
        y = _dot(comb_ref[rc], wo_ref[...])
        o_ref[0, rc] = _deepnorm(x_ref[0, rc], gt_ref[0], y, pg_ref[...], pb_ref[...])


def _even_out(a, u, vn, sg, x, gt, w_s, b_s, w_out, pg, pb, tm):
    bsz, t, _ = x.shape
    row = lambda b, i: (b, i, 0)
    return pl.pallas_call(
        _even_out_kernel,
        grid=(bsz, t // tm),
        in_specs=[
            pl.BlockSpec((1, tm, A_WIDTH), row), pl.BlockSpec((1, tm, B_WIDTH), row),
            pl.BlockSpec((1, tm, B_WIDTH), row), pl.BlockSpec((1, tm, D_INNER), row),
            pl.BlockSpec((1, tm, D_MODEL), row),
            pl.BlockSpec((1, 1, D_MODEL), lambda b, i: (b, 0, 0)),
            _const_spec((B_GROUPS, CHUNK, CHUNK)), _const_spec((B_GROUPS, CHUNK, B_GROUP_DIM)),
            _const_spec((D_INNER, D_MODEL)),
            _const_spec((1, D_MODEL)), _const_spec((1, D_MODEL)),
        ],
        out_specs=pl.BlockSpec((1, tm, D_MODEL), row),
        out_shape=jax.ShapeDtypeStruct((bsz, t, D_MODEL), F32),
        scratch_shapes=[pltpu.VMEM((tm, D_INNER), BF16)],
        compiler_params=_params("arbitrary", "arbitrary"),
        name="even_out",
    )(a, u, vn, sg, x, gt, w_s, b_s, w_out, pg, pb)


def _proj_odd_kernel(x_ref, sh_ref, sc_ref, w_ref, cs_ref, a_ref, b_ref, m_ref):
    tm = x_ref.shape[1]
    n = tm // FFT_N1
    m32 = _ln(x_ref[0]) * (1.0 + sc_ref[0]) + sh_ref[0]
    n_slab = D_MODEL // 128
    for j in range(n_slab):
        m_ref[j] = m32[:, j * 128:(j + 1) * 128]
    mp = jnp.concatenate(
        [jnp.concatenate([m_ref[j, pl.ds(t1, n, stride=FFT_N1), :] for j in range(n_slab)], axis=1)
         for t1 in range(FFT_N1)], axis=0).astype(BF16)
    cw = 4 * C_GROUP_DIM
    for c in range(D_INNER // cw):
        z = _dot(mp, w_ref[:, c * cw:(c + 1) * cw]).astype(BF16)
        for j in range(cw // C_GROUP_DIM):
            cols = slice(c * cw + j * C_GROUP_DIM, c * cw + (j + 1) * C_GROUP_DIM)
            ab = _dot(z[:, j * C_GROUP_DIM:(j + 1) * C_GROUP_DIM], cs_ref[...])
            for t1 in range(FFT_N1):
                a_ref[0, t1, :, cols] = ab[t1 * n:(t1 + 1) * n, :C_GROUP_DIM].astype(BF16)
                b_ref[0, t1, :, cols] = ab[t1 * n:(t1 + 1) * n, C_GROUP_DIM:].astype(BF16)


def _proj_odd(x, sh, sc, w_hb, cs128, tm):
    bsz, t, _ = x.shape
    row = lambda b, i: (b, i, 0)
    per_b = pl.BlockSpec((1, 1, D_MODEL), lambda b, i: (b, 0, 0))
    n = tm // FFT_N1
    perm_spec = pl.BlockSpec((1, FFT_N1, n, D_INNER), lambda b, i: (b, 0, i, 0))
    perm = jax.ShapeDtypeStruct((bsz, FFT_N1, t // FFT_N1, D_INNER), BF16)
    return pl.pallas_call(
        _proj_odd_kernel,
        grid=(bsz, t // tm),
        in_specs=[pl.BlockSpec((1, tm, D_MODEL), row), per_b, per_b,
                  _const_spec((D_MODEL, D_INNER)), _const_spec((C_GROUP_DIM, 2 * C_GROUP_DIM))],
        out_specs=[perm_spec, perm_spec],
        out_shape=[perm, perm],
        scratch_shapes=[pltpu.VMEM((D_MODEL // 128, tm, 128), F32)],
        compiler_params=_params("arbitrary", "arbitrary"),
        name="proj_odd",
    )(x, sh, sc, w_hb, cs128)


def _dft4(ar, ai):
    s0r, s0i = ar[0] + ar[2], ai[0] + ai[2]
    s1r, s1i = ar[0] - ar[2], ai[0] - ai[2]
    s2r, s2i = ar[1] + ar[3], ai[1] + ai[3]
    s3r, s3i = ar[1] - ar[3], ai[1] - ai[3]
    return ([s0r + s2r, s1r - s3i, s0r - s2r, s1r + s3i],
            [s0i + s2i, s1i + s3r, s0i - s2i, s1i - s3r])


def _dft8(zr, zi):
    er, ei = _dft4(zr[0::2], zi[0::2])
    orr, oi = _dft4(zr[1::2], zi[1::2])
    h = np.float32(np.sqrt(0.5))
    tr = [orr[0], (orr[1] - oi[1]) * h, -oi[2], (-orr[3] - oi[3]) * h]
    ti = [oi[0], (orr[1] + oi[1]) * h, orr[2], (orr[3] - oi[3]) * h]
    xr = [er[k] + tr[k] for k in range(4)] + [er[k] - tr[k] for k in range(4)]
    xi = [ei[k] + ti[k] for k in range(4)] + [ei[k] - ti[k] for k in range(4)]
    return xr, xi


ROW_CHAIN = 256
ATTN_ROWS = 512
FFT_COLS = 256
FFT_ROWS = 16


def _fft_t_kernel(a_ref, b_ref, twr_ref, twi_ref, w_ref, f_ref, g_ref):
    half = FFT_N2
    for ch in range(a_ref.shape[3] // FFT_COLS):
        c0 = ch * FFT_COLS
        for t1 in range(FFT_N1):
            g_ref[ch, t1] = (_dot(w_ref[:, :half], a_ref[0, t1, :, c0:c0 + FFT_COLS])
                             + _dot(w_ref[:, half:], b_ref[0, t1, :, c0:c0 + FFT_COLS]))
        for r in range(0, FFT_N2, FFT_ROWS):
            rows = slice(r, r + FFT_ROWS)
            rows_im = slice(half + r, half + r + FFT_ROWS)
            for j in range(FFT_COLS // 128):
                lanes = slice(j * 128, (j + 1) * 128)
                zr = [g_ref[ch, 0, rows, lanes]]
                zi = [g_ref[ch, 0, rows_im, lanes]]
                for t1 in range(1, FFT_N1):
                    gr = g_ref[ch, t1, rows, lanes]
                    gi = g_ref[ch, t1, rows_im, lanes]
                    wr = twr_ref[t1, rows, :]
                    wi = twi_ref[t1, rows, :]
                    zr.append(gr * wr - gi * wi)
                    zi.append(gr * wi + gi * wr)
                xr, _ = _dft8(zr, zi)
                for k1 in range(FFT_N1):
                    f_ref[0, k1 * FFT_N2 + r:k1 * FFT_N2 + r + FFT_ROWS,
                          c0 + j * 128:c0 + (j + 1) * 128] = xr[k1].astype(BF16)


def _fft_t(a, b, twr, twi, w512, tn):
    bsz, _, n2, w = a.shape
    t = FFT_N1 * n2
    blk = pl.BlockSpec((1, FFT_N1, n2, tn), lambda bb, j: (bb, 0, 0, j))
    return pl.pallas_call(
        _fft_t_kernel,
        grid=(bsz, w // tn),
        in_specs=[blk, blk,
                  _const_spec((FFT_N1, FFT_N2, 128)), _const_spec((FFT_N1, FFT_N2, 128)),
                  _const_spec((2 * FFT_N2, 2 * FFT_N2))],
        out_specs=pl.BlockSpec((1, t, tn), lambda bb, j: (bb, 0, j)),
        out_shape=jax.ShapeDtypeStruct((bsz, t, w), BF16),
        scratch_shapes=[pltpu.VMEM((tn // FFT_COLS, FFT_N1, 2 * FFT_N2, FFT_COLS), F32)],
        compiler_params=_params("arbitrary", "arbitrary"),
        name="fft_t",
    )(a, b, twr, twi, w512)


def _odd_out_kernel(f_ref, x_ref, sh_ref, sc_ref, gt_ref, wg_ref, wo_ref, pg_ref, pb_ref, o_ref):
    for r0 in range(0, x_ref.shape[1], ROW_CHAIN):
        rows = slice(r0, r0 + ROW_CHAIN)
        x = x_ref[0, rows]
        m = (_ln(x) * (1.0 + sc_ref[0]) + sh_ref[0]).astype(BF16)
        gate = jax.nn.silu(_dot(m, wg_ref[...]))
        y = _dot((f_ref[0, rows].astype(F32) * gate).astype(BF16), wo_ref[...])
        o_ref[0, rows] = _deepnorm(x, gt_ref[0], y, pg_ref[...], pb_ref[...])


def _odd_out(f, x, sh, sc, gt, w_g, w_out, pg, pb, tm):
    bsz, t, _ = x.shape
    row = lambda b, i: (b, i, 0)
    per_b = pl.BlockSpec((1, 1, D_MODEL), lambda b, i: (b, 0, 0))
    return pl.pallas_call(
        _odd_out_kernel,
        grid=(bsz, t // tm),
        in_specs=[
            pl.BlockSpec((1, tm, D_INNER), row), pl.BlockSpec((1, tm, D_MODEL), row),
            per_b, per_b, per_b,
            _const_spec((D_MODEL, D_INNER)), _const_spec((D_INNER, D_MODEL)),
            _const_spec((1, D_MODEL)), _const_spec((1, D_MODEL)),
        ],
        out_specs=pl.BlockSpec((1, tm, D_MODEL), row),
        out_shape=jax.ShapeDtypeStruct((bsz, t, D_MODEL), F32),
        compiler_params=_params("arbitrary", "arbitrary"),
        name="odd_out",
    )(f, x, sh, sc, gt, w_g, w_out, pg, pb)


def _rope_tables(n_tokens):
    rows = n_tokens // GRID_W
    r, cl = jnp.meshgrid(jnp.arange(rows), jnp.arange(GRID_W), indexing="ij")
    row = r.reshape(-1).astype(F32)
    col = cl.reshape(-1).astype(F32)
    n_pairs_axis = HEAD_DIM // 4
    inv_freq = ROPE_THETA ** (-jnp.arange(n_pairs_axis, dtype=F32) / n_pairs_axis)
    ang = jnp.concatenate([row[:, None] * inv_freq, col[:, None] * inv_freq], axis=-1)
    cs, sn = jnp.cos(ang), jnp.sin(ang)
    return jnp.concatenate([cs, cs], axis=-1), jnp.concatenate([-sn, sn], axis=-1)


def _dft_tables(n_tokens):
    assert n_tokens == FFT_N1 * FFT_N2
    c = np.arange(C_GROUP_DIM)
    ang = 2.0 * np.pi * np.outer(c, c) / C_GROUP_DIM
    norm = 1.0 / np.sqrt(float(n_tokens * C_GROUP_DIM))
    cs128 = np.concatenate([np.cos(ang), np.sin(ang)], axis=1) * norm
    t1 = np.arange(FFT_N1)[:, None]
    k2 = np.arange(FFT_N2)[None, :]
    tw = 2.0 * np.pi * t1 * k2 / n_tokens
    twr = np.broadcast_to(np.cos(tw)[:, :, None], (FFT_N1, FFT_N2, 128))
    twi = np.broadcast_to(np.sin(tw)[:, :, None], (FFT_N1, FFT_N2, 128))
    kk = np.arange(FFT_N2)
    ang2 = 2.0 * np.pi * np.outer(kk, kk) / FFT_N2
    c256, s256 = np.cos(ang2), np.sin(ang2)
    w512 = np.block([[c256, -s256], [s256, c256]])
    f = lambda a: jnp.asarray(np.ascontiguousarray(a), dtype=F32)
    return f(cs128).astype(BF16), f(twr), f(twi), f(w512).astype(BF16)


def kernel(x, c, ctx, c_ctx, w_mod, b_mod, post_ln_g, post_ln_b, even_w_in, even_q_norm, even_k_norm,
           even_v_ln_g, even_v_ln_b, even_w_s, even_b_s, even_w_out, odd_w_in, odd_w_out):
    bsz, t, _ = x.shape
    assert DEPTH == 2 and t % CHUNK == 0
    row1 = lambda v: v.reshape(1, -1)

    n_cond = -(-(bsz + 1) // 8) * 8
    cond = jnp.zeros((n_cond, D_MODEL), F32).at[:bsz].set(c).at[bsz].set(c_ctx)
    mod = _adaln(cond, w_mod, b_mod)
    split = lambda l, rows: [mod[l, rows, i * D_MODEL:(i + 1) * D_MODEL] for i in range(3)]
    sh0, sc0, gt0 = [v[:, None, :] for v in split(0, slice(0, bsz))]
    sh0c, sc0c, _ = split(0, slice(bsz, bsz + 1))
    sh1, sc1, gt1 = [v[:, None, :] for v in split(1, slice(0, bsz))]

    cos2, sin2 = _rope_tables(t)
    cs128, twr, twi, w512 = _dft_tables(t)

    w_in0 = even_w_in[0].astype(BF16)
    q, k, v, u, vn, sg = _proj_even(x, sh0, sc0, w_in0, row1(even_q_norm[0]), row1(even_k_norm[0]),
                                    row1(even_v_ln_g[0]), row1(even_v_ln_b[0]), cos2, sin2, tm=1024)
    kc, vc = _ctx_kv(ctx, sh0c, sc0c, w_in0, row1(even_k_norm[0]))
    a = _attention(q, kc, vc, k, v, tq=1024)
    b_s = jnp.broadcast_to(even_b_s[0][:, :, None], (B_GROUPS, CHUNK, B_GROUP_DIM))
    x1 = _even_out(a, u, vn, sg, x, gt0, even_w_s[0].astype(BF16), b_s, even_w_out[0].astype(BF16),
                   row1(post_ln_g[0]), row1(post_ln_b[0]), tm=1024)

    w_in1 = odd_w_in[0].astype(BF16)
    fa, fb = _proj_odd(x1, sh1, sc1, w_in1[:, :D_INNER], cs128, tm=1024)
    f = _fft_t(fa, fb, twr, twi, w512, tn=1024)
    return _odd_out(f, x1, sh1, sc1, gt1, w_in1[:, D_INNER:], odd_w_out[0].astype(BF16),
                    row1(post_ln_g[1]), row1(post_ln_b[1]), tm=1024)
```

```python
import functools

import numpy as np
import jax
import jax.numpy as jnp
from jax import lax
from jax.experimental import pallas as pl
from jax.experimental.pallas import tpu as pltpu

D_MODEL = 1024
DEPTH = 2
GRID_W = 64
D_INNER = 2 * D_MODEL
HEAD_DIM = 128
A_WIDTH = D_INNER // 2
N_Q_HEADS = A_WIDTH // HEAD_DIM
N_KV_HEADS = 2
Q_PER_KV = N_Q_HEADS // N_KV_HEADS
KV_WIDTH = N_KV_HEADS * HEAD_DIM
B_WIDTH = D_INNER - A_WIDTH
CHUNK = 128
B_GROUP_DIM = 128
B_GROUPS = B_WIDTH // B_GROUP_DIM
C_GROUP_DIM = 128
C_GROUPS = D_INNER // C_GROUP_DIM
ROPE_THETA = 10000.0
EVEN_IN = A_WIDTH + 2 * KV_WIDTH + 2 * B_WIDTH + D_INNER
ODD_IN = 2 * D_INNER
ALPHA = (2 * DEPTH) ** 0.25
EPS = 1e-6

_Q0, _K0, _V0 = 0, A_WIDTH, A_WIDTH + KV_WIDTH
_U0 = A_WIDTH + 2 * KV_WIDTH
_BV0 = _U0 + B_WIDTH
_G0 = _BV0 + B_WIDTH

FFT_N1 = 8
FFT_N2 = 256
FFT_SLOTS = 64

V7X_VMEM_LIMIT_BYTES = 60000 * 1024

F32 = jnp.float32
BF16 = jnp.bfloat16


def _dot(a, b):
    return jnp.dot(a, b, preferred_element_type=F32)


def _dot_nt(a, b):
    return lax.dot_general(a, b, (((1,), (1,)), ((), ())), preferred_element_type=F32)


def _ln(x):
    mu = jnp.mean(x, axis=-1, keepdims=True)
    xc = x - mu
    var = jnp.mean(xc * xc, axis=-1, keepdims=True)
    return xc * lax.rsqrt(var + EPS)


def _rms_head(z, g):
    return z * lax.rsqrt(jnp.mean(z * z, axis=-1, keepdims=True) + EPS) * g


def _rope(y, cos2, sin2):
    return y * cos2 + pltpu.roll(y, HEAD_DIM // 2, 1) * sin2


def _params(*sem):
    return pltpu.CompilerParams(dimension_semantics=sem, vmem_limit_bytes=V7X_VMEM_LIMIT_BYTES)


def _const_spec(shape):
    nd = len(shape)
    return pl.BlockSpec(shape, lambda *_: (0,) * nd, pipeline_mode=pl.Buffered(1))


def _adaln_kernel(c_ref, w_ref, b_ref, o_ref):
    h = jax.nn.silu(c_ref[...])
    w = w_ref[0]
    h_hi = h.astype(BF16)
    h_lo = (h - h_hi.astype(F32)).astype(BF16)
    w_hi = w.astype(BF16)
    w_lo = (w - w_hi.astype(F32)).astype(BF16)
    o_ref[0] = _dot(h_hi, w_hi) + _dot(h_hi, w_lo) + _dot(h_lo, w_hi) + b_ref[0]


def _adaln(cond, w_mod, b_mod):
    r = cond.shape[0]
    tn = D_MODEL
    return pl.pallas_call(
        _adaln_kernel,
        grid=(DEPTH, 3 * D_MODEL // tn),
        in_specs=[
            pl.BlockSpec((r, D_MODEL), lambda l, j: (0, 0)),
            pl.BlockSpec((1, D_MODEL, tn), lambda l, j: (l, 0, j)),
            pl.BlockSpec((1, 1, tn), lambda l, j: (l, 0, j)),
        ],
        out_specs=pl.BlockSpec((1, r, tn), lambda l, j: (l, 0, j)),
        out_shape=jax.ShapeDtypeStruct((DEPTH, r, 3 * D_MODEL), F32),
        compiler_params=_params("arbitrary", "arbitrary"),
        name="adaln",
    )(cond, w_mod, b_mod.reshape(DEPTH, 1, 3 * D_MODEL))


def _proj_even_kernel(x_ref, sh_ref, sc_ref, w_ref, qg_ref, kg_ref, vg_ref, vb_ref, cos_ref, sin_ref,
                      q_ref, k_ref, v_ref, u_ref, vn_ref, sg_ref):
    qg = qg_ref[...] * (HEAD_DIM ** -0.5)
    kg = kg_ref[...]
    cw = 4 * HEAD_DIM
    for r0 in range(0, x_ref.shape[1], ROW_CHAIN):
        rows = slice(r0, r0 + ROW_CHAIN)
        m = (_ln(x_ref[0, rows]) * (1.0 + sc_ref[0]) + sh_ref[0]).astype(BF16)
        cos2 = cos_ref[rows]
        sin2 = sin_ref[rows]
        for c in range(A_WIDTH // cw):
            z = _dot(m, w_ref[:, _Q0 + c * cw:_Q0 + (c + 1) * cw])
            for j in range(cw // HEAD_DIM):
                zh = z[:, j * HEAD_DIM:(j + 1) * HEAD_DIM]
                q_ref[0, c * (cw // HEAD_DIM) + j, rows] = _rope(_rms_head(zh, qg), cos2, sin2).astype(BF16)
        z = _dot(m, w_ref[:, _K0:_K0 + 2 * KV_WIDTH])
        for j in range(N_KV_HEADS):
            zh = z[:, j * HEAD_DIM:(j + 1) * HEAD_DIM]
            k_ref[0, rows, j * HEAD_DIM:(j + 1) * HEAD_DIM] = _rope(_rms_head(zh, kg), cos2, sin2).astype(BF16)
        v_ref[0, rows] = z[:, KV_WIDTH:].astype(BF16)
        for c in range(B_WIDTH // cw):
            z = _dot(m, w_ref[:, _U0 + c * cw:_U0 + (c + 1) * cw])
            u_ref[0, rows, c * cw:(c + 1) * cw] = jax.nn.gelu(z).astype(BF16)
        gv = jax.nn.gelu(_dot(m, w_ref[:, _BV0:_BV0 + B_WIDTH]))
        vn_ref[0, rows] = (_ln(gv) * vg_ref[...] + vb_ref[...]).astype(BF16)
        for c in range(D_INNER // cw):
            z = _dot(m, w_ref[:, _G0 + c * cw:_G0 + (c + 1) * cw])
            sg_ref[0, rows, c * cw:(c + 1) * cw] = jax.nn.silu(z).astype(BF16)


def _proj_even(x, sh, sc, w_in, q_g, k_g, v_g, v_b, cos2, sin2, tm):
    bsz, t, _ = x.shape
    row = lambda b, i: (b, i, 0)
    per_b = pl.BlockSpec((1, 1, D_MODEL), lambda b, i: (b, 0, 0))
    bf = lambda *s: jax.ShapeDtypeStruct(s, BF16)
    return pl.pallas_call(
        _proj_even_kernel,
        grid=(bsz, t // tm),
        in_specs=[
            pl.BlockSpec((1, tm, D_MODEL), row), per_b, per_b,
            _const_spec((D_MODEL, EVEN_IN)),
            _const_spec((1, HEAD_DIM)), _const_spec((1, HEAD_DIM)),
            _const_spec((1, B_WIDTH)), _const_spec((1, B_WIDTH)),
            pl.BlockSpec((tm, HEAD_DIM), lambda b, i: (i, 0)),
            pl.BlockSpec((tm, HEAD_DIM), lambda b, i: (i, 0)),
        ],
        out_specs=[
            pl.BlockSpec((1, N_Q_HEADS, tm, HEAD_DIM), lambda b, i: (b, 0, i, 0)),
            pl.BlockSpec((1, tm, KV_WIDTH), row), pl.BlockSpec((1, tm, KV_WIDTH), row),
            pl.BlockSpec((1, tm, B_WIDTH), row), pl.BlockSpec((1, tm, B_WIDTH), row),
            pl.BlockSpec((1, tm, D_INNER), row),
        ],
        out_shape=[bf(bsz, N_Q_HEADS, t, HEAD_DIM), bf(bsz, t, KV_WIDTH), bf(bsz, t, KV_WIDTH),
                   bf(bsz, t, B_WIDTH), bf(bsz, t, B_WIDTH), bf(bsz, t, D_INNER)],
        compiler_params=_params("arbitrary", "arbitrary"),
        name="proj_even",
    )(x, sh, sc, w_in, q_g, k_g, v_g, v_b, cos2, sin2)


def _ctx_kv_kernel(c_ref, sh_ref, sc_ref, w_ref, kg_ref, kc_ref, vc_ref):
    m = (_ln(c_ref[0]) * (1.0 + sc_ref[...]) + sh_ref[...]).astype(BF16)
    z = _dot(m, w_ref[...])
    for j in range(N_KV_HEADS):
        zh = z[:, j * HEAD_DIM:(j + 1) * HEAD_DIM]
        kc_ref[0, :, j * HEAD_DIM:(j + 1) * HEAD_DIM] = _rms_head(zh, kg_ref[...]).astype(BF16)
    vc_ref[0] = z[:, KV_WIDTH:].astype(BF16)


def _ctx_kv(ctx, sh_c, sc_c, w_in, k_g):
    bsz, s, _ = ctx.shape
    assert _K0 % (2 * KV_WIDTH) == 0
    return pl.pallas_call(
        _ctx_kv_kernel,
        grid=(bsz,),
        in_specs=[
            pl.BlockSpec((1, s, D_MODEL), lambda b: (b, 0, 0)),
            pl.BlockSpec((1, D_MODEL), lambda b: (0, 0)), pl.BlockSpec((1, D_MODEL), lambda b: (0, 0)),
            pl.BlockSpec((D_MODEL, 2 * KV_WIDTH), lambda b: (0, _K0 // (2 * KV_WIDTH))),
            pl.BlockSpec((1, HEAD_DIM), lambda b: (0, 0)),
        ],
        out_specs=[pl.BlockSpec((1, s, KV_WIDTH), lambda b: (b, 0, 0))] * 2,
        out_shape=[jax.ShapeDtypeStruct((bsz, s, KV_WIDTH), BF16)] * 2,
        compiler_params=_params("arbitrary"),
        name="ctx_kv",
    )(ctx, sh_c, sc_c, w_in, k_g)


def _attn_scores(q, kc_ref, k_ref):
    return _dot_nt(q, kc_ref[0]), _dot_nt(q, k_ref[0])


def _attn_kernel(q_ref, kc_ref, vc_ref, k_ref, v_ref, qn_ref, kcn_ref, kn_ref, a_ref, s0_ref):
    nc = kc_ref.shape[1]
    first = (pl.program_id(0) == 0) & (pl.program_id(1) == 0) & (pl.program_id(2) == 0)

    @pl.when(first)
    def _():
        s_c, s_x = _attn_scores(q_ref[0, 0, :ATTN_ROWS], kc_ref, k_ref)
        s0_ref[:, :nc] = s_c
        s0_ref[:, nc:] = s_x

    for r0 in range(0, q_ref.shape[2], ATTN_ROWS):
        rows = slice(r0, r0 + ATTN_ROWS)
        for j in range(q_ref.shape[1]):
            if r0 == 0 and j == 0:
                s_c, s_x = s0_ref[:, :nc], s0_ref[:, nc:]
            else:
                s_c, s_x = _attn_scores(q_ref[0, j, rows], kc_ref, k_ref)
            mx = jnp.maximum(jnp.max(s_c, axis=-1, keepdims=True), jnp.max(s_x, axis=-1, keepdims=True))
            p_c = jnp.exp(s_c - mx)
            p_x = jnp.exp(s_x - mx)
            den = jnp.sum(p_c, axis=-1, keepdims=True) + jnp.sum(p_x, axis=-1, keepdims=True)
            o = (_dot(p_c.astype(BF16), vc_ref[0]) + _dot(p_x.astype(BF16), v_ref[0])) / den
            a_ref[0, rows, j * HEAD_DIM:(j + 1) * HEAD_DIM] = o.astype(BF16)
    s_c, s_x = _attn_scores(qn_ref[0, 0], kcn_ref, kn_ref)
    s0_ref[:, :nc] = s_c
    s0_ref[:, nc:] = s_x


def _attention(q, kc, vc, k, v, tq):
    bsz, _, t, _ = q.shape
    s = kc.shape[1]
    nt = t // tq
    last = bsz * N_KV_HEADS * nt - 1

    def nxt(b, h, i):
        n = jnp.minimum((b * N_KV_HEADS + h) * nt + i + 1, last)
        return n // (N_KV_HEADS * nt), (n // nt) % N_KV_HEADS, n % nt

    def qn_map(b, h, i):
        b2, h2, i2 = nxt(b, h, i)
        return b2, h2 * Q_PER_KV, i2 * (tq // ATTN_ROWS), 0

    def kvn_map(b, h, i):
        b2, h2, _ = nxt(b, h, i)
        return b2, 0, h2

    kv_c = pl.BlockSpec((1, s, HEAD_DIM), lambda b, h, i: (b, 0, h))
    kv_x = pl.BlockSpec((1, t, HEAD_DIM), lambda b, h, i: (b, 0, h))
    return pl.pallas_call(
        _attn_kernel,
        grid=(bsz, N_KV_HEADS, nt),
        in_specs=[pl.BlockSpec((1, Q_PER_KV, tq, HEAD_DIM), lambda b, h, i: (b, h, i, 0)),
                  kv_c, kv_c, kv_x, kv_x,
                  pl.BlockSpec((1, 1, ATTN_ROWS, HEAD_DIM), qn_map),
                  pl.BlockSpec((1, s, HEAD_DIM), kvn_map), pl.BlockSpec((1, t, HEAD_DIM), kvn_map)],
        out_specs=pl.BlockSpec((1, tq, Q_PER_KV * HEAD_DIM), lambda b, h, i: (b, i, h)),
        out_shape=jax.ShapeDtypeStruct((bsz, t, A_WIDTH), BF16),
        scratch_shapes=[pltpu.VMEM((ATTN_ROWS, s + t), F32)],
        compiler_params=_params("arbitrary", "arbitrary", "arbitrary"),
        name="attention",
    )(q, kc, vc, k, v, q, kc, k)


def _deepnorm(x, gt, y, pg, pb):
    return _ln(ALPHA * x + gt * y) * pg + pb


def _even_out_kernel(a_ref, u_ref, vn_ref, sg_ref, x_ref, gt_ref, ws_ref, bs_ref, wo_ref, pg_ref, pb_ref,
                     o_ref, comb_ref):
    for r0 in range(0, a_ref.shape[1], ROW_CHAIN):
        rc = slice(r0, r0 + ROW_CHAIN)
        comb_ref[rc, :A_WIDTH] = a_ref[0, rc] * sg_ref[0, rc, :A_WIDTH]
        for n in range(ROW_CHAIN // CHUNK):
            rows = slice(r0 + n * CHUNK, r0 + (n + 1) * CHUNK)
            for g in range(B_GROUPS):
                cols = slice(g * B_GROUP_DIM, (g + 1) * B_GROUP_DIM)
                gcols = slice(A_WIDTH + g * B_GROUP_DIM, A_WIDTH + (g + 1) * B_GROUP_DIM)
                mixed = _dot(ws_ref[g], vn_ref[0, rows, cols]) + bs_ref[g]
                comb_ref[rows, gcols] = (u_ref[0, rows, cols].astype(F32) * mixed
                                         * sg_ref[0, rows, gcols].astype(F32)).astype(BF16)
        y = _dot(comb_ref[rc], wo_ref[...])
        o_ref[0, rc] = _deepnorm(x_ref[0, rc], gt_ref[0], y, pg_ref[...], pb_ref[...])


def _even_out(a, u, vn, sg, x, gt, w_s, b_s, w_out, pg, pb, tm):
    bsz, t, _ = x.shape
    row = lambda b, i: (b, i, 0)
    return pl.pallas_call(
        _even_out_kernel,
        grid=(bsz, t // tm),
        in_specs=[
            pl.BlockSpec((1, tm, A_WIDTH), row), pl.BlockSpec((1, tm, B_WIDTH), row),
            pl.BlockSpec((1, tm, B_WIDTH), row), pl.BlockSpec((1, tm, D_INNER), row),
            pl.BlockSpec((1, tm, D_MODEL), row),
            pl.BlockSpec((1, 1, D_MODEL), lambda b, i: (b, 0, 0)),
            _const_spec((B_GROUPS, CHUNK, CHUNK)), _const_spec((B_GROUPS, CHUNK, B_GROUP_DIM)),
            _const_spec((D_INNER, D_MODEL)),
            _const_spec((1, D_MODEL)), _const_spec((1, D_MODEL)),
        ],
        out_specs=pl.BlockSpec((1, tm, D_MODEL), row),
        out_shape=jax.ShapeDtypeStruct((bsz, t, D_MODEL), F32),
        scratch_shapes=[pltpu.VMEM((tm, D_INNER), BF16)],
        compiler_params=_params("arbitrary", "arbitrary"),
        name="even_out",
    )(a, u, vn, sg, x, gt, w_s, b_s, w_out, pg, pb)


def _proj_odd_kernel(x_ref, sh_ref, sc_ref, w_ref, cs_ref, a_ref, b_ref, m_ref):
    tm = x_ref.shape[1]
    n = tm // FFT_N1
    m32 = _ln(x_ref[0]) * (1.0 + sc_ref[0]) + sh_ref[0]
    n_slab = D_MODEL // 128
    for j in range(n_slab):
        m_ref[j] = m32[:, j * 128:(j + 1) * 128]
    mp = jnp.concatenate(
        [jnp.concatenate([m_ref[j, pl.ds(t1, n, stride=FFT_N1), :] for j in range(n_slab)], axis=1)
         for t1 in range(FFT_N1)], axis=0).astype(BF16)
    cw = 4 * C_GROUP_DIM
    pw = 2 * C_GROUP_DIM
    for c in range(D_INNER // cw):
        z = _dot(mp, w_ref[:, c * cw:(c + 1) * cw]).astype(BF16)
        for j in range(cw // pw):
            pair = c * (cw // pw) + j
            cols = slice(pair * 2 * FFT_SLOTS, (pair + 1) * 2 * FFT_SLOTS)
            ab = _dot(z[:, j * pw:(j + 1) * pw], cs_ref[...])
            for t1 in range(FFT_N1):
                a_ref[0, t1, :, cols] = ab[t1 * n:(t1 + 1) * n, :2 * FFT_SLOTS].astype(BF16)
                b_ref[0, t1, :, cols] = ab[t1 * n:(t1 + 1) * n, 2 * FFT_SLOTS:].astype(BF16)


def _proj_odd(x, sh, sc, w_hb, cs_pair, tm):
    bsz, t, _ = x.shape
    row = lambda b, i: (b, i, 0)
    per_b = pl.BlockSpec((1, 1, D_MODEL), lambda b, i: (b, 0, 0))
    n = tm // FFT_N1
    width = C_GROUPS * FFT_SLOTS
    perm_spec = pl.BlockSpec((1, FFT_N1, n, width), lambda b, i: (b, 0, i, 0))
    perm = jax.ShapeDtypeStruct((bsz, FFT_N1, t // FFT_N1, width), BF16)
    return pl.pallas_call(
        _proj_odd_kernel,
        grid=(bsz, t // tm),
        in_specs=[pl.BlockSpec((1, tm, D_MODEL), row), per_b, per_b,
                  _const_spec((D_MODEL, D_INNER)), _const_spec((2 * C_GROUP_DIM, 2 * C_GROUP_DIM))],
        out_specs=[perm_spec, perm_spec],
        out_shape=[perm, perm],
        scratch_shapes=[pltpu.VMEM((D_MODEL // 128, tm, 128), F32)],
        compiler_params=_params("arbitrary", "arbitrary"),
        name="proj_odd",
    )(x, sh, sc, w_hb, cs_pair)


def _dft4(ar, ai):
    s0r, s0i = ar[0] + ar[2], ai[0] + ai[2]
    s1r, s1i = ar[0] - ar[2], ai[0] - ai[2]
    s2r, s2i = ar[1] + ar[3], ai[1] + ai[3]
    s3r, s3i = ar[1] - ar[3], ai[1] - ai[3]
    return ([s0r + s2r, s1r - s3i, s0r - s2r, s1r + s3i],
            [s0i + s2i, s1i + s3r, s0i - s2i, s1i - s3r])


def _dft8(zr, zi):
    er, ei = _dft4(zr[0::2], zi[0::2])
    orr, oi = _dft4(zr[1::2], zi[1::2])
    h = np.float32(np.sqrt(0.5))
    tr = [orr[0], (orr[1] - oi[1]) * h, -oi[2], (-orr[3] - oi[3]) * h]
    ti = [oi[0], (orr[1] + oi[1]) * h, orr[2], (orr[3] - oi[3]) * h]
    xr = [er[k] + tr[k] for k in range(4)] + [er[k] - tr[k] for k in range(4)]
    xi = [ei[k] + ti[k] for k in range(4)] + [ei[k] - ti[k] for k in range(4)]
    return xr, xi


ROW_CHAIN = 256
ATTN_ROWS = 512
FFT_COLS = 256
FFT_ROWS = 16


def _fft_t_kernel(a_ref, b_ref, cs_ref, fp_ref, fm_ref, g_ref):
    half = FFT_N2
    lane = lax.broadcasted_iota(jnp.int32, (8, 128), 1)
    packed = (lane % FFT_SLOTS) == 0
    for ch in range(a_ref.shape[3] // FFT_COLS):
        c0 = ch * FFT_COLS
        for t1 in range(FFT_N1):
            g_ref[ch, t1, 0] = _dot(cs_ref[t1], a_ref[0, t1, :, c0:c0 + FFT_COLS])
            g_ref[ch, t1, 1] = _dot(cs_ref[t1], b_ref[0, t1, :, c0:c0 + FFT_COLS])
        for r in range(0, FFT_N2, FFT_ROWS):
            for j in range(FFT_COLS // 128):
                lanes = slice(j * 128, (j + 1) * 128)
                plus, minus = [], []
                for r8 in range(r, r + FFT_ROWS, 8):
                    rows = slice(r8, r8 + 8)
                    rows_s = slice(half + r8, half + r8 + 8)
                    pr, pi, mr, mi = [], [], [], []
                    for t1 in range(FFT_N1):
                        p1 = g_ref[ch, t1, 0, rows, lanes]
                        p3 = g_ref[ch, t1, 0, rows_s, lanes]
                        p4 = g_ref[ch, t1, 1, rows, lanes]
                        p2 = g_ref[ch, t1, 1, rows_s, lanes]
                        pr.append(jnp.where(packed, p1, p1 - p2))
                        pi.append(jnp.where(packed, p3, p3 + p4))
                        mr.append(jnp.where(packed, p4, p1 + p2))
                        mi.append(jnp.where(packed, p2, p3 - p4))
                    plus.append(_dft8(pr, pi)[0])
                    minus.append(_dft8(mr, mi)[0])
                for k1 in range(FFT_N1):
                    orow = slice(k1 * FFT_N2 + r, k1 * FFT_N2 + r + FFT_ROWS)
                    ocol = slice(c0 + j * 128, c0 + (j + 1) * 128)
                    fp_ref[0, orow, ocol] = jnp.concatenate([o[k1] for o in plus], axis=0).astype(BF16)
                    fm_ref[0, orow, ocol] = jnp.concatenate([o[k1] for o in minus], axis=0).astype(BF16)


def _fft_t(a, b, cs_t, tn):
    bsz, _, n2, w = a.shape
    t = FFT_N1 * n2
    blk = pl.BlockSpec((1, FFT_N1, n2, tn), lambda bb, j: (bb, 0, 0, j))
    out = pl.BlockSpec((1, t, tn), lambda bb, j: (bb, 0, j))
    return pl.pallas_call(
        _fft_t_kernel,
        grid=(bsz, w // tn),
        in_specs=[blk, blk, _const_spec((FFT_N1, 2 * FFT_N2, FFT_N2))],
        out_specs=[out, out],
        out_shape=[jax.ShapeDtypeStruct((bsz, t, w), BF16)] * 2,
        scratch_shapes=[pltpu.VMEM((tn // FFT_COLS, FFT_N1, 2, 2 * FFT_N2, FFT_COLS), F32)],
        compiler_params=_params("arbitrary", "arbitrary"),
        name="fft_t",
    )(a, b, cs_t)


def _odd_out_kernel(fp_ref, fm_ref, x_ref, sh_ref, sc_ref, gt_ref, wg_ref, wo_ref, pg_ref, pb_ref, o_ref):
    hw = fp_ref.shape[2]
    for r0 in range(0, x_ref.shape[1], ROW_CHAIN):
        rows = slice(r0, r0 + ROW_CHAIN)
        x = x_ref[0, rows]
        m = (_ln(x) * (1.0 + sc_ref[0]) + sh_ref[0]).astype(BF16)
        gate = jax.nn.silu(_dot(m, wg_ref[...]))
        y = (_dot((fp_ref[0, rows].astype(F32) * gate[:, :hw]).astype(BF16), wo_ref[:hw])
             + _dot((fm_ref[0, rows].astype(F32) * gate[:, hw:]).astype(BF16), wo_ref[hw:]))
        o_ref[0, rows] = _deepnorm(x, gt_ref[0], y, pg_ref[...], pb_ref[...])


def _odd_out(fp, fm, x, sh, sc, gt, w_g, w_out, pg, pb, tm):
    bsz, t, _ = x.shape
    row = lambda b, i: (b, i, 0)
    per_b = pl.BlockSpec((1, 1, D_MODEL), lambda b, i: (b, 0, 0))
    hw = fp.shape[2]
    return pl.pallas_call(
        _odd_out_kernel,
        grid=(bsz, t // tm),
        in_specs=[
            pl.BlockSpec((1, tm, hw), row), pl.BlockSpec((1, tm, hw), row), pl.BlockSpec((1, tm, D_MODEL), row),
            per_b, per_b, per_b,
            _const_spec((D_MODEL, D_INNER)), _const_spec((D_INNER, D_MODEL)),
            _const_spec((1, D_MODEL)), _const_spec((1, D_MODEL)),
        ],
        out_specs=pl.BlockSpec((1, tm, D_MODEL), row),
        out_shape=jax.ShapeDtypeStruct((bsz, t, D_MODEL), F32),
        compiler_params=_params("arbitrary", "arbitrary"),
        name="odd_out",
    )(fp, fm, x, sh, sc, gt, w_g, w_out, pg, pb)


def _slot_order(w, axis):
    w = jnp.moveaxis(w, axis, -1)
    g = w.reshape(w.shape[:-1] + (C_GROUPS, C_GROUP_DIM))
    plus = g[..., :FFT_SLOTS]
    minus = jnp.concatenate([g[..., FFT_SLOTS:FFT_SLOTS + 1], g[..., :FFT_SLOTS:-1]], axis=-1)
    flat = lambda v: v.reshape(v.shape[:-2] + (C_GROUPS * FFT_SLOTS,))
    return jnp.moveaxis(jnp.concatenate([flat(plus), flat(minus)], axis=-1), -1, axis)


def _rope_tables(n_tokens):
    rows = n_tokens // GRID_W
    r, cl = jnp.meshgrid(jnp.arange(rows), jnp.arange(GRID_W), indexing="ij")
    row = r.reshape(-1).astype(F32)
    col = cl.reshape(-1).astype(F32)
    n_pairs_axis = HEAD_DIM // 4
    inv_freq = ROPE_THETA ** (-jnp.arange(n_pairs_axis, dtype=F32) / n_pairs_axis)
    ang = jnp.concatenate([row[:, None] * inv_freq, col[:, None] * inv_freq], axis=-1)
    cs, sn = jnp.cos(ang), jnp.sin(ang)
    return jnp.concatenate([cs, cs], axis=-1), jnp.concatenate([-sn, sn], axis=-1)


def _dft_tables(n_tokens):
    assert n_tokens == FFT_N1 * FFT_N2 and C_GROUP_DIM == 2 * FFT_SLOTS
    c = np.arange(C_GROUP_DIM)[:, None]
    sl = np.arange(FFT_SLOTS)[None, :]
    ang = 2.0 * np.pi * c * sl / C_GROUP_DIM
    norm = 1.0 / np.sqrt(float(n_tokens * C_GROUP_DIM))
    re = np.cos(ang) * norm
    im = np.sin(ang) * norm
    im[:, 0] = np.cos(np.pi * c[:, 0]) * norm
    zero = np.zeros_like(re)
    cs_pair = np.block([[re, zero, im, zero], [zero, re, zero, im]])
    k2 = np.arange(FFT_N2)[None, :, None]
    t1 = np.arange(FFT_N1)[:, None, None]
    t2 = np.arange(FFT_N2)[None, None, :]
    ang_t = 2.0 * np.pi * k2 * (t1 + FFT_N1 * t2) / n_tokens
    cs_t = np.concatenate([np.cos(ang_t), np.sin(ang_t)], axis=1)
    f = lambda a: jnp.asarray(np.ascontiguousarray(a), dtype=F32)
    return f(cs_pair).astype(BF16), f(cs_t).astype(BF16)


def kernel(x, c, ctx, c_ctx, w_mod, b_mod, post_ln_g, post_ln_b, even_w_in, even_q_norm, even_k_norm,
           even_v_ln_g, even_v_ln_b, even_w_s, even_b_s, even_w_out, odd_w_in, odd_w_out):
    bsz, t, _ = x.shape
    assert DEPTH == 2 and t % CHUNK == 0
    row1 = lambda v: v.reshape(1, -1)

    n_cond = -(-(bsz + 1) // 8) * 8
    cond = jnp.zeros((n_cond, D_MODEL), F32).at[:bsz].set(c).at[bsz].set(c_ctx)
    mod = _adaln(cond, w_mod, b_mod)
    split = lambda l, rows: [mod[l, rows, i * D_MODEL:(i + 1) * D_MODEL] for i in range(3)]
    sh0, sc0, gt0 = [v[:, None, :] for v in split(0, slice(0, bsz))]
    sh0c, sc0c, _ = split(0, slice(bsz, bsz + 1))
    sh1, sc1, gt1 = [v[:, None, :] for v in split(1, slice(0, bsz))]

    cos2, sin2 = _rope_tables(t)
    cs_pair, cs_t = _dft_tables(t)

    w_in0 = even_w_in[0].astype(BF16)
    q, k, v, u, vn, sg = _proj_even(x, sh0, sc0, w_in0, row1(even_q_norm[0]), row1(even_k_norm[0]),
                                    row1(even_v_ln_g[0]), row1(even_v_ln_b[0]), cos2, sin2, tm=1024)
    kc, vc = _ctx_kv(ctx, sh0c, sc0c, w_in0, row1(even_k_norm[0]))
    a = _attention(q, kc, vc, k, v, tq=1024)
    b_s = jnp.broadcast_to(even_b_s[0][:, :, None], (B_GROUPS, CHUNK, B_GROUP_DIM))
    x1 = _even_out(a, u, vn, sg, x, gt0, even_w_s[0].astype(BF16), b_s, even_w_out[0].astype(BF16),
                   row1(post_ln_g[0]), row1(post_ln_b[0]), tm=1024)

    w_in1 = odd_w_in[0].astype(BF16)
    fa, fb = _proj_odd(x1, sh1, sc1, w_in1[:, :D_INNER], cs_pair, tm=1024)
    fp, fm = _fft_t(fa, fb, cs_t, tn=512)
    return _odd_out(fp, fm, x1, sh1, sc1, gt1, _slot_order(w_in1[:, D_INNER:], 1),
                    _slot_order(odd_w_out[0].astype(BF16), 0), row1(post_ln_g[1]), row1(post_ln_b[1]), tm=1024)
```

```python
import functools

import numpy as np
import jax
import jax.numpy as jnp
from jax import lax
from jax.experimental import pallas as pl
from jax.experimental.pallas import tpu as pltpu

D_MODEL = 1024
DEPTH = 2
GRID_W = 64
D_INNER = 2 * D_MODEL
HEAD_DIM = 128
A_WIDTH = D_INNER // 2
N_Q_HEADS = A_WIDTH // HEAD_DIM
N_KV_HEADS = 2
Q_PER_KV = N_Q_HEADS // N_KV_HEADS
KV_WIDTH = N_KV_HEADS * HEAD_DIM
B_WIDTH = D_INNER - A_WIDTH
CHUNK = 128
B_GROUP_DIM = 128
B_GROUPS = B_WIDTH // B_GROUP_DIM
C_GROUP_DIM = 128
C_GROUPS = D_INNER // C_GROUP_DIM
ROPE_THETA = 10000.0
EVEN_IN = A_WIDTH + 2 * KV_WIDTH + 2 * B_WIDTH + D_INNER
ODD_IN = 2 * D_INNER
ALPHA = (2 * DEPTH) ** 0.25
EPS = 1e-6
LOG2E = 1.4426950408889634

_Q0, _K0, _V0 = 0, A_WIDTH, A_WIDTH + KV_WIDTH
_U0 = A_WIDTH + 2 * KV_WIDTH
_BV0 = _U0 + B_WIDTH
_G0 = _BV0 + B_WIDTH

FFT_N1 = 8
FFT_N2 = 256
FFT_SLOTS = 64

V7X_VMEM_LIMIT_BYTES = 60000 * 1024

F32 = jnp.float32
BF16 = jnp.bfloat16


def _dot(a, b):
    return jnp.dot(a, b, preferred_element_type=F32)


def _dot_nt(a, b):
    return lax.dot_general(a, b, (((1,), (1,)), ((), ())), preferred_element_type=F32)


def _ln(x):
    mu = jnp.mean(x, axis=-1, keepdims=True)
    xc = x - mu
    var = jnp.mean(xc * xc, axis=-1, keepdims=True)
    return xc * lax.rsqrt(var + EPS)


def _rms_head(z, g):
    return z * lax.rsqrt(jnp.mean(z * z, axis=-1, keepdims=True) + EPS) * g


def _rope(y, cos2, sin2):
    return y * cos2 + pltpu.roll(y, HEAD_DIM // 2, 1) * sin2


def _params(*sem):
    return pltpu.CompilerParams(dimension_semantics=sem, vmem_limit_bytes=V7X_VMEM_LIMIT_BYTES)


def _const_spec(shape):
    nd = len(shape)
    return pl.BlockSpec(shape, lambda *_: (0,) * nd, pipeline_mode=pl.Buffered(1))


def _adaln_kernel(c_ref, w_ref, b_ref, o_ref):
    h = jax.nn.silu(c_ref[...])
    w = w_ref[0]
    h_hi = h.astype(BF16)
    h_lo = (h - h_hi.astype(F32)).astype(BF16)
    w_hi = w.astype(BF16)
    w_lo = (w - w_hi.astype(F32)).astype(BF16)
    o_ref[0] = _dot(h_hi, w_hi) + _dot(h_hi, w_lo) + _dot(h_lo, w_hi) + b_ref[0]


def _adaln(cond, w_mod, b_mod):
    r = cond.shape[0]
    tn = D_MODEL
    return pl.pallas_call(
        _adaln_kernel,
        grid=(DEPTH, 3 * D_MODEL // tn),
        in_specs=[
            pl.BlockSpec((r, D_MODEL), lambda l, j: (0, 0)),
            pl.BlockSpec((1, D_MODEL, tn), lambda l, j: (l, 0, j)),
            pl.BlockSpec((1, 1, tn), lambda l, j: (l, 0, j)),
        ],
        out_specs=pl.BlockSpec((1, r, tn), lambda l, j: (l, 0, j)),
        out_shape=jax.ShapeDtypeStruct((DEPTH, r, 3 * D_MODEL), F32),
        compiler_params=_params("arbitrary", "arbitrary"),
        name="adaln",
    )(cond, w_mod, b_mod.reshape(DEPTH, 1, 3 * D_MODEL))


def _proj_even_kernel(x_ref, sh_ref, sc_ref, w_ref, qg_ref, kg_ref, vg_ref, vb_ref, cos_ref, sin_ref,
                      q_ref, k_ref, vt_ref, u_ref, vn_ref, sg_ref):
    qg = qg_ref[...] * (HEAD_DIM ** -0.5 * LOG2E)
    kg = kg_ref[...]
    cw = 4 * HEAD_DIM
    for r0 in range(0, x_ref.shape[1], ROW_CHAIN):
        rows = slice(r0, r0 + ROW_CHAIN)
        m = (_ln(x_ref[0, rows]) * (1.0 + sc_ref[0]) + sh_ref[0]).astype(BF16)
        cos2 = cos_ref[rows]
        sin2 = sin_ref[rows]
        for c in range(A_WIDTH // cw):
            z = _dot(m, w_ref[:, _Q0 + c * cw:_Q0 + (c + 1) * cw])
            for j in range(cw // HEAD_DIM):
                zh = z[:, j * HEAD_DIM:(j + 1) * HEAD_DIM]
                q_ref[0, c * (cw // HEAD_DIM) + j, rows] = _rope(_rms_head(zh, qg), cos2, sin2).astype(BF16)
        z = _dot(m, w_ref[:, _K0:_K0 + 2 * KV_WIDTH])
        for j in range(N_KV_HEADS):
            zh = z[:, j * HEAD_DIM:(j + 1) * HEAD_DIM]
            k_ref[0, rows, j * HEAD_DIM:(j + 1) * HEAD_DIM] = _rope(_rms_head(zh, kg), cos2, sin2).astype(BF16)
        vt_ref[0, :, rows] = z[:, KV_WIDTH:].T.astype(BF16)
        for c in range(B_WIDTH // cw):
            z = _dot(m, w_ref[:, _U0 + c * cw:_U0 + (c + 1) * cw])
            u_ref[0, rows, c * cw:(c + 1) * cw] = jax.nn.gelu(z).astype(BF16)
        gv = jax.nn.gelu(_dot(m, w_ref[:, _BV0:_BV0 + B_WIDTH]))
        vn_ref[0, rows] = (_ln(gv) * vg_ref[...] + vb_ref[...]).astype(BF16)
        for c in range(D_INNER // cw):
            z = _dot(m, w_ref[:, _G0 + c * cw:_G0 + (c + 1) * cw])
            sg_ref[0, rows, c * cw:(c + 1) * cw] = jax.nn.silu(z).astype(BF16)


def _proj_even(x, sh, sc, w_in, q_g, k_g, v_g, v_b, cos2, sin2, tm):
    bsz, t, _ = x.shape
    row = lambda b, i: (b, i, 0)
    per_b = pl.BlockSpec((1, 1, D_MODEL), lambda b, i: (b, 0, 0))
    bf = lambda *s: jax.ShapeDtypeStruct(s, BF16)
    return pl.pallas_call(
        _proj_even_kernel,
        grid=(bsz, t // tm),
        in_specs=[
            pl.BlockSpec((1, tm, D_MODEL), row), per_b, per_b,
            _const_spec((D_MODEL, EVEN_IN)),
            _const_spec((1, HEAD_DIM)), _const_spec((1, HEAD_DIM)),
            _const_spec((1, B_WIDTH)), _const_spec((1, B_WIDTH)),
            pl.BlockSpec((tm, HEAD_DIM), lambda b, i: (i, 0)),
            pl.BlockSpec((tm, HEAD_DIM), lambda b, i: (i, 0)),
        ],
        out_specs=[
            pl.BlockSpec((1, N_Q_HEADS, tm, HEAD_DIM), lambda b, i: (b, 0, i, 0)),
            pl.BlockSpec((1, tm, KV_WIDTH), row), pl.BlockSpec((1, KV_WIDTH, tm), lambda b, i: (b, 0, i)),
            pl.BlockSpec((1, tm, B_WIDTH), row), pl.BlockSpec((1, tm, B_WIDTH), row),
            pl.BlockSpec((1, tm, D_INNER), row),
        ],
        out_shape=[bf(bsz, N_Q_HEADS, t, HEAD_DIM), bf(bsz, t, KV_WIDTH), bf(bsz, KV_WIDTH, t),
                   bf(bsz, t, B_WIDTH), bf(bsz, t, B_WIDTH), bf(bsz, t, D_INNER)],
        compiler_params=_params("arbitrary", "arbitrary"),
        name="proj_even",
    )(x, sh, sc, w_in, q_g, k_g, v_g, v_b, cos2, sin2)


def _ctx_kv_kernel(c_ref, sh_ref, sc_ref, w_ref, kg_ref, kc_ref, vct_ref):
    m = (_ln(c_ref[0]) * (1.0 + sc_ref[...]) + sh_ref[...]).astype(BF16)
    z = _dot(m, w_ref[...])
    for j in range(N_KV_HEADS):
        zh = z[:, j * HEAD_DIM:(j + 1) * HEAD_DIM]
        kc_ref[0, :, j * HEAD_DIM:(j + 1) * HEAD_DIM] = _rms_head(zh, kg_ref[...]).astype(BF16)
    vct_ref[0] = z[:, KV_WIDTH:].T.astype(BF16)


def _ctx_kv(ctx, sh_c, sc_c, w_in, k_g):
    bsz, s, _ = ctx.shape
    assert _K0 % (2 * KV_WIDTH) == 0
    return pl.pallas_call(
        _ctx_kv_kernel,
        grid=(bsz,),
        in_specs=[
            pl.BlockSpec((1, s, D_MODEL), lambda b: (b, 0, 0)),
            pl.BlockSpec((1, D_MODEL), lambda b: (0, 0)), pl.BlockSpec((1, D_MODEL), lambda b: (0, 0)),
            pl.BlockSpec((D_MODEL, 2 * KV_WIDTH), lambda b: (0, _K0 // (2 * KV_WIDTH))),
            pl.BlockSpec((1, HEAD_DIM), lambda b: (0, 0)),
        ],
        out_specs=[pl.BlockSpec((1, s, KV_WIDTH), lambda b: (b, 0, 0)),
                   pl.BlockSpec((1, KV_WIDTH, s), lambda b: (b, 0, 0))],
        out_shape=[jax.ShapeDtypeStruct((bsz, s, KV_WIDTH), BF16), jax.ShapeDtypeStruct((bsz, KV_WIDTH, s), BF16)],
        compiler_params=_params("arbitrary"),
        name="ctx_kv",
    )(ctx, sh_c, sc_c, w_in, k_g)


def _attn_scores(q, kc_ref, k_ref):
    return _dot_nt(kc_ref[0], q), _dot_nt(k_ref[0], q)


def _attn_kernel(q_ref, kc_ref, vct_ref, k_ref, vt_ref, qn_ref, kcn_ref, kn_ref, a_ref, s0_ref):
    nc = kc_ref.shape[1]
    first = (pl.program_id(0) == 0) & (pl.program_id(1) == 0) & (pl.program_id(2) == 0)

    @pl.when(first)
    def _():
        s_c, s_x = _attn_scores(q_ref[0, 0, :ATTN_ROWS], kc_ref, k_ref)
        s0_ref[:nc] = s_c
        s0_ref[nc:] = s_x

    chains = [(r0, j) for r0 in range(0, q_ref.shape[2], ATTN_ROWS) for j in range(q_ref.shape[1])]
    scores = (s0_ref[:nc], s0_ref[nc:])
    for idx, (r0, j) in enumerate(chains):
        if idx + 1 < len(chains):
            r1, j1 = chains[idx + 1]
            nxt_scores = _attn_scores(q_ref[0, j1, r1:r1 + ATTN_ROWS], kc_ref, k_ref)
        else:
            nxt_scores = _attn_scores(qn_ref[0, 0], kcn_ref, kn_ref)
        s_c, s_x = scores
        mx = jnp.maximum(jnp.max(s_c, axis=0, keepdims=True), jnp.max(s_x, axis=0, keepdims=True))
        e = jnp.exp2(s_c - mx)
        den = jnp.sum(e, axis=0, keepdims=True)
        o_t = _dot(vct_ref[0], e.astype(BF16))
        for c0 in range(0, s_x.shape[0], ATTN_KEYS):
            e = jnp.exp2(s_x[c0:c0 + ATTN_KEYS] - mx)
            den = den + jnp.sum(e, axis=0, keepdims=True)
            o_t = o_t + _dot(vt_ref[0, :, c0:c0 + ATTN_KEYS], e.astype(BF16))
        a_ref[0, r0:r0 + ATTN_ROWS, j * HEAD_DIM:(j + 1) * HEAD_DIM] = (o_t / den).T.astype(BF16)
        scores = nxt_scores
    s0_ref[:nc] = scores[0]
    s0_ref[nc:] = scores[1]


def _attention(q, kc, vct, k, vt, tq):
    bsz, _, t, _ = q.shape
    s = kc.shape[1]
    nt = t // tq
    last = bsz * N_KV_HEADS * nt - 1

    def nxt(b, h, i):
        n = jnp.minimum((b * N_KV_HEADS + h) * nt + i + 1, last)
        return n // (N_KV_HEADS * nt), (n // nt) % N_KV_HEADS, n % nt

    def qn_map(b, h, i):
        b2, h2, i2 = nxt(b, h, i)
        return b2, h2 * Q_PER_KV, i2 * (tq // ATTN_ROWS), 0

    def kn_map(b, h, i):
        b2, h2, _ = nxt(b, h, i)
        return b2, 0, h2

    k_c = pl.BlockSpec((1, s, HEAD_DIM), lambda b, h, i: (b, 0, h))
    k_x = pl.BlockSpec((1, t, HEAD_DIM), lambda b, h, i: (b, 0, h))
    v_c = pl.BlockSpec((1, HEAD_DIM, s), lambda b, h, i: (b, h, 0))
    v_x = pl.BlockSpec((1, HEAD_DIM, t), lambda b, h, i: (b, h, 0))
    return pl.pallas_call(
        _attn_kernel,
        grid=(bsz, N_KV_HEADS, nt),
        in_specs=[pl.BlockSpec((1, Q_PER_KV, tq, HEAD_DIM), lambda b, h, i: (b, h, i, 0)),
                  k_c, v_c, k_x, v_x,
                  pl.BlockSpec((1, 1, ATTN_ROWS, HEAD_DIM), qn_map),
                  pl.BlockSpec((1, s, HEAD_DIM), kn_map), pl.BlockSpec((1, t, HEAD_DIM), kn_map)],
        out_specs=pl.BlockSpec((1, tq, Q_PER_KV * HEAD_DIM), lambda b, h, i: (b, i, h)),
        out_shape=jax.ShapeDtypeStruct((bsz, t, A_WIDTH), BF16),
        scratch_shapes=[pltpu.VMEM((s + t, ATTN_ROWS), F32)],
        compiler_params=_params("arbitrary", "arbitrary", "arbitrary"),
        name="attention",
    )(q, kc, vct, k, vt, q, kc, k)


def _deepnorm(x, gt, y, pg, pb):
    return _ln(ALPHA * x + gt * y) * pg + pb


def _even_out_kernel(a_ref, u_ref, vn_ref, sg_ref, x_ref, gt_ref, ws_ref, bs_ref, wo_ref, pg_ref, pb_ref,
                     o_ref, comb_ref):
    for r0 in range(0, a_ref.shape[1], ROW_CHAIN):
        rc = slice(r0, r0 + ROW_CHAIN)
        comb_ref[rc, :A_WIDTH] = a_ref[0, rc] * sg_ref[0, rc, :A_WIDTH]
        for n in range(ROW_CHAIN // CHUNK):
            rows = slice(r0 + n * CHUNK, r0 + (n + 1) * CHUNK)
            for g in range(B_GROUPS):
                cols = slice(g * B_GROUP_DIM, (g + 1) * B_GROUP_DIM)
                gcols = slice(A_WIDTH + g * B_GROUP_DIM, A_WIDTH + (g + 1) * B_GROUP_DIM)
                mixed = _dot(ws_ref[g], vn_ref[0, rows, cols]) + bs_ref[g]
                comb_ref[rows, gcols] = (u_ref[0, rows, cols].astype(F32) * mixed
                                         * sg_ref[0, rows, gcols].astype(F32)).astype(BF16)
        y = _dot(comb_ref[rc], wo_ref[...])
        o_ref[0, rc] = _deepnorm(x_ref[0, rc], gt_ref[0], y, pg_ref[...], pb_ref[...])


def _even_out(a, u, vn, sg, x, gt, w_s, b_s, w_out, pg, pb, tm):
    bsz, t, _ = x.shape
    row = lambda b, i: (b, i, 0)
    return pl.pallas_call(
        _even_out_kernel,
        grid=(bsz, t // tm),
        in_specs=[
            pl.BlockSpec((1, tm, A_WIDTH), row), pl.BlockSpec((1, tm, B_WIDTH), row),
            pl.BlockSpec((1, tm, B_WIDTH), row), pl.BlockSpec((1, tm, D_INNER), row),
            pl.BlockSpec((1, tm, D_MODEL), row),
            pl.BlockSpec((1, 1, D_MODEL), lambda b, i: (b, 0, 0)),
            _const_spec((B_GROUPS, CHUNK, CHUNK)), _const_spec((B_GROUPS, CHUNK, B_GROUP_DIM)),
            _const_spec((D_INNER, D_MODEL)),
            _const_spec((1, D_MODEL)), _const_spec((1, D_MODEL)),
        ],
        out_specs=pl.BlockSpec((1, tm, D_MODEL), row),
        out_shape=jax.ShapeDtypeStruct((bsz, t, D_MODEL), F32),
        scratch_shapes=[pltpu.VMEM((tm, D_INNER), BF16)],
        compiler_params=_params("arbitrary", "arbitrary"),
        name="even_out",
    )(a, u, vn, sg, x, gt, w_s, b_s, w_out, pg, pb)


def _proj_odd_kernel(x_ref, sh_ref, sc_ref, w_ref, cs_ref, a_ref, b_ref, m_ref):
    tm = x_ref.shape[1]
    n = tm // FFT_N1
    m32 = _ln(x_ref[0]) * (1.0 + sc_ref[0]) + sh_ref[0]
    n_slab = D_MODEL // 128
    for j in range(n_slab):
        m_ref[j] = m32[:, j * 128:(j + 1) * 128]
    mp = jnp.concatenate(
        [jnp.concatenate([m_ref[j, pl.ds(t1, n, stride=FFT_N1), :] for j in range(n_slab)], axis=1)
         for t1 in range(FFT_N1)], axis=0).astype(BF16)
    cw = 4 * C_GROUP_DIM
    pw = 2 * C_GROUP_DIM
    for c in range(D_INNER // cw):
        z = _dot(mp, w_ref[:, c * cw:(c + 1) * cw]).astype(BF16)
        for j in range(cw // pw):
            pair = c * (cw // pw) + j
            cols = slice(pair * 2 * FFT_SLOTS, (pair + 1) * 2 * FFT_SLOTS)
            ab = _dot(z[:, j * pw:(j + 1) * pw], cs_ref[...])
            for t1 in range(FFT_N1):
                a_ref[0, t1, :, cols] = ab[t1 * n:(t1 + 1) * n, :2 * FFT_SLOTS].astype(BF16)
                b_ref[0, t1, :, cols] = ab[t1 * n:(t1 + 1) * n, 2 * FFT_SLOTS:].astype(BF16)


def _proj_odd(x, sh, sc, w_hb, cs_pair, tm):
    bsz, t, _ = x.shape
    row = lambda b, i: (b, i, 0)
    per_b = pl.BlockSpec((1, 1, D_MODEL), lambda b, i: (b, 0, 0))
    n = tm // FFT_N1
    width = C_GROUPS * FFT_SLOTS
    perm_spec = pl.BlockSpec((1, FFT_N1, n, width), lambda b, i: (b, 0, i, 0))
    perm = jax.ShapeDtypeStruct((bsz, FFT_N1, t // FFT_N1, width), BF16)
    return pl.pallas_call(
        _proj_odd_kernel,
        grid=(bsz, t // tm),
        in_specs=[pl.BlockSpec((1, tm, D_MODEL), row), per_b, per_b,
                  _const_spec((D_MODEL, D_INNER)), _const_spec((2 * C_GROUP_DIM, 2 * C_GROUP_DIM))],
        out_specs=[perm_spec, perm_spec],
        out_shape=[perm, perm],
        scratch_shapes=[pltpu.VMEM((D_MODEL // 128, tm, 128), F32)],
        compiler_params=_params("arbitrary", "arbitrary"),
        name="proj_odd",
    )(x, sh, sc, w_hb, cs_pair)


def _dft4(ar, ai):
    s0r, s0i = ar[0] + ar[2], ai[0] + ai[2]
    s1r, s1i = ar[0] - ar[2], ai[0] - ai[2]
    s2r, s2i = ar[1] + ar[3], ai[1] + ai[3]
    s3r, s3i = ar[1] - ar[3], ai[1] - ai[3]
    return ([s0r + s2r, s1r - s3i, s0r - s2r, s1r + s3i],
            [s0i + s2i, s1i + s3r, s0i - s2i, s1i - s3r])


def _dft8(zr, zi):
    er, ei = _dft4(zr[0::2], zi[0::2])
    orr, oi = _dft4(zr[1::2], zi[1::2])
    h = np.float32(np.sqrt(0.5))
    tr = [orr[0], (orr[1] - oi[1]) * h, -oi[2], (-orr[3] - oi[3]) * h]
    ti = [oi[0], (orr[1] + oi[1]) * h, orr[2], (orr[3] - oi[3]) * h]
    xr = [er[k] + tr[k] for k in range(4)] + [er[k] - tr[k] for k in range(4)]
    xi = [ei[k] + ti[k] for k in range(4)] + [ei[k] - ti[k] for k in range(4)]
    return xr, xi


ROW_CHAIN = 256
ATTN_ROWS = 512
ATTN_KEYS = 256
FFT_COLS = 256
FFT_ROWS = 16


def _fft_t_kernel(a_ref, b_ref, cs_ref, fp_ref, fm_ref, g_ref):
    half = FFT_N2
    lane = lax.broadcasted_iota(jnp.int32, (8, 128), 1)
    packed = (lane % FFT_SLOTS) == 0
    for ch in range(a_ref.shape[3] // FFT_COLS):
        c0 = ch * FFT_COLS
        for t1 in range(FFT_N1):
            g_ref[ch, t1, 0] = _dot(cs_ref[t1], a_ref[0, t1, :, c0:c0 + FFT_COLS])
            g_ref[ch, t1, 1] = _dot(cs_ref[t1], b_ref[0, t1, :, c0:c0 + FFT_COLS])
        for r in range(0, FFT_N2, FFT_ROWS):
            for j in range(FFT_COLS // 128):
                lanes = slice(j * 128, (j + 1) * 128)
                plus, minus = [], []
                for r8 in range(r, r + FFT_ROWS, 8):
                    rows = slice(r8, r8 + 8)
                    rows_s = slice(half + r8, half + r8 + 8)
                    pr, pi, mr, mi = [], [], [], []
                    for t1 in range(FFT_N1):
                        p1 = g_ref[ch, t1, 0, rows, lanes]
                        p3 = g_ref[ch, t1, 0, rows_s, lanes]
                        p4 = g_ref[ch, t1, 1, rows, lanes]
                        p2 = g_ref[ch, t1, 1, rows_s, lanes]
                        pr.append(jnp.where(packed, p1, p1 - p2))
                        pi.append(jnp.where(packed, p3, p3 + p4))
                        mr.append(jnp.where(packed, p4, p1 + p2))
                        mi.append(jnp.where(packed, p2, p3 - p4))
                    plus.append(_dft8(pr, pi)[0])
                    minus.append(_dft8(mr, mi)[0])
                for k1 in range(FFT_N1):
                    orow = slice(k1 * FFT_N2 + r, k1 * FFT_N2 + r + FFT_ROWS)
                    ocol = slice(c0 + j * 128, c0 + (j + 1) * 128)
                    fp_ref[0, orow, ocol] = jnp.concatenate([o[k1] for o in plus], axis=0).astype(BF16)
                    fm_ref[0, orow, ocol] = jnp.concatenate([o[k1] for o in minus], axis=0).astype(BF16)


def _fft_t(a, b, cs_t, tn):
    bsz, _, n2, w = a.shape
    t = FFT_N1 * n2
    blk = pl.BlockSpec((1, FFT_N1, n2, tn), lambda bb, j: (bb, 0, 0, j))
    out = pl.BlockSpec((1, t, tn), lambda bb, j: (bb, 0, j))
    return pl.pallas_call(
        _fft_t_kernel,
        grid=(bsz, w // tn),
        in_specs=[blk, blk, _const_spec((FFT_N1, 2 * FFT_N2, FFT_N2))],
        out_specs=[out, out],
        out_shape=[jax.ShapeDtypeStruct((bsz, t, w), BF16)] * 2,
        scratch_shapes=[pltpu.VMEM((tn // FFT_COLS, FFT_N1, 2, 2 * FFT_N2, FFT_COLS), F32)],
        compiler_params=_params("arbitrary", "arbitrary"),
        name="fft_t",
    )(a, b, cs_t)


def _odd_out_kernel(fp_ref, fm_ref, x_ref, sh_ref, sc_ref, gt_ref, wg_ref, wo_ref, pg_ref, pb_ref, o_ref):
    hw = fp_ref.shape[2]
    for r0 in range(0, x_ref.shape[1], ROW_CHAIN):
        rows = slice(r0, r0 + ROW_CHAIN)
        x = x_ref[0, rows]
        m = (_ln(x) * (1.0 + sc_ref[0]) + sh_ref[0]).astype(BF16)
        gate = jax.nn.silu(_dot(m, wg_ref[...]))
        y = (_dot((fp_ref[0, rows].astype(F32) * gate[:, :hw]).astype(BF16), wo_ref[:hw])
             + _dot((fm_ref[0, rows].astype(F32) * gate[:, hw:]).astype(BF16), wo_ref[hw:]))
        o_ref[0, rows] = _deepnorm(x, gt_ref[0], y, pg_ref[...], pb_ref[...])


def _odd_out(fp, fm, x, sh, sc, gt, w_g, w_out, pg, pb, tm):
    bsz, t, _ = x.shape
    row = lambda b, i: (b, i, 0)
    per_b = pl.BlockSpec((1, 1, D_MODEL), lambda b, i: (b, 0, 0))
    hw = fp.shape[2]
    return pl.pallas_call(
        _odd_out_kernel,
        grid=(bsz, t // tm),
        in_specs=[
            pl.BlockSpec((1, tm, hw), row), pl.BlockSpec((1, tm, hw), row), pl.BlockSpec((1, tm, D_MODEL), row),
            per_b, per_b, per_b,
            _const_spec((D_MODEL, D_INNER)), _const_spec((D_INNER, D_MODEL)),
            _const_spec((1, D_MODEL)), _const_spec((1, D_MODEL)),
        ],
        out_specs=pl.BlockSpec((1, tm, D_MODEL), row),
        out_shape=jax.ShapeDtypeStruct((bsz, t, D_MODEL), F32),
        compiler_params=_params("arbitrary", "arbitrary"),
        name="odd_out",
    )(fp, fm, x, sh, sc, gt, w_g, w_out, pg, pb)


def _slot_order(w, axis):
    w = jnp.moveaxis(w, axis, -1)
    g = w.reshape(w.shape[:-1] + (C_GROUPS, C_GROUP_DIM))
    plus = g[..., :FFT_SLOTS]
    minus = jnp.concatenate([g[..., FFT_SLOTS:FFT_SLOTS + 1], g[..., :FFT_SLOTS:-1]], axis=-1)
    flat = lambda v: v.reshape(v.shape[:-2] + (C_GROUPS * FFT_SLOTS,))
    return jnp.moveaxis(jnp.concatenate([flat(plus), flat(minus)], axis=-1), -1, axis)


def _rope_tables(n_tokens):
    rows = n_tokens // GRID_W
    r, cl = jnp.meshgrid(jnp.arange(rows), jnp.arange(GRID_W), indexing="ij")
    row = r.reshape(-1).astype(F32)
    col = cl.reshape(-1).astype(F32)
    n_pairs_axis = HEAD_DIM // 4
    inv_freq = ROPE_THETA ** (-jnp.arange(n_pairs_axis, dtype=F32) / n_pairs_axis)
    ang = jnp.concatenate([row[:, None] * inv_freq, col[:, None] * inv_freq], axis=-1)
    cs, sn = jnp.cos(ang), jnp.sin(ang)
    return jnp.concatenate([cs, cs], axis=-1), jnp.concatenate([-sn, sn], axis=-1)


def _dft_tables(n_tokens):
    assert n_tokens == FFT_N1 * FFT_N2 and C_GROUP_DIM == 2 * FFT_SLOTS
    c = np.arange(C_GROUP_DIM)[:, None]
    sl = np.arange(FFT_SLOTS)[None, :]
    ang = 2.0 * np.pi * c * sl / C_GROUP_DIM
    norm = 1.0 / np.sqrt(float(n_tokens * C_GROUP_DIM))
    re = np.cos(ang) * norm
    im = np.sin(ang) * norm
    im[:, 0] = np.cos(np.pi * c[:, 0]) * norm
    zero = np.zeros_like(re)
    cs_pair = np.block([[re, zero, im, zero], [zero, re, zero, im]])
    k2 = np.arange(FFT_N2)[None, :, None]
    t1 = np.arange(FFT_N1)[:, None, None]
    t2 = np.arange(FFT_N2)[None, None, :]
    ang_t = 2.0 * np.pi * k2 * (t1 + FFT_N1 * t2) / n_tokens
    cs_t = np.concatenate([np.cos(ang_t), np.sin(ang_t)], axis=1)
    f = lambda a: jnp.asarray(np.ascontiguousarray(a), dtype=F32)
    return f(cs_pair).astype(BF16), f(cs_t).astype(BF16)


def kernel(x, c, ctx, c_ctx, w_mod, b_mod, post_ln_g, post_ln_b, even_w_in, even_q_norm, even_k_norm,
           even_v_ln_g, even_v_ln_b, even_w_s, even_b_s, even_w_out, odd_w_in, odd_w_out):
    bsz, t, _ = x.shape
    assert DEPTH == 2 and t % CHUNK == 0
    row1 = lambda v: v.reshape(1, -1)

    n_cond = -(-(bsz + 1) // 8) * 8
    cond = jnp.zeros((n_cond, D_MODEL), F32).at[:bsz].set(c).at[bsz].set(c_ctx)
    mod = _adaln(cond, w_mod, b_mod)
    split = lambda l, rows: [mod[l, rows, i * D_MODEL:(i + 1) * D_MODEL] for i in range(3)]
    sh0, sc0, gt0 = [v[:, None, :] for v in split(0, slice(0, bsz))]
    sh0c, sc0c, _ = split(0, slice(bsz, bsz + 1))
    sh1, sc1, gt1 = [v[:, None, :] for v in split(1, slice(0, bsz))]

    cos2, sin2 = _rope_tables(t)
    cs_pair, cs_t = _dft_tables(t)

    w_in0 = even_w_in[0].astype(BF16)
    q, k, vt, u, vn, sg = _proj_even(x, sh0, sc0, w_in0, row1(even_q_norm[0]), row1(even_k_norm[0]),
                                    row1(even_v_ln_g[0]), row1(even_v_ln_b[0]), cos2, sin2, tm=1024)
    kc, vct = _ctx_kv(ctx, sh0c, sc0c, w_in0, row1(even_k_norm[0]))
    a = _attention(q, kc, vct, k, vt, tq=1024)
    b_s = jnp.broadcast_to(even_b_s[0][:, :, None], (B_GROUPS, CHUNK, B_GROUP_DIM))
    x1 = _even_out(a, u, vn, sg, x, gt0, even_w_s[0].astype(BF16), b_s, even_w_out[0].astype(BF16),
                   row1(post_ln_g[0]), row1(post_ln_b[0]), tm=1024)

    w_in1 = odd_w_in[0].astype(BF16)
    fa, fb = _proj_odd(x1, sh1, sc1, w_in1[:, :D_INNER], cs_pair, tm=1024)
    fp, fm = _fft_t(fa, fb, cs_t, tn=512)
    return _odd_out(fp, fm, x1, sh1, sc1, gt1, _slot_order(w_in1[:, D_INNER:], 1),
                    _slot_order(odd_w_out[0].astype(BF16), 0), row1(post_ln_g[1]), row1(post_ln_b[1]), tm=1024)
```

```python
import functools

import numpy as np
import jax
import jax.numpy as jnp
from jax import lax
from jax.experimental import pallas as pl
from jax.experimental.pallas import tpu as pltpu

D_MODEL = 1024
DEPTH = 2
GRID_W = 64
D_INNER = 2 * D_MODEL
HEAD_DIM = 128
A_WIDTH = D_INNER // 2
N_Q_HEADS = A_WIDTH // HEAD_DIM
N_KV_HEADS = 2
Q_PER_KV = N_Q_HEADS // N_KV_HEADS
KV_WIDTH = N_KV_HEADS * HEAD_DIM
B_WIDTH = D_INNER - A_WIDTH
CHUNK = 128
B_GROUP_DIM = 128
B_GROUPS = B_WIDTH // B_GROUP_DIM
C_GROUP_DIM = 128
C_GROUPS = D_INNER // C_GROUP_DIM
ROPE_THETA = 10000.0
EVEN_IN = A_WIDTH + 2 * KV_WIDTH + 2 * B_WIDTH + D_INNER
ODD_IN = 2 * D_INNER
ALPHA = (2 * DEPTH) ** 0.25
EPS = 1e-6

_Q0, _K0, _V0 = 0, A_WIDTH, A_WIDTH + KV_WIDTH
_U0 = A_WIDTH + 2 * KV_WIDTH
_BV0 = _U0 + B_WIDTH
_G0 = _BV0 + B_WIDTH

FFT_N1 = 8
FFT_N2 = 256
FFT_SLOTS = 64

V7X_VMEM_LIMIT_BYTES = 60000 * 1024

F32 = jnp.float32
BF16 = jnp.bfloat16


def _dot(a, b):
    return jnp.dot(a, b, preferred_element_type=F32)


def _dot_nt(a, b):
    return lax.dot_general(a, b, (((1,), (1,)), ((), ())), preferred_element_type=F32)


def _ln(x):
    mu = jnp.mean(x, axis=-1, keepdims=True)
    xc = x - mu
    var = jnp.mean(xc * xc, axis=-1, keepdims=True)
    return xc * lax.rsqrt(var + EPS)


def _rms_head(z, g):
    return z * lax.rsqrt(jnp.mean(z * z, axis=-1, keepdims=True) + EPS) * g


def _rope(y, cos2, sin2):
    return y * cos2 + pltpu.roll(y, HEAD_DIM // 2, 1) * sin2


def _params(*sem):
    return pltpu.CompilerParams(dimension_semantics=sem, vmem_limit_bytes=V7X_VMEM_LIMIT_BYTES)


def _const_spec(shape):
    nd = len(shape)
    return pl.BlockSpec(shape, lambda *_: (0,) * nd, pipeline_mode=pl.Buffered(1))


def _adaln_kernel(c_ref, w_ref, b_ref, o_ref):
    h = jax.nn.silu(c_ref[...])
    w = w_ref[0]
    h_hi = h.astype(BF16)
    h_lo = (h - h_hi.astype(F32)).astype(BF16)
    w_hi = w.astype(BF16)
    w_lo = (w - w_hi.astype(F32)).astype(BF16)
    o_ref[0] = _dot(h_hi, w_hi) + _dot(h_hi, w_lo) + _dot(h_lo, w_hi) + b_ref[0]


def _adaln(cond, w_mod, b_mod):
    r = cond.shape[0]
    tn = D_MODEL
    return pl.pallas_call(
        _adaln_kernel,
        grid=(DEPTH, 3 * D_MODEL // tn),
        in_specs=[
            pl.BlockSpec((r, D_MODEL), lambda l, j: (0, 0)),
            pl.BlockSpec((1, D_MODEL, tn), lambda l, j: (l, 0, j)),
            pl.BlockSpec((1, 1, tn), lambda l, j: (l, 0, j)),
        ],
        out_specs=pl.BlockSpec((1, r, tn), lambda l, j: (l, 0, j)),
        out_shape=jax.ShapeDtypeStruct((DEPTH, r, 3 * D_MODEL), F32),
        compiler_params=_params("arbitrary", "arbitrary"),
        name="adaln",
    )(cond, w_mod, b_mod.reshape(DEPTH, 1, 3 * D_MODEL))


def _proj_even_kernel(x_ref, sh_ref, sc_ref, w_ref, qg_ref, kg_ref, vg_ref, vb_ref, cos_ref, sin_ref,
                      q_ref, k_ref, v_ref, u_ref, vn_ref, sg_ref):
    qg = qg_ref[...] * (HEAD_DIM ** -0.5)
    kg = kg_ref[...]
    cw = 4 * HEAD_DIM
    for r0 in range(0, x_ref.shape[1], ROW_CHAIN):
        rows = slice(r0, r0 + ROW_CHAIN)
        m = (_ln(x_ref[0, rows]) * (1.0 + sc_ref[0]) + sh_ref[0]).astype(BF16)
        cos2 = cos_ref[rows]
        sin2 = sin_ref[rows]
        for c in range(A_WIDTH // cw):
            z = _dot(m, w_ref[:, _Q0 + c * cw:_Q0 + (c + 1) * cw])
            for j in range(cw // HEAD_DIM):
                zh = z[:, j * HEAD_DIM:(j + 1) * HEAD_DIM]
                q_ref[0, c * (cw // HEAD_DIM) + j, rows] = _rope(_rms_head(zh, qg), cos2, sin2).astype(BF16)
        z = _dot(m, w_ref[:, _K0:_K0 + 2 * KV_WIDTH])
        for j in range(N_KV_HEADS):
            zh = z[:, j * HEAD_DIM:(j + 1) * HEAD_DIM]
            k_ref[0, rows, j * HEAD_DIM:(j + 1) * HEAD_DIM] = _rope(_rms_head(zh, kg), cos2, sin2).astype(BF16)
        v_ref[0, rows] = z[:, KV_WIDTH:].astype(BF16)
        for c in range(B_WIDTH // cw):
            z = _dot(m, w_ref[:, _U0 + c * cw:_U0 + (c + 1) * cw])
            u_ref[0, rows, c * cw:(c + 1) * cw] = jax.nn.gelu(z).astype(BF16)
        gv = jax.nn.gelu(_dot(m, w_ref[:, _BV0:_BV0 + B_WIDTH]))
        vn_ref[0, rows] = (_ln(gv) * vg_ref[...] + vb_ref[...]).astype(BF16)
        for c in range(D_INNER // cw):
            z = _dot(m, w_ref[:, _G0 + c * cw:_G0 + (c + 1) * cw])
            sg_ref[0, rows, c * cw:(c + 1) * cw] = jax.nn.silu(z).astype(BF16)


def _proj_even(x, sh, sc, w_in, q_g, k_g, v_g, v_b, cos2, sin2, tm):
    bsz, t, _ = x.shape
    row = lambda b, i: (b, i, 0)
    per_b = pl.BlockSpec((1, 1, D_MODEL), lambda b, i: (b, 0, 0))
    bf = lambda *s: jax.ShapeDtypeStruct(s, BF16)
    return pl.pallas_call(
        _proj_even_kernel,
        grid=(bsz, t // tm),
        in_specs=[
            pl.BlockSpec((1, tm, D_MODEL), row), per_b, per_b,
            _const_spec((D_MODEL, EVEN_IN)),
            _const_spec((1, HEAD_DIM)), _const_spec((1, HEAD_DIM)),
            _const_spec((1, B_WIDTH)), _const_spec((1, B_WIDTH)),
            pl.BlockSpec((tm, HEAD_DIM), lambda b, i: (i, 0)),
            pl.BlockSpec((tm, HEAD_DIM), lambda b, i: (i, 0)),
        ],
        out_specs=[
            pl.BlockSpec((1, N_Q_HEADS, tm, HEAD_DIM), lambda b, i: (b, 0, i, 0)),
            pl.BlockSpec((1, tm, KV_WIDTH), row), pl.BlockSpec((1, tm, KV_WIDTH), row),
            pl.BlockSpec((1, tm, B_WIDTH), row), pl.BlockSpec((1, tm, B_WIDTH), row),
            pl.BlockSpec((1, tm, D_INNER), row),
        ],
        out_shape=[bf(bsz, N_Q_HEADS, t, HEAD_DIM), bf(bsz, t, KV_WIDTH), bf(bsz, t, KV_WIDTH),
                   bf(bsz, t, B_WIDTH), bf(bsz, t, B_WIDTH), bf(bsz, t, D_INNER)],
        compiler_params=_params("arbitrary", "arbitrary"),
        name="proj_even",
    )(x, sh, sc, w_in, q_g, k_g, v_g, v_b, cos2, sin2)


def _ctx_kv_kernel(c_ref, sh_ref, sc_ref, w_ref, kg_ref, kc_ref, vc_ref):
    m = (_ln(c_ref[0]) * (1.0 + sc_ref[...]) + sh_ref[...]).astype(BF16)
    z = _dot(m, w_ref[...])
    for j in range(N_KV_HEADS):
        zh = z[:, j * HEAD_DIM:(j + 1) * HEAD_DIM]
        kc_ref[0, :, j * HEAD_DIM:(j + 1) * HEAD_DIM] = _rms_head(zh, kg_ref[...]).astype(BF16)
    vc_ref[0] = z[:, KV_WIDTH:].astype(BF16)


def _ctx_kv(ctx, sh_c, sc_c, w_in, k_g):
    bsz, s, _ = ctx.shape
    assert _K0 % (2 * KV_WIDTH) == 0
    return pl.pallas_call(
        _ctx_kv_kernel,
        grid=(bsz,),
        in_specs=[
            pl.BlockSpec((1, s, D_MODEL), lambda b: (b, 0, 0)),
            pl.BlockSpec((1, D_MODEL), lambda b: (0, 0)), pl.BlockSpec((1, D_MODEL), lambda b: (0, 0)),
            pl.BlockSpec((D_MODEL, 2 * KV_WIDTH), lambda b: (0, _K0 // (2 * KV_WIDTH))),
            pl.BlockSpec((1, HEAD_DIM), lambda b: (0, 0)),
        ],
        out_specs=[pl.BlockSpec((1, s, KV_WIDTH), lambda b: (b, 0, 0))] * 2,
        out_shape=[jax.ShapeDtypeStruct((bsz, s, KV_WIDTH), BF16)] * 2,
        compiler_params=_params("arbitrary"),
        name="ctx_kv",
    )(ctx, sh_c, sc_c, w_in, k_g)


def _attn_scores(q, kc_ref, k_ref):
    return _dot_nt(q, kc_ref[0]), _dot_nt(q, k_ref[0])


def _attn_kernel(q_ref, kc_ref, vc_ref, k_ref, v_ref, qn_ref, kcn_ref, kn_ref, a_ref, s0_ref):
    nc = kc_ref.shape[1]
    first = (pl.program_id(0) == 0) & (pl.program_id(1) == 0) & (pl.program_id(2) == 0)

    @pl.when(first)
    def _():
        s_c, s_x = _attn_scores(q_ref[0, 0, :ATTN_ROWS], kc_ref, k_ref)
        s0_ref[:, :nc] = s_c
        s0_ref[:, nc:] = s_x

    for r0 in range(0, q_ref.shape[2], ATTN_ROWS):
        rows = slice(r0, r0 + ATTN_ROWS)
        for j in range(q_ref.shape[1]):
            if r0 == 0 and j == 0:
                s_c, s_x = s0_ref[:, :nc], s0_ref[:, nc:]
            else:
                s_c, s_x = _attn_scores(q_ref[0, j, rows], kc_ref, k_ref)
            mx = jnp.maximum(jnp.max(s_c, axis=-1, keepdims=True), jnp.max(s_x, axis=-1, keepdims=True))
            p_c = jnp.exp(s_c - mx)
            p_x = jnp.exp(s_x - mx)
            den = jnp.sum(p_c, axis=-1, keepdims=True) + jnp.sum(p_x, axis=-1, keepdims=True)
            o = (_dot(p_c.astype(BF16), vc_ref[0]) + _dot(p_x.astype(BF16), v_ref[0])) / den
            a_ref[0, rows, j * HEAD_DIM:(j + 1) * HEAD_DIM] = o.astype(BF16)
    s_c, s_x = _attn_scores(qn_ref[0, 0], kcn_ref, kn_ref)
    s0_ref[:, :nc] = s_c
    s0_ref[:, nc:] = s_x


def _attention(q, kc, vc, k, v, tq):
    bsz, _, t, _ = q.shape
    s = kc.shape[1]
    nt = t // tq
    last = bsz * N_KV_HEADS * nt - 1

    def nxt(b, h, i):
        n = jnp.minimum((b * N_KV_HEADS + h) * nt + i + 1, last)
        return n // (N_KV_HEADS * nt), (n // nt) % N_KV_HEADS, n % nt

    def qn_map(b, h, i):
        b2, h2, i2 = nxt(b, h, i)
        return b2, h2 * Q_PER_KV, i2 * (tq // ATTN_ROWS), 0

    def kvn_map(b, h, i):
        b2, h2, _ = nxt(b, h, i)
        return b2, 0, h2

    kv_c = pl.BlockSpec((1, s, HEAD_DIM), lambda b, h, i: (b, 0, h))
    kv_x = pl.BlockSpec((1, t, HEAD_DIM), lambda b, h, i: (b, 0, h))
    return pl.pallas_call(
        _attn_kernel,
        grid=(bsz, N_KV_HEADS, nt),
        in_specs=[pl.BlockSpec((1, Q_PER_KV, tq, HEAD_DIM), lambda b, h, i: (b, h, i, 0)),
                  kv_c, kv_c, kv_x, kv_x,
                  pl.BlockSpec((1, 1, ATTN_ROWS, HEAD_DIM), qn_map),
                  pl.BlockSpec((1, s, HEAD_DIM), kvn_map), pl.BlockSpec((1, t, HEAD_DIM), kvn_map)],
        out_specs=pl.BlockSpec((1, tq, Q_PER_KV * HEAD_DIM), lambda b, h, i: (b, i, h)),
        out_shape=jax.ShapeDtypeStruct((bsz, t, A_WIDTH), BF16),
        scratch_shapes=[pltpu.VMEM((ATTN_ROWS, s + t), F32)],
        compiler_params=_params("arbitrary", "arbitrary", "arbitrary"),
        name="attention",
    )(q, kc, vc, k, v, q, kc, k)


def _deepnorm(x, gt, y, pg, pb):
    return _ln(ALPHA * x + gt * y) * pg + pb


def _even_out_kernel(a_ref, u_ref, vn_ref, sg_ref, x_ref, gt_ref, ws_ref, bs_ref, wo_ref, pg_ref, pb_ref,
                     o_ref, comb_ref):
    for r0 in range(0, a_ref.shape[1], OUT_CHAIN):
        rc = slice(r0, r0 + OUT_CHAIN)
        comb_ref[rc, :A_WIDTH] = a_ref[0, rc] * sg_ref[0, rc, :A_WIDTH]
        for n in range(OUT_CHAIN // CHUNK):
            rows = slice(r0 + n * CHUNK, r0 + (n + 1) * CHUNK)
            for g in range(B_GROUPS):
                cols = slice(g * B_GROUP_DIM, (g + 1) * B_GROUP_DIM)
                gcols = slice(A_WIDTH + g * B_GROUP_DIM, A_WIDTH + (g + 1) * B_GROUP_DIM)
                mixed = _dot(ws_ref[g], vn_ref[0, rows, cols]) + bs_ref[g]
                comb_ref[rows, gcols] = (u_ref[0, rows, cols].astype(F32) * mixed
                                         * sg_ref[0, rows, gcols].astype(F32)).astype(BF16)
        y = _dot(comb_ref[rc], wo_ref[...])
        o_ref[0, rc] = _deepnorm(x_ref[0, rc], gt_ref[0], y, pg_ref[...], pb_ref[...])


def _even_out(a, u, vn, sg, x, gt, w_s, b_s, w_out, pg, pb, tm):
    bsz, t, _ = x.shape
    row = lambda b, i: (b, i, 0)
    return pl.pallas_call(
        _even_out_kernel,
        grid=(bsz, t // tm),
        in_specs=[
            pl.BlockSpec((1, tm, A_WIDTH), row), pl.BlockSpec((1, tm, B_WIDTH), row),
            pl.BlockSpec((1, tm, B_WIDTH), row), pl.BlockSpec((1, tm, D_INNER), row),
            pl.BlockSpec((1, tm, D_MODEL), row),
            pl.BlockSpec((1, 1, D_MODEL), lambda b, i: (b, 0, 0)),
            _const_spec((B_GROUPS, CHUNK, CHUNK)), _const_spec((B_GROUPS, CHUNK, B_GROUP_DIM)),
            _const_spec((D_INNER, D_MODEL)),
            _const_spec((1, D_MODEL)), _const_spec((1, D_MODEL)),
        ],
        out_specs=pl.BlockSpec((1, tm, D_MODEL), row),
        out_shape=jax.ShapeDtypeStruct((bsz, t, D_MODEL), F32),
        scratch_shapes=[pltpu.VMEM((tm, D_INNER), BF16)],
        compiler_params=_params("arbitrary", "arbitrary"),
        name="even_out",
    )(a, u, vn, sg, x, gt, w_s, b_s, w_out, pg, pb)


def _proj_odd_kernel(x_ref, sh_ref, sc_ref, w_ref, cs_ref, a_ref, b_ref, m_ref):
    tm = x_ref.shape[1]
    n = tm // FFT_N1
    m32 = _ln(x_ref[0]) * (1.0 + sc_ref[0]) + sh_ref[0]
    n_slab = D_MODEL // 128
    for j in range(n_slab):
        m_ref[j] = m32[:, j * 128:(j + 1) * 128]
    mp = jnp.concatenate(
        [jnp.concatenate([m_ref[j, pl.ds(t1, n, stride=FFT_N1), :] for j in range(n_slab)], axis=1)
         for t1 in range(FFT_N1)], axis=0).astype(BF16)
    cw = 4 * C_GROUP_DIM
    pw = 2 * C_GROUP_DIM
    for c in range(D_INNER // cw):
        z = _dot(mp, w_ref[:, c * cw:(c + 1) * cw]).astype(BF16)
        for j in range(cw // pw):
            pair = c * (cw // pw) + j
            cols = slice(pair * 2 * FFT_SLOTS, (pair + 1) * 2 * FFT_SLOTS)
            ab = _dot(z[:, j * pw:(j + 1) * pw], cs_ref[...])
            for t1 in range(FFT_N1):
                a_ref[0, t1, :, cols] = ab[t1 * n:(t1 + 1) * n, :2 * FFT_SLOTS].astype(BF16)
                b_ref[0, t1, :, cols] = ab[t1 * n:(t1 + 1) * n, 2 * FFT_SLOTS:].astype(BF16)


def _proj_odd(x, sh, sc, w_hb, cs_pair, tm):
    bsz, t, _ = x.shape
    row = lambda b, i: (b, i, 0)
    per_b = pl.BlockSpec((1, 1, D_MODEL), lambda b, i: (b, 0, 0))
    n = tm // FFT_N1
    width = C_GROUPS * FFT_SLOTS
    perm_spec = pl.BlockSpec((1, FFT_N1, n, width), lambda b, i: (b, 0, i, 0))
    perm = jax.ShapeDtypeStruct((bsz, FFT_N1, t // FFT_N1, width), BF16)
    return pl.pallas_call(
        _proj_odd_kernel,
        grid=(bsz, t // tm),
        in_specs=[pl.BlockSpec((1, tm, D_MODEL), row), per_b, per_b,
                  _const_spec((D_MODEL, D_INNER)), _const_spec((2 * C_GROUP_DIM, 2 * C_GROUP_DIM))],
        out_specs=[perm_spec, perm_spec],
        out_shape=[perm, perm],
        scratch_shapes=[pltpu.VMEM((D_MODEL // 128, tm, 128), F32)],
        compiler_params=_params("arbitrary", "arbitrary"),
        name="proj_odd",
    )(x, sh, sc, w_hb, cs_pair)


def _dft4(ar, ai):
    s0r, s0i = ar[0] + ar[2], ai[0] + ai[2]
    s1r, s1i = ar[0] - ar[2], ai[0] - ai[2]
    s2r, s2i = ar[1] + ar[3], ai[1] + ai[3]
    s3r, s3i = ar[1] - ar[3], ai[1] - ai[3]
    return ([s0r + s2r, s1r - s3i, s0r - s2r, s1r + s3i],
            [s0i + s2i, s1i + s3r, s0i - s2i, s1i - s3r])


def _dft8(zr, zi):
    er, ei = _dft4(zr[0::2], zi[0::2])
    orr, oi = _dft4(zr[1::2], zi[1::2])
    h = np.float32(np.sqrt(0.5))
    tr = [orr[0], (orr[1] - oi[1]) * h, -oi[2], (-orr[3] - oi[3]) * h]
    ti = [oi[0], (orr[1] + oi[1]) * h, orr[2], (orr[3] - oi[3]) * h]
    xr = [er[k] + tr[k] for k in range(4)] + [er[k] - tr[k] for k in range(4)]
    xi = [ei[k] + ti[k] for k in range(4)] + [ei[k] - ti[k] for k in range(4)]
    return xr, xi


ROW_CHAIN = 256
OUT_CHAIN = 512
ATTN_ROWS = 512
FFT_COLS = 256
FFT_ROWS = 16


def _fft_t_kernel(a_ref, b_ref, cs_ref, fp_ref, fm_ref, g_ref):
    half = FFT_N2
    lane = lax.broadcasted_iota(jnp.int32, (8, 128), 1)
    packed = (lane % FFT_SLOTS) == 0
    for ch in range(a_ref.shape[3] // FFT_COLS):
        c0 = ch * FFT_COLS
        for t1 in range(FFT_N1):
            g_ref[ch, t1, 0] = _dot(cs_ref[t1], a_ref[0, t1, :, c0:c0 + FFT_COLS])
            g_ref[ch, t1, 1] = _dot(cs_ref[t1], b_ref[0, t1, :, c0:c0 + FFT_COLS])
        for r in range(0, FFT_N2, FFT_ROWS):
            for j in range(FFT_COLS // 128):
                lanes = slice(j * 128, (j + 1) * 128)
                plus, minus = [], []
                for r8 in range(r, r + FFT_ROWS, 8):
                    rows = slice(r8, r8 + 8)
                    rows_s = slice(half + r8, half + r8 + 8)
                    pr, pi, mr, mi = [], [], [], []
                    for t1 in range(FFT_N1):
                        p1 = g_ref[ch, t1, 0, rows, lanes]
                        p3 = g_ref[ch, t1, 0, rows_s, lanes]
                        p4 = g_ref[ch, t1, 1, rows, lanes]
                        p2 = g_ref[ch, t1, 1, rows_s, lanes]
                        pr.append(jnp.where(packed, p1, p1 - p2))
                        pi.append(jnp.where(packed, p3, p3 + p4))
                        mr.append(jnp.where(packed, p4, p1 + p2))
                        mi.append(jnp.where(packed, p2, p3 - p4))
                    plus.append(_dft8(pr, pi)[0])
                    minus.append(_dft8(mr, mi)[0])
                for k1 in range(FFT_N1):
                    orow = slice(k1 * FFT_N2 + r, k1 * FFT_N2 + r + FFT_ROWS)
                    ocol = slice(c0 + j * 128, c0 + (j + 1) * 128)
                    fp_ref[0, orow, ocol] = jnp.concatenate([o[k1] for o in plus], axis=0).astype(BF16)
                    fm_ref[0, orow, ocol] = jnp.concatenate([o[k1] for o in minus], axis=0).astype(BF16)


def _fft_t(a, b, cs_t, tn):
    bsz, _, n2, w = a.shape
    t = FFT_N1 * n2
    blk = pl.BlockSpec((1, FFT_N1, n2, tn), lambda bb, j: (bb, 0, 0, j))
    out = pl.BlockSpec((1, t, tn), lambda bb, j: (bb, 0, j))
    return pl.pallas_call(
        _fft_t_kernel,
        grid=(bsz, w // tn),
        in_specs=[blk, blk, _const_spec((FFT_N1, 2 * FFT_N2, FFT_N2))],
        out_specs=[out, out],
        out_shape=[jax.ShapeDtypeStruct((bsz, t, w), BF16)] * 2,
        scratch_shapes=[pltpu.VMEM((tn // FFT_COLS, FFT_N1, 2, 2 * FFT_N2, FFT_COLS), F32)],
        compiler_params=_params("arbitrary", "arbitrary"),
        name="fft_t",
    )(a, b, cs_t)


def _odd_out_kernel(fp_ref, fm_ref, x_ref, sh_ref, sc_ref, gt_ref, wg_ref, wo_ref, pg_ref, pb_ref, o_ref):
    hw = fp_ref.shape[2]
    for r0 in range(0, x_ref.shape[1], OUT_CHAIN):
        rows = slice(r0, r0 + OUT_CHAIN)
        x = x_ref[0, rows]
        m = (_ln(x) * (1.0 + sc_ref[0]) + sh_ref[0]).astype(BF16)
        gate = jax.nn.silu(_dot(m, wg_ref[...]))
        y = (_dot((fp_ref[0, rows].astype(F32) * gate[:, :hw]).astype(BF16), wo_ref[:hw])
             + _dot((fm_ref[0, rows].astype(F32) * gate[:, hw:]).astype(BF16), wo_ref[hw:]))
        o_ref[0, rows] = _deepnorm(x, gt_ref[0], y, pg_ref[...], pb_ref[...])


def _odd_out(fp, fm, x, sh, sc, gt, w_g, w_out, pg, pb, tm):
    bsz, t, _ = x.shape
    row = lambda b, i: (b, i, 0)
    per_b = pl.BlockSpec((1, 1, D_MODEL), lambda b, i: (b, 0, 0))
    hw = fp.shape[2]
    return pl.pallas_call(
        _odd_out_kernel,
        grid=(bsz, t // tm),
        in_specs=[
            pl.BlockSpec((1, tm, hw), row), pl.BlockSpec((1, tm, hw), row), pl.BlockSpec((1, tm, D_MODEL), row),
            per_b, per_b, per_b,
            _const_spec((D_MODEL, D_INNER)), _const_spec((D_INNER, D_MODEL)),
            _const_spec((1, D_MODEL)), _const_spec((1, D_MODEL)),
        ],
        out_specs=pl.BlockSpec((1, tm, D_MODEL), row),
        out_shape=jax.ShapeDtypeStruct((bsz, t, D_MODEL), F32),
        compiler_params=_params("arbitrary", "arbitrary"),
        name="odd_out",
    )(fp, fm, x, sh, sc, gt, w_g, w_out, pg, pb)


def _slot_order(w, axis):
    w = jnp.moveaxis(w, axis, -1)
    g = w.reshape(w.shape[:-1] + (C_GROUPS, C_GROUP_DIM))
    plus = g[..., :FFT_SLOTS]
    minus = jnp.concatenate([g[..., FFT_SLOTS:FFT_SLOTS + 1], g[..., :FFT_SLOTS:-1]], axis=-1)
    flat = lambda v: v.reshape(v.shape[:-2] + (C_GROUPS * FFT_SLOTS,))
    return jnp.moveaxis(jnp.concatenate([flat(plus), flat(minus)], axis=-1), -1, axis)


def _rope_tables(n_tokens):
    rows = n_tokens // GRID_W
    r, cl = jnp.meshgrid(jnp.arange(rows), jnp.arange(GRID_W), indexing="ij")
    row = r.reshape(-1).astype(F32)
    col = cl.reshape(-1).astype(F32)
    n_pairs_axis = HEAD_DIM // 4
    inv_freq = ROPE_THETA ** (-jnp.arange(n_pairs_axis, dtype=F32) / n_pairs_axis)
    ang = jnp.concatenate([row[:, None] * inv_freq, col[:, None] * inv_freq], axis=-1)
    cs, sn = jnp.cos(ang), jnp.sin(ang)
    return jnp.concatenate([cs, cs], axis=-1), jnp.concatenate([-sn, sn], axis=-1)


def _dft_tables(n_tokens):
    assert n_tokens == FFT_N1 * FFT_N2 and C_GROUP_DIM == 2 * FFT_SLOTS
    c = np.arange(C_GROUP_DIM)[:, None]
    sl = np.arange(FFT_SLOTS)[None, :]
    ang = 2.0 * np.pi * c * sl / C_GROUP_DIM
    norm = 1.0 / np.sqrt(float(n_tokens * C_GROUP_DIM))
    re = np.cos(ang) * norm
    im = np.sin(ang) * norm
    im[:, 0] = np.cos(np.pi * c[:, 0]) * norm
    zero = np.zeros_like(re)
    cs_pair = np.block([[re, zero, im, zero], [zero, re, zero, im]])
    k2 = np.arange(FFT_N2)[None, :, None]
    t1 = np.arange(FFT_N1)[:, None, None]
    t2 = np.arange(FFT_N2)[None, None, :]
    ang_t = 2.0 * np.pi * k2 * (t1 + FFT_N1 * t2) / n_tokens
    cs_t = np.concatenate([np.cos(ang_t), np.sin(ang_t)], axis=1)
    f = lambda a: jnp.asarray(np.ascontiguousarray(a), dtype=F32)
    return f(cs_pair).astype(BF16), f(cs_t).astype(BF16)


def kernel(x, c, ctx, c_ctx, w_mod, b_mod, post_ln_g, post_ln_b, even_w_in, even_q_norm, even_k_norm,
           even_v_ln_g, even_v_ln_b, even_w_s, even_b_s, even_w_out, odd_w_in, odd_w_out):
    bsz, t, _ = x.shape
    assert DEPTH == 2 and t % CHUNK == 0
    row1 = lambda v: v.reshape(1, -1)

    n_cond = -(-(bsz + 1) // 8) * 8
    cond = jnp.zeros((n_cond, D_MODEL), F32).at[:bsz].set(c).at[bsz].set(c_ctx)
    mod = _adaln(cond, w_mod, b_mod)
    split = lambda l, rows: [mod[l, rows, i * D_MODEL:(i + 1) * D_MODEL] for i in range(3)]
    sh0, sc0, gt0 = [v[:, None, :] for v in split(0, slice(0, bsz))]
    sh0c, sc0c, _ = split(0, slice(bsz, bsz + 1))
    sh1, sc1, gt1 = [v[:, None, :] for v in split(1, slice(0, bsz))]

    cos2, sin2 = _rope_tables(t)
    cs_pair, cs_t = _dft_tables(t)

    w_in0 = even_w_in[0].astype(BF16)
    q, k, v, u, vn, sg = _proj_even(x, sh0, sc0, w_in0, row1(even_q_norm[0]), row1(even_k_norm[0]),
                                    row1(even_v_ln_g[0]), row1(even_v_ln_b[0]), cos2, sin2, tm=1024)
    kc, vc = _ctx_kv(ctx, sh0c, sc0c, w_in0, row1(even_k_norm[0]))
    a = _attention(q, kc, vc, k, v, tq=1024)
    b_s = jnp.broadcast_to(even_b_s[0][:, :, None], (B_GROUPS, CHUNK, B_GROUP_DIM))
    x1 = _even_out(a, u, vn, sg, x, gt0, even_w_s[0].astype(BF16), b_s, even_w_out[0].astype(BF16),
                   row1(post_ln_g[0]), row1(post_ln_b[0]), tm=1024)

    w_in1 = odd_w_in[0].astype(BF16)
    fa, fb = _proj_odd(x1, sh1, sc1, w_in1, cs_pair, tm=1024)
    fp, fm = _fft_t(fa, fb, cs_t, tn=512)
    return _odd_out(fp, fm, x1, sh1, sc1, gt1, _slot_order(w_in1[:, D_INNER:], 1),
                    _slot_order(odd_w_out[0].astype(BF16), 0), row1(post_ln_g[1]), row1(post_ln_b[1]), tm=1024)
```

```python
import functools

import numpy as np
import jax
import jax.numpy as jnp
from jax import lax
from jax.experimental import pallas as pl
from jax.experimental.pallas import tpu as pltpu

D_MODEL = 1024
DEPTH = 2
GRID_W = 64
D_INNER = 2 * D_MODEL
HEAD_DIM = 128
A_WIDTH = D_INNER // 2
N_Q_HEADS = A_WIDTH // HEAD_DIM
N_KV_HEADS = 2
Q_PER_KV = N_Q_HEADS // N_KV_HEADS
KV_WIDTH = N_KV_HEADS * HEAD_DIM
B_WIDTH = D_INNER - A_WIDTH
CHUNK = 128
B_GROUP_DIM = 128
B_GROUPS = B_WIDTH // B_GROUP_DIM
C_GROUP_DIM = 128
C_GROUPS = D_INNER // C_GROUP_DIM
ROPE_THETA = 10000.0
EVEN_IN = A_WIDTH + 2 * KV_WIDTH + 2 * B_WIDTH + D_INNER
ODD_IN = 2 * D_INNER
ALPHA = (2 * DEPTH) ** 0.25
EPS = 1e-6

_Q0, _K0, _V0 = 0, A_WIDTH, A_WIDTH + KV_WIDTH
_U0 = A_WIDTH + 2 * KV_WIDTH
_BV0 = _U0 + B_WIDTH
_G0 = _BV0 + B_WIDTH

FFT_N1 = 8
FFT_N2 = 256
FFT_SLOTS = 64

V7X_VMEM_LIMIT_BYTES = 60000 * 1024

F32 = jnp.float32
BF16 = jnp.bfloat16


def _dot(a, b):
    return jnp.dot(a, b, preferred_element_type=F32)


def _dot_nt(a, b):
    return lax.dot_general(a, b, (((1,), (1,)), ((), ())), preferred_element_type=F32)


def _ln(x):
    mu = jnp.mean(x, axis=-1, keepdims=True)
    xc = x - mu
    var = jnp.mean(xc * xc, axis=-1, keepdims=True)
    return xc * lax.rsqrt(var + EPS)


def _rms_head(z, g):
    return z * lax.rsqrt(jnp.mean(z * z, axis=-1, keepdims=True) + EPS) * g


def _rope(y, cos2, sin2):
    return y * cos2 + pltpu.roll(y, HEAD_DIM // 2, 1) * sin2


def _params(*sem):
    return pltpu.CompilerParams(dimension_semantics=sem, vmem_limit_bytes=V7X_VMEM_LIMIT_BYTES)


def _const_spec(shape):
    nd = len(shape)
    return pl.BlockSpec(shape, lambda *_: (0,) * nd, pipeline_mode=pl.Buffered(1))


def _adaln_kernel(c_ref, w_ref, b_ref, o_ref):
    h = jax.nn.silu(c_ref[...])
    w = w_ref[0]
    h_hi = h.astype(BF16)
    h_lo = (h - h_hi.astype(F32)).astype(BF16)
    w_hi = w.astype(BF16)
    w_lo = (w - w_hi.astype(F32)).astype(BF16)
    o_ref[0] = _dot(h_hi, w_hi) + _dot(h_hi, w_lo) + _dot(h_lo, w_hi) + b_ref[0]


def _adaln(cond, w_mod, b_mod):
    r = cond.shape[0]
    tn = 3 * D_MODEL
    return pl.pallas_call(
        _adaln_kernel,
        grid=(DEPTH, 3 * D_MODEL // tn),
        in_specs=[
            pl.BlockSpec((r, D_MODEL), lambda l, j: (0, 0)),
            pl.BlockSpec((1, D_MODEL, tn), lambda l, j: (l, 0, j)),
            pl.BlockSpec((1, 1, tn), lambda l, j: (l, 0, j)),
        ],
        out_specs=pl.BlockSpec((1, r, tn), lambda l, j: (l, 0, j)),
        out_shape=jax.ShapeDtypeStruct((DEPTH, r, 3 * D_MODEL), F32),
        compiler_params=_params("arbitrary", "arbitrary"),
        name="adaln",
    )(cond, w_mod, b_mod.reshape(DEPTH, 1, 3 * D_MODEL))


def _proj_even_kernel(x_ref, sh_ref, sc_ref, w_ref, qg_ref, kg_ref, vg_ref, vb_ref, cos_ref, sin_ref,
                      q_ref, k_ref, v_ref, u_ref, vn_ref, sg_ref):
    qg = qg_ref[...] * (HEAD_DIM ** -0.5)
    kg = kg_ref[...]
    cw = 4 * HEAD_DIM
    for r0 in range(0, x_ref.shape[1], ROW_CHAIN):
        rows = slice(r0, r0 + ROW_CHAIN)
        m = (_ln(x_ref[0, rows]) * (1.0 + sc_ref[0]) + sh_ref[0]).astype(BF16)
        cos2 = cos_ref[rows]
        sin2 = sin_ref[rows]
        for c in range(A_WIDTH // cw):
            z = _dot(m, w_ref[:, _Q0 + c * cw:_Q0 + (c + 1) * cw])
            for j in range(cw // HEAD_DIM):
                zh = z[:, j * HEAD_DIM:(j + 1) * HEAD_DIM]
                q_ref[0, c * (cw // HEAD_DIM) + j, rows] = _rope(_rms_head(zh, qg), cos2, sin2).astype(BF16)
        z = _dot(m, w_ref[:, _K0:_K0 + 2 * KV_WIDTH])
        for j in range(N_KV_HEADS):
            zh = z[:, j * HEAD_DIM:(j + 1) * HEAD_DIM]
            k_ref[0, rows, j * HEAD_DIM:(j + 1) * HEAD_DIM] = _rope(_rms_head(zh, kg), cos2, sin2).astype(BF16)
        v_ref[0, rows] = z[:, KV_WIDTH:].astype(BF16)
        for c in range(B_WIDTH // cw):
            z = _dot(m, w_ref[:, _U0 + c * cw:_U0 + (c + 1) * cw])
            u_ref[0, rows, c * cw:(c + 1) * cw] = jax.nn.gelu(z).astype(BF16)
        gv = jax.nn.gelu(_dot(m, w_ref[:, _BV0:_BV0 + B_WIDTH]))
        vn_ref[0, rows] = (_ln(gv) * vg_ref[...] + vb_ref[...]).astype(BF16)
        for c in range(D_INNER // cw):
            z = _dot(m, w_ref[:, _G0 + c * cw:_G0 + (c + 1) * cw])
            sg_ref[0, rows, c * cw:(c + 1) * cw] = jax.nn.silu(z).astype(BF16)


def _proj_even(x, sh, sc, w_in, q_g, k_g, v_g, v_b, cos2, sin2, tm):
    bsz, t, _ = x.shape
    row = lambda b, i: (b, i, 0)
    per_b = pl.BlockSpec((1, 1, D_MODEL), lambda b, i: (b, 0, 0))
    bf = lambda *s: jax.ShapeDtypeStruct(s, BF16)
    return pl.pallas_call(
        _proj_even_kernel,
        grid=(bsz, t // tm),
        in_specs=[
            pl.BlockSpec((1, tm, D_MODEL), row), per_b, per_b,
            _const_spec((D_MODEL, EVEN_IN)),
            _const_spec((1, HEAD_DIM)), _const_spec((1, HEAD_DIM)),
            _const_spec((1, B_WIDTH)), _const_spec((1, B_WIDTH)),
            pl.BlockSpec((tm, HEAD_DIM), lambda b, i: (i, 0)),
            pl.BlockSpec((tm, HEAD_DIM), lambda b, i: (i, 0)),
        ],
        out_specs=[
            pl.BlockSpec((1, N_Q_HEADS, tm, HEAD_DIM), lambda b, i: (b, 0, i, 0)),
            pl.BlockSpec((1, tm, KV_WIDTH), row), pl.BlockSpec((1, tm, KV_WIDTH), row),
            pl.BlockSpec((1, tm, B_WIDTH), row), pl.BlockSpec((1, tm, B_WIDTH), row),
            pl.BlockSpec((1, tm, D_INNER), row),
        ],
        out_shape=[bf(bsz, N_Q_HEADS, t, HEAD_DIM), bf(bsz, t, KV_WIDTH), bf(bsz, t, KV_WIDTH),
                   bf(bsz, t, B_WIDTH), bf(bsz, t, B_WIDTH), bf(bsz, t, D_INNER)],
        compiler_params=_params("arbitrary", "arbitrary"),
        name="proj_even",
    )(x, sh, sc, w_in, q_g, k_g, v_g, v_b, cos2, sin2)


def _ctx_kv_kernel(c_ref, sh_ref, sc_ref, w_ref, kg_ref, kc_ref, vc_ref):
    nb, s, _ = c_ref.shape
    c = c_ref[...].reshape(nb * s, D_MODEL)
    m = (_ln(c) * (1.0 + sc_ref[...]) + sh_ref[...]).astype(BF16)
    z = _dot(m, w_ref[...])
    for j in range(N_KV_HEADS):
        zh = z[:, j * HEAD_DIM:(j + 1) * HEAD_DIM]
        kc_ref[:, :, j * HEAD_DIM:(j + 1) * HEAD_DIM] = (
            _rms_head(zh, kg_ref[...]).astype(BF16).reshape(nb, s, HEAD_DIM))
    vc_ref[...] = z[:, KV_WIDTH:].astype(BF16).reshape(nb, s, KV_WIDTH)


def _ctx_kv(ctx, sh_c, sc_c, w_in, k_g, nb):
    bsz, s, _ = ctx.shape
    assert _K0 % (2 * KV_WIDTH) == 0 and bsz % nb == 0
    return pl.pallas_call(
        _ctx_kv_kernel,
        grid=(bsz // nb,),
        in_specs=[
            pl.BlockSpec((nb, s, D_MODEL), lambda b: (b, 0, 0)),
            pl.BlockSpec((1, D_MODEL), lambda b: (0, 0)), pl.BlockSpec((1, D_MODEL), lambda b: (0, 0)),
            pl.BlockSpec((D_MODEL, 2 * KV_WIDTH), lambda b: (0, _K0 // (2 * KV_WIDTH))),
            pl.BlockSpec((1, HEAD_DIM), lambda b: (0, 0)),
        ],
        out_specs=[pl.BlockSpec((nb, s, KV_WIDTH), lambda b: (b, 0, 0))] * 2,
        out_shape=[jax.ShapeDtypeStruct((bsz, s, KV_WIDTH), BF16)] * 2,
        compiler_params=_params("arbitrary"),
        name="ctx_kv",
    )(ctx, sh_c, sc_c, w_in, k_g)


def _attn_scores(q, kc_ref, k_ref):
    return _dot_nt(q, kc_ref[0]), _dot_nt(q, k_ref[0])


def _attn_kernel(q_ref, kc_ref, vc_ref, k_ref, v_ref, qn_ref, kcn_ref, kn_ref, a_ref, s0_ref):
    nc = kc_ref.shape[1]
    first = (pl.program_id(0) == 0) & (pl.program_id(1) == 0) & (pl.program_id(2) == 0)

    @pl.when(first)
    def _():
        s_c, s_x = _attn_scores(q_ref[0, 0, :ATTN_ROWS], kc_ref, k_ref)
        s0_ref[:, :nc] = s_c
        s0_ref[:, nc:] = s_x

    for r0 in range(0, q_ref.shape[2], ATTN_ROWS):
        rows = slice(r0, r0 + ATTN_ROWS)
        for j in range(q_ref.shape[1]):
            if r0 == 0 and j == 0:
                s_c, s_x = s0_ref[:, :nc], s0_ref[:, nc:]
            else:
                s_c, s_x = _attn_scores(q_ref[0, j, rows], kc_ref, k_ref)
            mx = jnp.maximum(jnp.max(s_c, axis=-1, keepdims=True), jnp.max(s_x, axis=-1, keepdims=True))
            p_c = jnp.exp(s_c - mx)
            p_x = jnp.exp(s_x - mx)
            den = jnp.sum(p_c, axis=-1, keepdims=True) + jnp.sum(p_x, axis=-1, keepdims=True)
            o = (_dot(p_c.astype(BF16), vc_ref[0]) + _dot(p_x.astype(BF16), v_ref[0])) / den
            a_ref[0, rows, j * HEAD_DIM:(j + 1) * HEAD_DIM] = o.astype(BF16)
    s_c, s_x = _attn_scores(qn_ref[0, 0], kcn_ref, kn_ref)
    s0_ref[:, :nc] = s_c
    s0_ref[:, nc:] = s_x


def _attention(q, kc, vc, k, v, tq):
    bsz, _, t, _ = q.shape
    s = kc.shape[1]
    nt = t // tq
    last = bsz * N_KV_HEADS * nt - 1

    def nxt(b, h, i):
        n = jnp.minimum((b * N_KV_HEADS + h) * nt + i + 1, last)
        return n // (N_KV_HEADS * nt), (n // nt) % N_KV_HEADS, n % nt

    def qn_map(b, h, i):
        b2, h2, i2 = nxt(b, h, i)
        return b2, h2 * Q_PER_KV, i2 * (tq // ATTN_ROWS), 0

    def kvn_map(b, h, i):
        b2, h2, _ = nxt(b, h, i)
        return b2, 0, h2

    kv_c = pl.BlockSpec((1, s, HEAD_DIM), lambda b, h, i: (b, 0, h))
    kv_x = pl.BlockSpec((1, t, HEAD_DIM), lambda b, h, i: (b, 0, h))
    return pl.pallas_call(
        _attn_kernel,
        grid=(bsz, N_KV_HEADS, nt),
        in_specs=[pl.BlockSpec((1, Q_PER_KV, tq, HEAD_DIM), lambda b, h, i: (b, h, i, 0)),
                  kv_c, kv_c, kv_x, kv_x,
                  pl.BlockSpec((1, 1, ATTN_ROWS, HEAD_DIM), qn_map),
                  pl.BlockSpec((1, s, HEAD_DIM), kvn_map), pl.BlockSpec((1, t, HEAD_DIM), kvn_map)],
        out_specs=pl.BlockSpec((1, tq, Q_PER_KV * HEAD_DIM), lambda b, h, i: (b, i, h)),
        out_shape=jax.ShapeDtypeStruct((bsz, t, A_WIDTH), BF16),
        scratch_shapes=[pltpu.VMEM((ATTN_ROWS, s + t), F32)],
        compiler_params=_params("arbitrary", "arbitrary", "arbitrary"),
        name="attention",
    )(q, kc, vc, k, v, q, kc, k)


def _deepnorm(x, gt, y, pg, pb):
    return _ln(ALPHA * x + gt * y) * pg + pb


def _even_out_kernel(a_ref, u_ref, vn_ref, sg_ref, x_ref, gt_ref, ws_ref, bs_ref, wo_ref, pg_ref, pb_ref,
                     o_ref, comb_ref):
    for r0 in range(0, a_ref.shape[1], OUT_CHAIN):
        rc = slice(r0, r0 + OUT_CHAIN)
        comb_ref[rc, :A_WIDTH] = a_ref[0, rc] * sg_ref[0, rc, :A_WIDTH]
        for n in range(OUT_CHAIN // CHUNK):
            rows = slice(r0 + n * CHUNK, r0 + (n + 1) * CHUNK)
            for g in range(B_GROUPS):
                cols = slice(g * B_GROUP_DIM, (g + 1) * B_GROUP_DIM)
                gcols = slice(A_WIDTH + g * B_GROUP_DIM, A_WIDTH + (g + 1) * B_GROUP_DIM)
                mixed = _dot(ws_ref[g], vn_ref[0, rows, cols]) + bs_ref[g]
                comb_ref[rows, gcols] = (u_ref[0, rows, cols].astype(F32) * mixed
                                         * sg_ref[0, rows, gcols].astype(F32)).astype(BF16)
        y = _dot(comb_ref[rc], wo_ref[...])
        o_ref[0, rc] = _deepnorm(x_ref[0, rc], gt_ref[0], y, pg_ref[...], pb_ref[...])


def _even_out(a, u, vn, sg, x, gt, w_s, b_s, w_out, pg, pb, tm):
    bsz, t, _ = x.shape
    row = lambda b, i: (b, i, 0)
    return pl.pallas_call(
        _even_out_kernel,
        grid=(bsz, t // tm),
        in_specs=[
            pl.BlockSpec((1, tm, A_WIDTH), row), pl.BlockSpec((1, tm, B_WIDTH), row),
            pl.BlockSpec((1, tm, B_WIDTH), row), pl.BlockSpec((1, tm, D_INNER), row),
            pl.BlockSpec((1, tm, D_MODEL), row),
            pl.BlockSpec((1, 1, D_MODEL), lambda b, i: (b, 0, 0)),
            _const_spec((B_GROUPS, CHUNK, CHUNK)), _const_spec((B_GROUPS, CHUNK, B_GROUP_DIM)),
            _const_spec((D_INNER, D_MODEL)),
            _const_spec((1, D_MODEL)), _const_spec((1, D_MODEL)),
        ],
        out_specs=pl.BlockSpec((1, tm, D_MODEL), row),
        out_shape=jax.ShapeDtypeStruct((bsz, t, D_MODEL), F32),
        scratch_shapes=[pltpu.VMEM((tm, D_INNER), BF16)],
        compiler_params=_params("arbitrary", "arbitrary"),
        name="even_out",
    )(a, u, vn, sg, x, gt, w_s, b_s, w_out, pg, pb)


def _proj_odd_kernel(x_ref, sh_ref, sc_ref, w_ref, cs_ref, a_ref, b_ref, m_ref):
    tm = x_ref.shape[1]
    n = tm // FFT_N1
    m32 = _ln(x_ref[0]) * (1.0 + sc_ref[0]) + sh_ref[0]
    n_slab = D_MODEL // 128
    for j in range(n_slab):
        m_ref[j] = m32[:, j * 128:(j + 1) * 128]
    mp = jnp.concatenate(
        [jnp.concatenate([m_ref[j, pl.ds(t1, n, stride=FFT_N1), :] for j in range(n_slab)], axis=1)
         for t1 in range(FFT_N1)], axis=0).astype(BF16)
    cw = 4 * C_GROUP_DIM
    pw = 2 * C_GROUP_DIM
    for c in range(D_INNER // cw):
        z = _dot(mp, w_ref[:, c * cw:(c + 1) * cw]).astype(BF16)
        for j in range(cw // pw):
            pair = c * (cw // pw) + j
            cols = slice(pair * 2 * FFT_SLOTS, (pair + 1) * 2 * FFT_SLOTS)
            ab = _dot(z[:, j * pw:(j + 1) * pw], cs_ref[...])
            for t1 in range(FFT_N1):
                a_ref[0, t1, :, cols] = ab[t1 * n:(t1 + 1) * n, :2 * FFT_SLOTS].astype(BF16)
                b_ref[0, t1, :, cols] = ab[t1 * n:(t1 + 1) * n, 2 * FFT_SLOTS:].astype(BF16)


def _proj_odd(x, sh, sc, w_hb, cs_pair, tm):
    bsz, t, _ = x.shape
    row = lambda b, i: (b, i, 0)
    per_b = pl.BlockSpec((1, 1, D_MODEL), lambda b, i: (b, 0, 0))
    n = tm // FFT_N1
    width = C_GROUPS * FFT_SLOTS
    perm_spec = pl.BlockSpec((1, FFT_N1, n, width), lambda b, i: (b, 0, i, 0))
    perm = jax.ShapeDtypeStruct((bsz, FFT_N1, t // FFT_N1, width), BF16)
    return pl.pallas_call(
        _proj_odd_kernel,
        grid=(bsz, t // tm),
        in_specs=[pl.BlockSpec((1, tm, D_MODEL), row), per_b, per_b,
                  _const_spec((D_MODEL, D_INNER)), _const_spec((2 * C_GROUP_DIM, 2 * C_GROUP_DIM))],
        out_specs=[perm_spec, perm_spec],
        out_shape=[perm, perm],
        scratch_shapes=[pltpu.VMEM((D_MODEL // 128, tm, 128), F32)],
        compiler_params=_params("arbitrary", "arbitrary"),
        name="proj_odd",
    )(x, sh, sc, w_hb, cs_pair)


def _dft4(ar, ai):
    s0r, s0i = ar[0] + ar[2], ai[0] + ai[2]
    s1r, s1i = ar[0] - ar[2], ai[0] - ai[2]
    s2r, s2i = ar[1] + ar[3], ai[1] + ai[3]
    s3r, s3i = ar[1] - ar[3], ai[1] - ai[3]
    return ([s0r + s2r, s1r - s3i, s0r - s2r, s1r + s3i],
            [s0i + s2i, s1i + s3r, s0i - s2i, s1i - s3r])


def _dft8(zr, zi):
    er, ei = _dft4(zr[0::2], zi[0::2])
    orr, oi = _dft4(zr[1::2], zi[1::2])
    h = np.float32(np.sqrt(0.5))
    tr = [orr[0], (orr[1] - oi[1]) * h, -oi[2], (-orr[3] - oi[3]) * h]
    ti = [oi[0], (orr[1] + oi[1]) * h, orr[2], (orr[3] - oi[3]) * h]
    xr = [er[k] + tr[k] for k in range(4)] + [er[k] - tr[k] for k in range(4)]
    xi = [ei[k] + ti[k] for k in range(4)] + [ei[k] - ti[k] for k in range(4)]
    return xr, xi


ROW_CHAIN = 256
OUT_CHAIN = 512
ATTN_ROWS = 512
FFT_COLS = 256
FFT_ROWS = 16


def _fft_t_kernel(a_ref, b_ref, cs_ref, fp_ref, fm_ref, g_ref):
    half = FFT_N2
    lane = lax.broadcasted_iota(jnp.int32, (8, 128), 1)
    packed = (lane % FFT_SLOTS) == 0
    for ch in range(a_ref.shape[3] // FFT_COLS):
        c0 = ch * FFT_COLS
        for t1 in range(FFT_N1):
            g_ref[ch, t1, 0] = _dot(cs_ref[t1], a_ref[0, t1, :, c0:c0 + FFT_COLS])
            g_ref[ch, t1, 1] = _dot(cs_ref[t1], b_ref[0, t1, :, c0:c0 + FFT_COLS])
        for r in range(0, FFT_N2, FFT_ROWS):
            for j in range(FFT_COLS // 128):
                lanes = slice(j * 128, (j + 1) * 128)
                plus, minus = [], []
                for r8 in range(r, r + FFT_ROWS, 8):
                    rows = slice(r8, r8 + 8)
                    rows_s = slice(half + r8, half + r8 + 8)
                    pr, pi, mr, mi = [], [], [], []
                    for t1 in range(FFT_N1):
                        p1 = g_ref[ch, t1, 0, rows, lanes]
                        p3 = g_ref[ch, t1, 0, rows_s, lanes]
                        p4 = g_ref[ch, t1, 1, rows, lanes]
                        p2 = g_ref[ch, t1, 1, rows_s, lanes]
                        pr.append(jnp.where(packed, p1, p1 - p2))
                        pi.append(jnp.where(packed, p3, p3 + p4))
                        mr.append(jnp.where(packed, p4, p1 + p2))
                        mi.append(jnp.where(packed, p2, p3 - p4))
                    plus.append(_dft8(pr, pi)[0])
                    minus.append(_dft8(mr, mi)[0])
                for k1 in range(FFT_N1):
                    orow = slice(k1 * FFT_N2 + r, k1 * FFT_N2 + r + FFT_ROWS)
                    ocol = slice(c0 + j * 128, c0 + (j + 1) * 128)
                    fp_ref[0, orow, ocol] = jnp.concatenate([o[k1] for o in plus], axis=0).astype(BF16)
                    fm_ref[0, orow, ocol] = jnp.concatenate([o[k1] for o in minus], axis=0).astype(BF16)


def _fft_t(a, b, cs_t, tn):
    bsz, _, n2, w = a.shape
    t = FFT_N1 * n2
    blk = pl.BlockSpec((1, FFT_N1, n2, tn), lambda bb, j: (bb, 0, 0, j))
    out = pl.BlockSpec((1, t, tn), lambda bb, j: (bb, 0, j))
    return pl.pallas_call(
        _fft_t_kernel,
        grid=(bsz, w // tn),
        in_specs=[blk, blk, _const_spec((FFT_N1, 2 * FFT_N2, FFT_N2))],
        out_specs=[out, out],
        out_shape=[jax.ShapeDtypeStruct((bsz, t, w), BF16)] * 2,
        scratch_shapes=[pltpu.VMEM((tn // FFT_COLS, FFT_N1, 2, 2 * FFT_N2, FFT_COLS), F32)],
        compiler_params=_params("arbitrary", "arbitrary"),
        name="fft_t",
    )(a, b, cs_t)


def _odd_out_kernel(fp_ref, fm_ref, x_ref, sh_ref, sc_ref, gt_ref, wg_ref, wo_ref, pg_ref, pb_ref, o_ref):
    hw = fp_ref.shape[2]
    for r0 in range(0, x_ref.shape[1], OUT_CHAIN):
        rows = slice(r0, r0 + OUT_CHAIN)
        x = x_ref[0, rows]
        m = (_ln(x) * (1.0 + sc_ref[0]) + sh_ref[0]).astype(BF16)
        gate = jax.nn.silu(_dot(m, wg_ref[...]))
        y = (_dot((fp_ref[0, rows].astype(F32) * gate[:, :hw]).astype(BF16), wo_ref[:hw])
             + _dot((fm_ref[0, rows].astype(F32) * gate[:, hw:]).astype(BF16), wo_ref[hw:]))
        o_ref[0, rows] = _deepnorm(x, gt_ref[0], y, pg_ref[...], pb_ref[...])


def _odd_out(fp, fm, x, sh, sc, gt, w_g, w_out, pg, pb, tm):
    bsz, t, _ = x.shape
    row = lambda b, i: (b, i, 0)
    per_b = pl.BlockSpec((1, 1, D_MODEL), lambda b, i: (b, 0, 0))
    hw = fp.shape[2]
    return pl.pallas_call(
        _odd_out_kernel,
        grid=(bsz, t // tm),
        in_specs=[
            pl.BlockSpec((1, tm, hw), row), pl.BlockSpec((1, tm, hw), row), pl.BlockSpec((1, tm, D_MODEL), row),
            per_b, per_b, per_b,
            _const_spec((D_MODEL, D_INNER)), _const_spec((D_INNER, D_MODEL)),
            _const_spec((1, D_MODEL)), _const_spec((1, D_MODEL)),
        ],
        out_specs=pl.BlockSpec((1, tm, D_MODEL), row),
        out_shape=jax.ShapeDtypeStruct((bsz, t, D_MODEL), F32),
        compiler_params=_params("arbitrary", "arbitrary"),
        name="odd_out",
    )(fp, fm, x, sh, sc, gt, w_g, w_out, pg, pb)


def _slot_order(w, axis):
    w = jnp.moveaxis(w, axis, -1)
    g = w.reshape(w.shape[:-1] + (C_GROUPS, C_GROUP_DIM))
    plus = g[..., :FFT_SLOTS]
    minus = jnp.concatenate([g[..., FFT_SLOTS:FFT_SLOTS + 1], g[..., :FFT_SLOTS:-1]], axis=-1)
    flat = lambda v: v.reshape(v.shape[:-2] + (C_GROUPS * FFT_SLOTS,))
    return jnp.moveaxis(jnp.concatenate([flat(plus), flat(minus)], axis=-1), -1, axis)


def _rope_tables(n_tokens):
    rows = n_tokens // GRID_W
    r, cl = jnp.meshgrid(jnp.arange(rows), jnp.arange(GRID_W), indexing="ij")
    row = r.reshape(-1).astype(F32)
    col = cl.reshape(-1).astype(F32)
    n_pairs_axis = HEAD_DIM // 4
    inv_freq = ROPE_THETA ** (-jnp.arange(n_pairs_axis, dtype=F32) / n_pairs_axis)
    ang = jnp.concatenate([row[:, None] * inv_freq, col[:, None] * inv_freq], axis=-1)
    cs, sn = jnp.cos(ang), jnp.sin(ang)
    return jnp.concatenate([cs, cs], axis=-1), jnp.concatenate([-sn, sn], axis=-1)


def _dft_tables(n_tokens):
    assert n_tokens == FFT_N1 * FFT_N2 and C_GROUP_DIM == 2 * FFT_SLOTS
    c = np.arange(C_GROUP_DIM)[:, None]
    sl = np.arange(FFT_SLOTS)[None, :]
    ang = 2.0 * np.pi * c * sl / C_GROUP_DIM
    norm = 1.0 / np.sqrt(float(n_tokens * C_GROUP_DIM))
    re = np.cos(ang) * norm
    im = np.sin(ang) * norm
    im[:, 0] = np.cos(np.pi * c[:, 0]) * norm
    zero = np.zeros_like(re)
    cs_pair = np.block([[re, zero, im, zero], [zero, re, zero, im]])
    k2 = np.arange(FFT_N2)[None, :, None]
    t1 = np.arange(FFT_N1)[:, None, None]
    t2 = np.arange(FFT_N2)[None, None, :]
    ang_t = 2.0 * np.pi * k2 * (t1 + FFT_N1 * t2) / n_tokens
    cs_t = np.concatenate([np.cos(ang_t), np.sin(ang_t)], axis=1)
    f = lambda a: jnp.asarray(np.ascontiguousarray(a), dtype=F32)
    return f(cs_pair).astype(BF16), f(cs_t).astype(BF16)


def kernel(x, c, ctx, c_ctx, w_mod, b_mod, post_ln_g, post_ln_b, even_w_in, even_q_norm, even_k_norm,
           even_v_ln_g, even_v_ln_b, even_w_s, even_b_s, even_w_out, odd_w_in, odd_w_out):
    bsz, t, _ = x.shape
    assert DEPTH == 2 and t % CHUNK == 0
    row1 = lambda v: v.reshape(1, -1)

    n_cond = -(-(bsz + 1) // 8) * 8
    cond = jnp.zeros((n_cond, D_MODEL), F32).at[:bsz].set(c).at[bsz].set(c_ctx)
    mod = _adaln(cond, w_mod, b_mod)
    split = lambda l, rows: [mod[l, rows, i * D_MODEL:(i + 1) * D_MODEL] for i in range(3)]
    sh0, sc0, gt0 = [v[:, None, :] for v in split(0, slice(0, bsz))]
    sh0c, sc0c, _ = split(0, slice(bsz, bsz + 1))
    sh1, sc1, gt1 = [v[:, None, :] for v in split(1, slice(0, bsz))]

    cos2, sin2 = _rope_tables(t)
    cs_pair, cs_t = _dft_tables(t)

    w_in0 = even_w_in[0].astype(BF16)
    q, k, v, u, vn, sg = _proj_even(x, sh0, sc0, w_in0, row1(even_q_norm[0]), row1(even_k_norm[0]),
                                    row1(even_v_ln_g[0]), row1(even_v_ln_b[0]), cos2, sin2, tm=1024)
    kc, vc = _ctx_kv(ctx, sh0c, sc0c, w_in0, row1(even_k_norm[0]), nb=4)
    a = _attention(q, kc, vc, k, v, tq=1024)
    b_s = jnp.broadcast_to(even_b_s[0][:, :, None], (B_GROUPS, CHUNK, B_GROUP_DIM))
    x1 = _even_out(a, u, vn, sg, x, gt0, even_w_s[0].astype(BF16), b_s, even_w_out[0].astype(BF16),
                   row1(post_ln_g[0]), row1(post_ln_b[0]), tm=1024)

    w_in1 = odd_w_in[0].astype(BF16)
    fa, fb = _proj_odd(x1, sh1, sc1, w_in1, cs_pair, tm=1024)
    fp, fm = _fft_t(fa, fb, cs_t, tn=512)
    return _odd_out(fp, fm, x1, sh1, sc1, gt1, _slot_order(w_in1[:, D_INNER:], 1),
                    _slot_order(odd_w_out[0].astype(BF16), 0), row1(post_ln_g[1]), row1(post_ln_b[1]), tm=1024)
```

```python
import functools

import numpy as np
import jax
import jax.numpy as jnp
from jax import lax
from jax.experimental import pallas as pl
from jax.experimental.pallas import tpu as pltpu

D_MODEL = 1024
DEPTH = 2
GRID_W = 64
D_INNER = 2 * D_MODEL
HEAD_DIM = 128
A_WIDTH = D_INNER // 2
N_Q_HEADS = A_WIDTH // HEAD_DIM
N_KV_HEADS = 2
Q_PER_KV = N_Q_HEADS // N_KV_HEADS
KV_WIDTH = N_KV_HEADS * HEAD_DIM
B_WIDTH = D_INNER - A_WIDTH
CHUNK = 128
B_GROUP_DIM = 128
B_GROUPS = B_WIDTH // B_GROUP_DIM
C_GROUP_DIM = 128
C_GROUPS = D_INNER // C_GROUP_DIM
ROPE_THETA = 10000.0
EVEN_IN = A_WIDTH + 2 * KV_WIDTH + 2 * B_WIDTH + D_INNER
ODD_IN = 2 * D_INNER
ALPHA = (2 * DEPTH) ** 0.25
EPS = 1e-6

_Q0, _K0, _V0 = 0, A_WIDTH, A_WIDTH + KV_WIDTH
_U0 = A_WIDTH + 2 * KV_WIDTH
_BV0 = _U0 + B_WIDTH
_G0 = _BV0 + B_WIDTH

FFT_N1 = 8
FFT_N2 = 256
FFT_SLOTS = 64

V7X_VMEM_LIMIT_BYTES = 60000 * 1024

F32 = jnp.float32
BF16 = jnp.bfloat16


def _dot(a, b):
    return jnp.dot(a, b, preferred_element_type=F32)


def _dot_nt(a, b):
    return lax.dot_general(a, b, (((1,), (1,)), ((), ())), preferred_element_type=F32)


def _ln(x):
    mu = jnp.mean(x, axis=-1, keepdims=True)
    xc = x - mu
    var = jnp.mean(xc * xc, axis=-1, keepdims=True)
    return xc * lax.rsqrt(var + EPS)


def _rms_head(z, g):
    return z * lax.rsqrt(jnp.mean(z * z, axis=-1, keepdims=True) + EPS) * g


def _rope(y, cos2, sin2):
    return y * cos2 + pltpu.roll(y, HEAD_DIM // 2, 1) * sin2


def _params(*sem):
    return pltpu.CompilerParams(dimension_semantics=sem, vmem_limit_bytes=V7X_VMEM_LIMIT_BYTES)


def _const_spec(shape):
    nd = len(shape)
    return pl.BlockSpec(shape, lambda *_: (0,) * nd, pipeline_mode=pl.Buffered(1))


def _adaln_kernel(c_ref, w_ref, b_ref, o_ref):
    h = jax.nn.silu(c_ref[...])
    w = w_ref[0]
    h_hi = h.astype(BF16)
    h_lo = (h - h_hi.astype(F32)).astype(BF16)
    w_hi = w.astype(BF16)
    w_lo = (w - w_hi.astype(F32)).astype(BF16)
    o_ref[0] = _dot(h_hi, w_hi) + _dot(h_hi, w_lo) + _dot(h_lo, w_hi) + b_ref[0]


def _adaln(cond, w_mod, b_mod):
    r = cond.shape[0]
    tn = D_MODEL
    return pl.pallas_call(
        _adaln_kernel,
        grid=(DEPTH, 3 * D_MODEL // tn),
        in_specs=[
            pl.BlockSpec((r, D_MODEL), lambda l, j: (0, 0)),
            pl.BlockSpec((1, D_MODEL, tn), lambda l, j: (l, 0, j)),
            pl.BlockSpec((1, 1, tn), lambda l, j: (l, 0, j)),
        ],
        out_specs=pl.BlockSpec((1, r, tn), lambda l, j: (l, 0, j)),
        out_shape=jax.ShapeDtypeStruct((DEPTH, r, 3 * D_MODEL), F32),
        compiler_params=_params("arbitrary", "arbitrary"),
        name="adaln",
    )(cond, w_mod, b_mod.reshape(DEPTH, 1, 3 * D_MODEL))


def _proj_even_kernel(x_ref, sh_ref, sc_ref, w_ref, qg_ref, kg_ref, vg_ref, vb_ref, cos_ref, sin_ref,
                      q_ref, k_ref, v_ref, u_ref, vn_ref, sg_ref):
    qg = qg_ref[...] * (HEAD_DIM ** -0.5)
    kg = kg_ref[...]
    cw = 4 * HEAD_DIM
    for r0 in range(0, x_ref.shape[1], ROW_CHAIN):
        rows = slice(r0, r0 + ROW_CHAIN)
        m = (_ln(x_ref[0, rows]) * (1.0 + sc_ref[0]) + sh_ref[0]).astype(BF16)
        cos2 = cos_ref[rows]
        sin2 = sin_ref[rows]
        for c in range(A_WIDTH // cw):
            z = _dot(m, w_ref[:, _Q0 + c * cw:_Q0 + (c + 1) * cw])
            for j in range(cw // HEAD_DIM):
                zh = z[:, j * HEAD_DIM:(j + 1) * HEAD_DIM]
                q_ref[0, c * (cw // HEAD_DIM) + j, rows] = _rope(_rms_head(zh, qg), cos2, sin2).astype(BF16)
        z = _dot(m, w_ref[:, _K0:_K0 + 2 * KV_WIDTH])
        for j in range(N_KV_HEADS):
            zh = z[:, j * HEAD_DIM:(j + 1) * HEAD_DIM]
            k_ref[0, rows, j * HEAD_DIM:(j + 1) * HEAD_DIM] = _rope(_rms_head(zh, kg), cos2, sin2).astype(BF16)
        v_ref[0, rows] = z[:, KV_WIDTH:].astype(BF16)
        for c in range(B_WIDTH // cw):
            z = _dot(m, w_ref[:, _U0 + c * cw:_U0 + (c + 1) * cw])
            u_ref[0, rows, c * cw:(c + 1) * cw] = jax.nn.gelu(z).astype(BF16)
        gv = jax.nn.gelu(_dot(m, w_ref[:, _BV0:_BV0 + B_WIDTH]))
        vn_ref[0, rows] = (_ln(gv) * vg_ref[...] + vb_ref[...]).astype(BF16)
        for c in range(D_INNER // cw):
            z = _dot(m, w_ref[:, _G0 + c * cw:_G0 + (c + 1) * cw])
            sg_ref[0, rows, c * cw:(c + 1) * cw] = jax.nn.silu(z).astype(BF16)


def _proj_even(x, sh, sc, w_in, q_g, k_g, v_g, v_b, cos2, sin2, tm):
    bsz, t, _ = x.shape
    row = lambda b, i: (b, i, 0)
    per_b = pl.BlockSpec((1, 1, D_MODEL), lambda b, i: (b, 0, 0))
    bf = lambda *s: jax.ShapeDtypeStruct(s, BF16)
    return pl.pallas_call(
        _proj_even_kernel,
        grid=(bsz, t // tm),
        in_specs=[
            pl.BlockSpec((1, tm, D_MODEL), row), per_b, per_b,
            _const_spec((D_MODEL, EVEN_IN)),
            _const_spec((1, HEAD_DIM)), _const_spec((1, HEAD_DIM)),
            _const_spec((1, B_WIDTH)), _const_spec((1, B_WIDTH)),
            pl.BlockSpec((tm, HEAD_DIM), lambda b, i: (i, 0)),
            pl.BlockSpec((tm, HEAD_DIM), lambda b, i: (i, 0)),
        ],
        out_specs=[
            pl.BlockSpec((1, N_Q_HEADS, tm, HEAD_DIM), lambda b, i: (b, 0, i, 0)),
            pl.BlockSpec((1, tm, KV_WIDTH), row), pl.BlockSpec((1, tm, KV_WIDTH), row),
            pl.BlockSpec((1, tm, B_WIDTH), row), pl.BlockSpec((1, tm, B_WIDTH), row),
            pl.BlockSpec((1, tm, D_INNER), row),
        ],
        out_shape=[bf(bsz, N_Q_HEADS, t, HEAD_DIM), bf(bsz, t, KV_WIDTH), bf(bsz, t, KV_WIDTH),
                   bf(bsz, t, B_WIDTH), bf(bsz, t, B_WIDTH), bf(bsz, t, D_INNER)],
        compiler_params=_params("arbitrary", "arbitrary"),
        name="proj_even",
    )(x, sh, sc, w_in, q_g, k_g, v_g, v_b, cos2, sin2)


def _ctx_kv_kernel(c_ref, sh_ref, sc_ref, w_ref, kg_ref, kc_ref, vc_ref):
    nb, s, _ = c_ref.shape
    c = c_ref[...].reshape(nb * s, D_MODEL)
    m = (_ln(c) * (1.0 + sc_ref[...]) + sh_ref[...]).astype(BF16)
    z = _dot(m, w_ref[...])
    for j in range(N_KV_HEADS):
        zh = z[:, j * HEAD_DIM:(j + 1) * HEAD_DIM]
        kc_ref[:, :, j * HEAD_DIM:(j + 1) * HEAD_DIM] = (
            _rms_head(zh, kg_ref[...]).astype(BF16).reshape(nb, s, HEAD_DIM))
    vc_ref[...] = z[:, KV_WIDTH:].astype(BF16).reshape(nb, s, KV_WIDTH)


def _ctx_kv(ctx, sh_c, sc_c, w_in, k_g, nb):
    bsz, s, _ = ctx.shape
    assert _K0 % (2 * KV_WIDTH) == 0 and bsz % nb == 0
    return pl.pallas_call(
        _ctx_kv_kernel,
        grid=(bsz // nb,),
        in_specs=[
            pl.BlockSpec((nb, s, D_MODEL), lambda b: (b, 0, 0)),
            pl.BlockSpec((1, D_MODEL), lambda b: (0, 0)), pl.BlockSpec((1, D_MODEL), lambda b: (0, 0)),
            pl.BlockSpec((D_MODEL, 2 * KV_WIDTH), lambda b: (0, _K0 // (2 * KV_WIDTH))),
            pl.BlockSpec((1, HEAD_DIM), lambda b: (0, 0)),
        ],
        out_specs=[pl.BlockSpec((nb, s, KV_WIDTH), lambda b: (b, 0, 0))] * 2,
        out_shape=[jax.ShapeDtypeStruct((bsz, s, KV_WIDTH), BF16)] * 2,
        compiler_params=_params("arbitrary"),
        name="ctx_kv",
    )(ctx, sh_c, sc_c, w_in, k_g)


def _attn_scores(q, kc_ref, k_ref):
    return _dot_nt(q, kc_ref[0]), _dot_nt(q, k_ref[0])


def _attn_kernel(q_ref, kc_ref, vc_ref, k_ref, v_ref, qn_ref, kcn_ref, kn_ref, a_ref, s0_ref):
    nc = kc_ref.shape[1]
    first = (pl.program_id(0) == 0) & (pl.program_id(1) == 0) & (pl.program_id(2) == 0)

    @pl.when(first)
    def _():
        s_c, s_x = _attn_scores(q_ref[0, 0, :ATTN_ROWS], kc_ref, k_ref)
        s0_ref[:, :nc] = s_c
        s0_ref[:, nc:] = s_x

    for r0 in range(0, q_ref.shape[2], ATTN_ROWS):
        rows = slice(r0, r0 + ATTN_ROWS)
        for j in range(q_ref.shape[1]):
            if r0 == 0 and j == 0:
                s_c, s_x = s0_ref[:, :nc], s0_ref[:, nc:]
            else:
                s_c, s_x = _attn_scores(q_ref[0, j, rows], kc_ref, k_ref)
            mx = jnp.maximum(jnp.max(s_c, axis=-1, keepdims=True), jnp.max(s_x, axis=-1, keepdims=True))
            p_c = jnp.exp(s_c - mx)
            p_x = jnp.exp(s_x - mx)
            den = jnp.sum(p_c, axis=-1, keepdims=True) + jnp.sum(p_x, axis=-1, keepdims=True)
            o = (_dot(p_c.astype(BF16), vc_ref[0]) + _dot(p_x.astype(BF16), v_ref[0])) / den
            a_ref[0, rows, j * HEAD_DIM:(j + 1) * HEAD_DIM] = o.astype(BF16)
    s_c, s_x = _attn_scores(qn_ref[0, 0], kcn_ref, kn_ref)
    s0_ref[:, :nc] = s_c
    s0_ref[:, nc:] = s_x


def _attention(q, kc, vc, k, v, tq):
    bsz, _, t, _ = q.shape
    s = kc.shape[1]
    nt = t // tq
    last = bsz * N_KV_HEADS * nt - 1

    def nxt(b, h, i):
        n = jnp.minimum((b * N_KV_HEADS + h) * nt + i + 1, last)
        return n // (N_KV_HEADS * nt), (n // nt) % N_KV_HEADS, n % nt

    def qn_map(b, h, i):
        b2, h2, i2 = nxt(b, h, i)
        return b2, h2 * Q_PER_KV, i2 * (tq // ATTN_ROWS), 0

    def kvn_map(b, h, i):
        b2, h2, _ = nxt(b, h, i)
        return b2, 0, h2

    kv_c = pl.BlockSpec((1, s, HEAD_DIM), lambda b, h, i: (b, 0, h))
    kv_x = pl.BlockSpec((1, t, HEAD_DIM), lambda b, h, i: (b, 0, h))
    return pl.pallas_call(
        _attn_kernel,
        grid=(bsz, N_KV_HEADS, nt),
        in_specs=[pl.BlockSpec((1, Q_PER_KV, tq, HEAD_DIM), lambda b, h, i: (b, h, i, 0)),
                  kv_c, kv_c, kv_x, kv_x,
                  pl.BlockSpec((1, 1, ATTN_ROWS, HEAD_DIM), qn_map),
                  pl.BlockSpec((1, s, HEAD_DIM), kvn_map), pl.BlockSpec((1, t, HEAD_DIM), kvn_map)],
        out_specs=pl.BlockSpec((1, tq, Q_PER_KV * HEAD_DIM), lambda b, h, i: (b, i, h)),
        out_shape=jax.ShapeDtypeStruct((bsz, t, A_WIDTH), BF16),
        scratch_shapes=[pltpu.VMEM((ATTN_ROWS, s + t), F32)],
        compiler_params=_params("arbitrary", "arbitrary", "arbitrary"),
        name="attention",
    )(q, kc, vc, k, v, q, kc, k)


def _deepnorm(x, gt, y, pg, pb):
    return _ln(ALPHA * x + gt * y) * pg + pb


def _even_out_kernel(a_ref, u_ref, vn_ref, sg_ref, x_ref, gt_ref, ws_ref, bs_ref, wo_ref, pg_ref, pb_ref,
                     o_ref, comb_ref):
    for r0 in range(0, a_ref.shape[1], OUT_CHAIN):
        rc = slice(r0, r0 + OUT_CHAIN)
        comb_ref[rc, :A_WIDTH] = a_ref[0, rc] * sg_ref[0, rc, :A_WIDTH]
        for n in range(OUT_CHAIN // CHUNK):
            rows = slice(r0 + n * CHUNK, r0 + (n + 1) * CHUNK)
            for g in range(B_GROUPS):
                cols = slice(g * B_GROUP_DIM, (g + 1) * B_GROUP_DIM)
                gcols = slice(A_WIDTH + g * B_GROUP_DIM, A_WIDTH + (g + 1) * B_GROUP_DIM)
                mixed = _dot(ws_ref[g], vn_ref[0, rows, cols]) + bs_ref[g]
                comb_ref[rows, gcols] = (u_ref[0, rows, cols].astype(F32) * mixed
                                         * sg_ref[0, rows, gcols].astype(F32)).astype(BF16)
        y = _dot(comb_ref[rc], wo_ref[...])
        o_ref[0, rc] = _deepnorm(x_ref[0, rc], gt_ref[0], y, pg_ref[...], pb_ref[...])


def _even_out(a, u, vn, sg, x, gt, w_s, b_s, w_out, pg, pb, tm):
    bsz, t, _ = x.shape
    row = lambda b, i: (b, i, 0)
    return pl.pallas_call(
        _even_out_kernel,
        grid=(bsz, t // tm),
        in_specs=[
            pl.BlockSpec((1, tm, A_WIDTH), row), pl.BlockSpec((1, tm, B_WIDTH), row),
            pl.BlockSpec((1, tm, B_WIDTH), row), pl.BlockSpec((1, tm, D_INNER), row),
            pl.BlockSpec((1, tm, D_MODEL), row),
            pl.BlockSpec((1, 1, D_MODEL), lambda b, i: (b, 0, 0)),
            _const_spec((B_GROUPS, CHUNK, CHUNK)), _const_spec((B_GROUPS, CHUNK, B_GROUP_DIM)),
            _const_spec((D_INNER, D_MODEL)),
            _const_spec((1, D_MODEL)), _const_spec((1, D_MODEL)),
        ],
        out_specs=pl.BlockSpec((1, tm, D_MODEL), row),
        out_shape=jax.ShapeDtypeStruct((bsz, t, D_MODEL), F32),
        scratch_shapes=[pltpu.VMEM((tm, D_INNER), BF16)],
        compiler_params=_params("arbitrary", "arbitrary"),
        name="even_out",
    )(a, u, vn, sg, x, gt, w_s, b_s, w_out, pg, pb)


def _proj_odd_kernel(x_ref, sh_ref, sc_ref, w_ref, cs_ref, a_ref, b_ref, m_ref):
    tm = x_ref.shape[1]
    n = tm // FFT_N1
    m32 = _ln(x_ref[0]) * (1.0 + sc_ref[0]) + sh_ref[0]
    n_slab = D_MODEL // 128
    for j in range(n_slab):
        m_ref[j] = m32[:, j * 128:(j + 1) * 128]
    mp = jnp.concatenate(
        [jnp.concatenate([m_ref[j, pl.ds(t1, n, stride=FFT_N1), :] for j in range(n_slab)], axis=1)
         for t1 in range(FFT_N1)], axis=0).astype(BF16)
    cw = 4 * C_GROUP_DIM
    pw = 2 * C_GROUP_DIM
    for c in range(D_INNER // cw):
        z = _dot(mp, w_ref[:, c * cw:(c + 1) * cw]).astype(BF16)
        for j in range(cw // pw):
            pair = c * (cw // pw) + j
            cols = slice(pair * 2 * FFT_SLOTS, (pair + 1) * 2 * FFT_SLOTS)
            ab = _dot(z[:, j * pw:(j + 1) * pw], cs_ref[...])
            for t1 in range(FFT_N1):
                a_ref[0, t1, :, cols] = ab[t1 * n:(t1 + 1) * n, :2 * FFT_SLOTS].astype(BF16)
                b_ref[0, t1, :, cols] = ab[t1 * n:(t1 + 1) * n, 2 * FFT_SLOTS:].astype(BF16)


def _proj_odd(x, sh, sc, w_hb, cs_pair, tm):
    bsz, t, _ = x.shape
    row = lambda b, i: (b, i, 0)
    per_b = pl.BlockSpec((1, 1, D_MODEL), lambda b, i: (b, 0, 0))
    n = tm // FFT_N1
    width = C_GROUPS * FFT_SLOTS
    perm_spec = pl.BlockSpec((1, FFT_N1, n, width), lambda b, i: (b, 0, i, 0))
    perm = jax.ShapeDtypeStruct((bsz, FFT_N1, t // FFT_N1, width), BF16)
    return pl.pallas_call(
        _proj_odd_kernel,
        grid=(bsz, t // tm),
        in_specs=[pl.BlockSpec((1, tm, D_MODEL), row), per_b, per_b,
                  _const_spec((D_MODEL, D_INNER)), _const_spec((2 * C_GROUP_DIM, 2 * C_GROUP_DIM))],
        out_specs=[perm_spec, perm_spec],
        out_shape=[perm, perm],
        scratch_shapes=[pltpu.VMEM((D_MODEL // 128, tm, 128), F32)],
        compiler_params=_params("arbitrary", "arbitrary"),
        name="proj_odd",
    )(x, sh, sc, w_hb, cs_pair)


def _dft4(ar, ai):
    s0r, s0i = ar[0] + ar[2], ai[0] + ai[2]
    s1r, s1i = ar[0] - ar[2], ai[0] - ai[2]
    s2r, s2i = ar[1] + ar[3], ai[1] + ai[3]
    s3r, s3i = ar[1] - ar[3], ai[1] - ai[3]
    return ([s0r + s2r, s1r - s3i, s0r - s2r, s1r + s3i],
            [s0i + s2i, s1i + s3r, s0i - s2i, s1i - s3r])


def _dft8(zr, zi):
    er, ei = _dft4(zr[0::2], zi[0::2])
    orr, oi = _dft4(zr[1::2], zi[1::2])
    h = np.float32(np.sqrt(0.5))
    tr = [orr[0], (orr[1] - oi[1]) * h, -oi[2], (-orr[3] - oi[3]) * h]
    ti = [oi[0], (orr[1] + oi[1]) * h, orr[2], (orr[3] - oi[3]) * h]
    xr = [er[k] + tr[k] for k in range(4)] + [er[k] - tr[k] for k in range(4)]
    xi = [ei[k] + ti[k] for k in range(4)] + [ei[k] - ti[k] for k in range(4)]
    return xr, xi


ROW_CHAIN = 256
OUT_CHAIN = 512
ATTN_ROWS = 512
FFT_COLS = 256
FFT_ROWS = 16


def _fft_t_kernel(a_ref, b_ref, cs_ref, fp_ref, fm_ref, g_ref):
    half = FFT_N2
    lane = lax.broadcasted_iota(jnp.int32, (8, 128), 1)
    packed = (lane % FFT_SLOTS) == 0
    for ch in range(a_ref.shape[3] // FFT_COLS):
        c0 = ch * FFT_COLS
        for t1 in range(FFT_N1):
            g_ref[ch, t1, 0] = _dot(cs_ref[t1], a_ref[0, t1, :, c0:c0 + FFT_COLS])
            g_ref[ch, t1, 1] = _dot(cs_ref[t1], b_ref[0, t1, :, c0:c0 + FFT_COLS])
        for r in range(0, FFT_N2, FFT_ROWS):
            for j in range(FFT_COLS // 128):
                lanes = slice(j * 128, (j + 1) * 128)
                plus, minus = [], []
                for r8 in range(r, r + FFT_ROWS, 8):
                    rows = slice(r8, r8 + 8)
                    rows_s = slice(half + r8, half + r8 + 8)
                    pr, pi, mr, mi = [], [], [], []
                    for t1 in range(FFT_N1):
                        p1 = g_ref[ch, t1, 0, rows, lanes]
                        p3 = g_ref[ch, t1, 0, rows_s, lanes]
                        p4 = g_ref[ch, t1, 1, rows, lanes]
                        p2 = g_ref[ch, t1, 1, rows_s, lanes]
                        pr.append(jnp.where(packed, p1, p1 - p2))
                        pi.append(jnp.where(packed, p3, p3 + p4))
                        mr.append(jnp.where(packed, p4, p1 + p2))
                        mi.append(jnp.where(packed, p2, p3 - p4))
                    plus.append(_dft8(pr, pi)[0])
                    minus.append(_dft8(mr, mi)[0])
                for k1 in range(FFT_N1):
                    orow = slice(k1 * FFT_N2 + r, k1 * FFT_N2 + r + FFT_ROWS)
                    ocol = slice(c0 + j * 128, c0 + (j + 1) * 128)
                    fp_ref[0, orow, ocol] = jnp.concatenate([o[k1] for o in plus], axis=0).astype(BF16)
                    fm_ref[0, orow, ocol] = jnp.concatenate([o[k1] for o in minus], axis=0).astype(BF16)


def _fft_t(a, b, cs_t, tn):
    bsz, _, n2, w = a.shape
    t = FFT_N1 * n2
    blk = pl.BlockSpec((1, FFT_N1, n2, tn), lambda bb, j: (bb, 0, 0, j))
    out = pl.BlockSpec((1, t, tn), lambda bb, j: (bb, 0, j))
    return pl.pallas_call(
        _fft_t_kernel,
        grid=(bsz, w // tn),
        in_specs=[blk, blk, _const_spec((FFT_N1, 2 * FFT_N2, FFT_N2))],
        out_specs=[out, out],
        out_shape=[jax.ShapeDtypeStruct((bsz, t, w), BF16)] * 2,
        scratch_shapes=[pltpu.VMEM((tn // FFT_COLS, FFT_N1, 2, 2 * FFT_N2, FFT_COLS), F32)],
        compiler_params=_params("arbitrary", "arbitrary"),
        name="fft_t",
    )(a, b, cs_t)


def _odd_out_kernel(fp_ref, fm_ref, x_ref, sh_ref, sc_ref, gt_ref, wg_ref, wo_ref, pg_ref, pb_ref, o_ref):
    for r0 in range(0, x_ref.shape[1], OUT_CHAIN):
        rows = slice(r0, r0 + OUT_CHAIN)
        x = x_ref[0, rows]
        m = (_ln(x) * (1.0 + sc_ref[0]) + sh_ref[0]).astype(BF16)
        y = None
        for half, f_ref in enumerate((fp_ref, fm_ref)):
            gate = jax.nn.silu(_dot(m, wg_ref[half]))
            part = _dot((f_ref[0, rows].astype(F32) * gate).astype(BF16), wo_ref[half])
            y = part if y is None else y + part
        o_ref[0, rows] = _deepnorm(x, gt_ref[0], y, pg_ref[...], pb_ref[...])


def _odd_out(fp, fm, x, sh, sc, gt, w_g, w_out, pg, pb, tm):
    bsz, t, _ = x.shape
    row = lambda b, i: (b, i, 0)
    per_b = pl.BlockSpec((1, 1, D_MODEL), lambda b, i: (b, 0, 0))
    hw = fp.shape[2]
    return pl.pallas_call(
        _odd_out_kernel,
        grid=(bsz, t // tm),
        in_specs=[
            pl.BlockSpec((1, tm, hw), row), pl.BlockSpec((1, tm, hw), row), pl.BlockSpec((1, tm, D_MODEL), row),
            per_b, per_b, per_b,
            _const_spec((2, D_MODEL, hw)), _const_spec((2, hw, D_MODEL)),
            _const_spec((1, D_MODEL)), _const_spec((1, D_MODEL)),
        ],
        out_specs=pl.BlockSpec((1, tm, D_MODEL), row),
        out_shape=jax.ShapeDtypeStruct((bsz, t, D_MODEL), F32),
        compiler_params=_params("arbitrary", "arbitrary"),
        name="odd_out",
    )(fp, fm, x, sh, sc, gt, w_g, w_out, pg, pb)


def _slot_weights_kernel(wg_ref, wo_ref, pm_ref, pmt_ref, wg2_ref, wo2_ref):
    h = pm_ref.shape[0] // 2
    g = _dot(wg_ref[...].astype(BF16), pm_ref[...]).astype(BF16)
    wg2_ref[0] = g[:, :h]
    wg2_ref[1] = g[:, h:]
    o = _dot(pmt_ref[...], wo_ref[...].astype(BF16)).astype(BF16)
    wo2_ref[0] = o[:h]
    wo2_ref[1] = o[h:]


def _slot_weights(w_in, w_out):
    pw = 2 * C_GROUP_DIM
    n_pair = D_INNER // pw
    hw = C_GROUPS * FFT_SLOTS
    src = np.zeros(pw, np.int64)
    for grp in range(2):
        for s in range(FFT_SLOTS):
            src[grp * FFT_SLOTS + s] = grp * C_GROUP_DIM + s
            minus = FFT_SLOTS if s == 0 else C_GROUP_DIM - s
            src[2 * FFT_SLOTS + grp * FFT_SLOTS + s] = grp * C_GROUP_DIM + minus
    pm = np.zeros((pw, pw), np.float32)
    pm[src, np.arange(pw)] = 1.0
    return pl.pallas_call(
        _slot_weights_kernel,
        grid=(n_pair,),
        in_specs=[pl.BlockSpec((D_MODEL, pw), lambda p: (0, D_INNER // pw + p)),
                  pl.BlockSpec((pw, D_MODEL), lambda p: (p, 0)),
                  _const_spec((pw, pw)), _const_spec((pw, pw))],
        out_specs=[pl.BlockSpec((2, D_MODEL, pw // 2), lambda p: (0, 0, p)),
                   pl.BlockSpec((2, pw // 2, D_MODEL), lambda p: (0, p, 0))],
        out_shape=[jax.ShapeDtypeStruct((2, D_MODEL, hw), BF16), jax.ShapeDtypeStruct((2, hw, D_MODEL), BF16)],
        compiler_params=_params("arbitrary"),
        name="slot_weights",
    )(w_in, w_out, jnp.asarray(pm).astype(BF16), jnp.asarray(pm.T.copy()).astype(BF16))


def _rope_tables(n_tokens):
    tok = np.arange(n_tokens)
    row = (tok // GRID_W).astype(np.float64)
    col = (tok % GRID_W).astype(np.float64)
    n_pairs_axis = HEAD_DIM // 4
    inv_freq = ROPE_THETA ** (-np.arange(n_pairs_axis, dtype=np.float64) / n_pairs_axis)
    ang = np.concatenate([row[:, None] * inv_freq, col[:, None] * inv_freq], axis=-1)
    cs, sn = np.cos(ang), np.sin(ang)
    f = lambda a: jnp.asarray(np.ascontiguousarray(a), dtype=F32)
    return f(np.concatenate([cs, cs], axis=-1)), f(np.concatenate([-sn, sn], axis=-1))


def _dft_tables(n_tokens):
    assert n_tokens == FFT_N1 * FFT_N2 and C_GROUP_DIM == 2 * FFT_SLOTS
    c = np.arange(C_GROUP_DIM)[:, None]
    sl = np.arange(FFT_SLOTS)[None, :]
    ang = 2.0 * np.pi * c * sl / C_GROUP_DIM
    norm = 1.0 / np.sqrt(float(n_tokens * C_GROUP_DIM))
    re = np.cos(ang) * norm
    im = np.sin(ang) * norm
    im[:, 0] = np.cos(np.pi * c[:, 0]) * norm
    zero = np.zeros_like(re)
    cs_pair = np.block([[re, zero, im, zero], [zero, re, zero, im]])
    k2 = np.arange(FFT_N2)[None, :, None]
    t1 = np.arange(FFT_N1)[:, None, None]
    t2 = np.arange(FFT_N2)[None, None, :]
    ang_t = 2.0 * np.pi * k2 * (t1 + FFT_N1 * t2) / n_tokens
    cs_t = np.concatenate([np.cos(ang_t), np.sin(ang_t)], axis=1)
    f = lambda a: jnp.asarray(np.ascontiguousarray(a), dtype=F32)
    return f(cs_pair).astype(BF16), f(cs_t).astype(BF16)


def kernel(x, c, ctx, c_ctx, w_mod, b_mod, post_ln_g, post_ln_b, even_w_in, even_q_norm, even_k_norm,
           even_v_ln_g, even_v_ln_b, even_w_s, even_b_s, even_w_out, odd_w_in, odd_w_out):
    bsz, t, _ = x.shape
    assert DEPTH == 2 and t % CHUNK == 0
    row1 = lambda v: v.reshape(1, -1)

    n_cond = -(-(bsz + 1) // 8) * 8
    cond = jnp.zeros((n_cond, D_MODEL), F32).at[:bsz].set(c).at[bsz].set(c_ctx)
    mod = _adaln(cond, w_mod, b_mod)
    split = lambda l, rows: [mod[l, rows, i * D_MODEL:(i + 1) * D_MODEL] for i in range(3)]
    sh0, sc0, gt0 = [v[:, None, :] for v in split(0, slice(0, bsz))]
    sh0c, sc0c, _ = split(0, slice(bsz, bsz + 1))
    sh1, sc1, gt1 = [v[:, None, :] for v in split(1, slice(0, bsz))]

    cos2, sin2 = _rope_tables(t)
    cs_pair, cs_t = _dft_tables(t)

    w_in0 = even_w_in[0].astype(BF16)
    q, k, v, u, vn, sg = _proj_even(x, sh0, sc0, w_in0, row1(even_q_norm[0]), row1(even_k_norm[0]),
                                    row1(even_v_ln_g[0]), row1(even_v_ln_b[0]), cos2, sin2, tm=1024)
    kc, vc = _ctx_kv(ctx, sh0c, sc0c, w_in0, row1(even_k_norm[0]), nb=4)
    a = _attention(q, kc, vc, k, v, tq=1024)
    b_s = jnp.broadcast_to(even_b_s[0][:, :, None], (B_GROUPS, CHUNK, B_GROUP_DIM))
    x1 = _even_out(a, u, vn, sg, x, gt0, even_w_s[0].astype(BF16), b_s, even_w_out[0].astype(BF16),
                   row1(post_ln_g[0]), row1(post_ln_b[0]), tm=1024)

    fa, fb = _proj_odd(x1, sh1, sc1, odd_w_in[0, :, :D_INNER].astype(BF16), cs_pair, tm=1024)
    fp, fm = _fft_t(fa, fb, cs_t, tn=512)
    w_g, w_out1 = _slot_weights(odd_w_in[0], odd_w_out[0])
    return _odd_out(fp, fm, x1, sh1, sc1, gt1, w_g, w_out1, row1(post_ln_g[1]), row1(post_ln_b[1]), tm=1024)
```

```python
import numpy as np
import jax
import jax.numpy as jnp
from jax import lax
from jax.experimental import pallas as pl
from jax.experimental.pallas import tpu as pltpu

D_MODEL = 1024
DEPTH = 2
GRID_W = 64
D_INNER = 2 * D_MODEL
HEAD_DIM = 128
A_WIDTH = D_INNER // 2
N_Q_HEADS = A_WIDTH // HEAD_DIM
N_KV_HEADS = 2
Q_PER_KV = N_Q_HEADS // N_KV_HEADS
KV_WIDTH = N_KV_HEADS * HEAD_DIM
B_WIDTH = D_INNER - A_WIDTH
CHUNK = 128
B_GROUP_DIM = 128
B_GROUPS = B_WIDTH // B_GROUP_DIM
C_GROUP_DIM = 128
C_GROUPS = D_INNER // C_GROUP_DIM
ROPE_THETA = 10000.0
EVEN_IN = A_WIDTH + 2 * KV_WIDTH + 2 * B_WIDTH + D_INNER
ALPHA = (2 * DEPTH) ** 0.25
EPS = 1e-6

_Q0, _K0 = 0, A_WIDTH
_U0 = A_WIDTH + 2 * KV_WIDTH
_BV0 = _U0 + B_WIDTH
_G0 = _BV0 + B_WIDTH

FFT_N1 = 8
FFT_N2 = 256
FFT_SLOTS = 64

V7X_VMEM_LIMIT_BYTES = 60000 * 1024

F32 = jnp.float32
BF16 = jnp.bfloat16


def _dot(a, b):
    return jnp.dot(a, b, preferred_element_type=F32)


def _dot_nt(a, b):
    return lax.dot_general(a, b, (((1,), (1,)), ((), ())), preferred_element_type=F32)


def _ln(x):
    mu = jnp.mean(x, axis=-1, keepdims=True)
    xc = x - mu
    var = jnp.mean(xc * xc, axis=-1, keepdims=True)
    return xc * lax.rsqrt(var + EPS)


def _rms_head(z, g):
    return z * lax.rsqrt(jnp.mean(z * z, axis=-1, keepdims=True) + EPS) * g


def _rope(y, cos2, sin2):
    return y * cos2 + pltpu.roll(y, HEAD_DIM // 2, 1) * sin2


def _params(*sem):
    return pltpu.CompilerParams(dimension_semantics=sem, vmem_limit_bytes=V7X_VMEM_LIMIT_BYTES)


def _const_spec(shape):
    nd = len(shape)
    return pl.BlockSpec(shape, lambda *_: (0,) * nd, pipeline_mode=pl.Buffered(1))


def _adaln_kernel(c_ref, w_ref, b_ref, o_ref):
    h = jax.nn.silu(c_ref[...])
    w = w_ref[0]
    h_hi = h.astype(BF16)
    h_lo = (h - h_hi.astype(F32)).astype(BF16)
    w_hi = w.astype(BF16)
    w_lo = (w - w_hi.astype(F32)).astype(BF16)
    o_ref[0] = _dot(h_hi, w_hi) + _dot(h_hi, w_lo) + _dot(h_lo, w_hi) + b_ref[0]


def _adaln(cond, w_mod, b_mod):
    r = cond.shape[0]
    tn = D_MODEL
    return pl.pallas_call(
        _adaln_kernel,
        grid=(DEPTH, 3 * D_MODEL // tn),
        in_specs=[
            pl.BlockSpec((r, D_MODEL), lambda l, j: (0, 0)),
            pl.BlockSpec((1, D_MODEL, tn), lambda l, j: (l, 0, j)),
            pl.BlockSpec((1, 1, tn), lambda l, j: (l, 0, j)),
        ],
        out_specs=pl.BlockSpec((1, r, tn), lambda l, j: (l, 0, j)),
        out_shape=jax.ShapeDtypeStruct((DEPTH, r, 3 * D_MODEL), F32),
        compiler_params=_params("arbitrary", "arbitrary"),
        name="adaln",
    )(cond, w_mod, b_mod.reshape(DEPTH, 1, 3 * D_MODEL))


def _proj_even_kernel(x_ref, sh_ref, sc_ref, w_ref, qg_ref, kg_ref, vg_ref, vb_ref, cos_ref, sin_ref,
                      q_ref, k_ref, v_ref, u_ref, vn_ref, sg_ref):
    qg = qg_ref[...] * (HEAD_DIM ** -0.5)
    kg = kg_ref[...]
    cw = 4 * HEAD_DIM
    for r0 in range(0, x_ref.shape[1], ROW_CHAIN):
        rows = slice(r0, r0 + ROW_CHAIN)
        m = (_ln(x_ref[0, rows]) * (1.0 + sc_ref[0]) + sh_ref[0]).astype(BF16)
        cos2 = cos_ref[rows]
        sin2 = sin_ref[rows]
        for c in range(A_WIDTH // cw):
            z = _dot(m, w_ref[:, _Q0 + c * cw:_Q0 + (c + 1) * cw])
            for j in range(cw // HEAD_DIM):
                zh = z[:, j * HEAD_DIM:(j + 1) * HEAD_DIM]
                q_ref[0, c * (cw // HEAD_DIM) + j, rows] = _rope(_rms_head(zh, qg), cos2, sin2).astype(BF16)
        z = _dot(m, w_ref[:, _K0:_K0 + 2 * KV_WIDTH])
        for j in range(N_KV_HEADS):
            zh = z[:, j * HEAD_DIM:(j + 1) * HEAD_DIM]
            k_ref[0, rows, j * HEAD_DIM:(j + 1) * HEAD_DIM] = _rope(_rms_head(zh, kg), cos2, sin2).astype(BF16)
        v_ref[0, rows] = z[:, KV_WIDTH:].astype(BF16)
        for c in range(B_WIDTH // cw):
            z = _dot(m, w_ref[:, _U0 + c * cw:_U0 + (c + 1) * cw])
            u_ref[0, rows, c * cw:(c + 1) * cw] = jax.nn.gelu(z).astype(BF16)
        gv = jax.nn.gelu(_dot(m, w_ref[:, _BV0:_BV0 + B_WIDTH]))
        vn_ref[0, rows] = (_ln(gv) * vg_ref[...] + vb_ref[...]).astype(BF16)
        for c in range(D_INNER // cw):
            z = _dot(m, w_ref[:, _G0 + c * cw:_G0 + (c + 1) * cw])
            sg_ref[0, rows, c * cw:(c + 1) * cw] = jax.nn.silu(z).astype(BF16)


def _proj_even(x, sh, sc, w_in, q_g, k_g, v_g, v_b, cos2, sin2, tm):
    bsz, t, _ = x.shape
    row = lambda b, i: (b, i, 0)
    per_b = pl.BlockSpec((1, 1, D_MODEL), lambda b, i: (b, 0, 0))
    bf = lambda *s: jax.ShapeDtypeStruct(s, BF16)
    return pl.pallas_call(
        _proj_even_kernel,
        grid=(bsz, t // tm),
        in_specs=[
            pl.BlockSpec((1, tm, D_MODEL), row), per_b, per_b,
            _const_spec((D_MODEL, EVEN_IN)),
            _const_spec((1, HEAD_DIM)), _const_spec((1, HEAD_DIM)),
            _const_spec((1, B_WIDTH)), _const_spec((1, B_WIDTH)),
            pl.BlockSpec((tm, HEAD_DIM), lambda b, i: (i, 0)),
            pl.BlockSpec((tm, HEAD_DIM), lambda b, i: (i, 0)),
        ],
        out_specs=[
            pl.BlockSpec((1, N_Q_HEADS, tm, HEAD_DIM), lambda b, i: (b, 0, i, 0)),
            pl.BlockSpec((1, tm, KV_WIDTH), row), pl.BlockSpec((1, tm, KV_WIDTH), row),
            pl.BlockSpec((1, tm, B_WIDTH), row), pl.BlockSpec((1, tm, B_WIDTH), row),
            pl.BlockSpec((1, tm, D_INNER), row),
        ],
        out_shape=[bf(bsz, N_Q_HEADS, t, HEAD_DIM), bf(bsz, t, KV_WIDTH), bf(bsz, t, KV_WIDTH),
                   bf(bsz, t, B_WIDTH), bf(bsz, t, B_WIDTH), bf(bsz, t, D_INNER)],
        compiler_params=_params("arbitrary", "arbitrary"),
        name="proj_even",
    )(x, sh, sc, w_in, q_g, k_g, v_g, v_b, cos2, sin2)


def _ctx_kv_kernel(c_ref, sh_ref, sc_ref, w_ref, kg_ref, kc_ref, vc_ref):
    nb, s, _ = c_ref.shape
    c = c_ref[...].reshape(nb * s, D_MODEL)
    m = (_ln(c) * (1.0 + sc_ref[...]) + sh_ref[...]).astype(BF16)
    z = _dot(m, w_ref[...])
    for j in range(N_KV_HEADS):
        zh = z[:, j * HEAD_DIM:(j + 1) * HEAD_DIM]
        kc_ref[:, :, j * HEAD_DIM:(j + 1) * HEAD_DIM] = (
            _rms_head(zh, kg_ref[...]).astype(BF16).reshape(nb, s, HEAD_DIM))
    vc_ref[...] = z[:, KV_WIDTH:].astype(BF16).reshape(nb, s, KV_WIDTH)


def _ctx_kv(ctx, sh_c, sc_c, w_in, k_g, nb):
    bsz, s, _ = ctx.shape
    assert _K0 % (2 * KV_WIDTH) == 0 and bsz % nb == 0
    return pl.pallas_call(
        _ctx_kv_kernel,
        grid=(bsz // nb,),
        in_specs=[
            pl.BlockSpec((nb, s, D_MODEL), lambda b: (b, 0, 0)),
            pl.BlockSpec((1, D_MODEL), lambda b: (0, 0)), pl.BlockSpec((1, D_MODEL), lambda b: (0, 0)),
            pl.BlockSpec((D_MODEL, 2 * KV_WIDTH), lambda b: (0, _K0 // (2 * KV_WIDTH))),
            pl.BlockSpec((1, HEAD_DIM), lambda b: (0, 0)),
        ],
        out_specs=[pl.BlockSpec((nb, s, KV_WIDTH), lambda b: (b, 0, 0))] * 2,
        out_shape=[jax.ShapeDtypeStruct((bsz, s, KV_WIDTH), BF16)] * 2,
        compiler_params=_params("arbitrary"),
        name="ctx_kv",
    )(ctx, sh_c, sc_c, w_in, k_g)


def _attn_scores(q, kc_ref, k_ref):
    return _dot_nt(q, kc_ref[0]), _dot_nt(q, k_ref[0])


def _attn_kernel(q_ref, kc_ref, vc_ref, k_ref, v_ref, qn_ref, kcn_ref, kn_ref, a_ref, s0_ref):
    nc = kc_ref.shape[1]
    first = (pl.program_id(0) == 0) & (pl.program_id(1) == 0) & (pl.program_id(2) == 0)

    @pl.when(first)
    def _():
        s_c, s_x = _attn_scores(q_ref[0, 0, :ATTN_ROWS], kc_ref, k_ref)
        s0_ref[:, :nc] = s_c
        s0_ref[:, nc:] = s_x

    for r0 in range(0, q_ref.shape[2], ATTN_ROWS):
        rows = slice(r0, r0 + ATTN_ROWS)
        for j in range(q_ref.shape[1]):
            if r0 == 0 and j == 0:
                s_c, s_x = s0_ref[:, :nc], s0_ref[:, nc:]
            else:
                s_c, s_x = _attn_scores(q_ref[0, j, rows], kc_ref, k_ref)
            mx = jnp.maximum(jnp.max(s_c, axis=-1, keepdims=True), jnp.max(s_x, axis=-1, keepdims=True))
            p_c = jnp.exp(s_c - mx)
            p_x = jnp.exp(s_x - mx)
            den = jnp.sum(p_c, axis=-1, keepdims=True) + jnp.sum(p_x, axis=-1, keepdims=True)
            o = (_dot(p_c.astype(BF16), vc_ref[0]) + _dot(p_x.astype(BF16), v_ref[0])) / den
            a_ref[0, rows, j * HEAD_DIM:(j + 1) * HEAD_DIM] = o.astype(BF16)
    s_c, s_x = _attn_scores(qn_ref[0, 0], kcn_ref, kn_ref)
    s0_ref[:, :nc] = s_c
    s0_ref[:, nc:] = s_x


def _attention(q, kc, vc, k, v, tq):
    bsz, _, t, _ = q.shape
    s = kc.shape[1]
    nt = t // tq
    last = bsz * N_KV_HEADS * nt - 1

    def nxt(b, h, i):
        n = jnp.minimum((b * N_KV_HEADS + h) * nt + i + 1, last)
        return n // (N_KV_HEADS * nt), (n // nt) % N_KV_HEADS, n % nt

    def qn_map(b, h, i):
        b2, h2, i2 = nxt(b, h, i)
        return b2, h2 * Q_PER_KV, i2 * (tq // ATTN_ROWS), 0

    def kvn_map(b, h, i):
        b2, h2, _ = nxt(b, h, i)
        return b2, 0, h2

    kv_c = pl.BlockSpec((1, s, HEAD_DIM), lambda b, h, i: (b, 0, h))
    kv_x = pl.BlockSpec((1, t, HEAD_DIM), lambda b, h, i: (b, 0, h))
    return pl.pallas_call(
        _attn_kernel,
        grid=(bsz, N_KV_HEADS, nt),
        in_specs=[pl.BlockSpec((1, Q_PER_KV, tq, HEAD_DIM), lambda b, h, i: (b, h, i, 0)),
                  kv_c, kv_c, kv_x, kv_x,
                  pl.BlockSpec((1, 1, ATTN_ROWS, HEAD_DIM), qn_map),
                  pl.BlockSpec((1, s, HEAD_DIM), kvn_map), pl.BlockSpec((1, t, HEAD_DIM), kvn_map)],
        out_specs=pl.BlockSpec((1, tq, Q_PER_KV * HEAD_DIM), lambda b, h, i: (b, i, h)),
        out_shape=jax.ShapeDtypeStruct((bsz, t, A_WIDTH), BF16),
        scratch_shapes=[pltpu.VMEM((ATTN_ROWS, s + t), F32)],
        compiler_params=_params("arbitrary", "arbitrary", "arbitrary"),
        name="attention",
    )(q, kc, vc, k, v, q, kc, k)


def _deepnorm(x, gt, y, pg, pb):
    return _ln(ALPHA * x + gt * y) * pg + pb


def _even_out_kernel(a_ref, u_ref, vn_ref, sg_ref, x_ref, gt_ref, ws_ref, bs_ref, wo_ref, pg_ref, pb_ref,
                     o_ref, comb_ref):
    for r0 in range(0, a_ref.shape[1], OUT_CHAIN):
        rc = slice(r0, r0 + OUT_CHAIN)
        comb_ref[rc, :A_WIDTH] = a_ref[0, rc] * sg_ref[0, rc, :A_WIDTH]
        for n in range(OUT_CHAIN // CHUNK):
            rows = slice(r0 + n * CHUNK, r0 + (n + 1) * CHUNK)
            for g in range(B_GROUPS):
                cols = slice(g * B_GROUP_DIM, (g + 1) * B_GROUP_DIM)
                gcols = slice(A_WIDTH + g * B_GROUP_DIM, A_WIDTH + (g + 1) * B_GROUP_DIM)
                mixed = _dot(ws_ref[g], vn_ref[0, rows, cols]) + bs_ref[g]
                comb_ref[rows, gcols] = (u_ref[0, rows, cols].astype(F32) * mixed
                                         * sg_ref[0, rows, gcols].astype(F32)).astype(BF16)
        y = _dot(comb_ref[rc], wo_ref[...])
        o_ref[0, rc] = _deepnorm(x_ref[0, rc], gt_ref[0], y, pg_ref[...], pb_ref[...])


def _even_out(a, u, vn, sg, x, gt, w_s, b_s, w_out, pg, pb, tm):
    bsz, t, _ = x.shape
    row = lambda b, i: (b, i, 0)
    return pl.pallas_call(
        _even_out_kernel,
        grid=(bsz, t // tm),
        in_specs=[
            pl.BlockSpec((1, tm, A_WIDTH), row), pl.BlockSpec((1, tm, B_WIDTH), row),
            pl.BlockSpec((1, tm, B_WIDTH), row), pl.BlockSpec((1, tm, D_INNER), row),
            pl.BlockSpec((1, tm, D_MODEL), row),
            pl.BlockSpec((1, 1, D_MODEL), lambda b, i: (b, 0, 0)),
            _const_spec((B_GROUPS, CHUNK, CHUNK)), _const_spec((B_GROUPS, CHUNK, B_GROUP_DIM)),
            _const_spec((D_INNER, D_MODEL)),
            _const_spec((1, D_MODEL)), _const_spec((1, D_MODEL)),
        ],
        out_specs=pl.BlockSpec((1, tm, D_MODEL), row),
        out_shape=jax.ShapeDtypeStruct((bsz, t, D_MODEL), F32),
        scratch_shapes=[pltpu.VMEM((tm, D_INNER), BF16)],
        compiler_params=_params("arbitrary", "arbitrary"),
        name="even_out",
    )(a, u, vn, sg, x, gt, w_s, b_s, w_out, pg, pb)


def _proj_odd_kernel(x_ref, sh_ref, sc_ref, w_ref, cs_ref, a_ref, b_ref, m_ref):
    tm = x_ref.shape[1]
    n = tm // FFT_N1
    m32 = _ln(x_ref[0]) * (1.0 + sc_ref[0]) + sh_ref[0]
    n_slab = D_MODEL // 128
    for j in range(n_slab):
        m_ref[j] = m32[:, j * 128:(j + 1) * 128]
    mp = jnp.concatenate(
        [jnp.concatenate([m_ref[j, pl.ds(t1, n, stride=FFT_N1), :] for j in range(n_slab)], axis=1)
         for t1 in range(FFT_N1)], axis=0).astype(BF16)
    cw = 4 * C_GROUP_DIM
    pw = 2 * C_GROUP_DIM
    for c in range(D_INNER // cw):
        z = _dot(mp, w_ref[:, c * cw:(c + 1) * cw]).astype(BF16)
        for j in range(cw // pw):
            pair = c * (cw // pw) + j
            cols = slice(pair * 2 * FFT_SLOTS, (pair + 1) * 2 * FFT_SLOTS)
            ab = _dot(z[:, j * pw:(j + 1) * pw], cs_ref[...])
            for t1 in range(FFT_N1):
                a_ref[0, t1, :, cols] = ab[t1 * n:(t1 + 1) * n, :2 * FFT_SLOTS].astype(BF16)
                b_ref[0, t1, :, cols] = ab[t1 * n:(t1 + 1) * n, 2 * FFT_SLOTS:].astype(BF16)


def _proj_odd(x, sh, sc, w_hb, cs_pair, tm):
    bsz, t, _ = x.shape
    row = lambda b, i: (b, i, 0)
    per_b = pl.BlockSpec((1, 1, D_MODEL), lambda b, i: (b, 0, 0))
    n = tm // FFT_N1
    width = C_GROUPS * FFT_SLOTS
    perm_spec = pl.BlockSpec((1, FFT_N1, n, width), lambda b, i: (b, 0, i, 0))
    perm = jax.ShapeDtypeStruct((bsz, FFT_N1, t // FFT_N1, width), BF16)
    return pl.pallas_call(
        _proj_odd_kernel,
        grid=(bsz, t // tm),
        in_specs=[pl.BlockSpec((1, tm, D_MODEL), row), per_b, per_b,
                  _const_spec((D_MODEL, D_INNER)), _const_spec((2 * C_GROUP_DIM, 2 * C_GROUP_DIM))],
        out_specs=[perm_spec, perm_spec],
        out_shape=[perm, perm],
        scratch_shapes=[pltpu.VMEM((D_MODEL // 128, tm, 128), F32)],
        compiler_params=_params("arbitrary", "arbitrary"),
        name="proj_odd",
    )(x, sh, sc, w_hb, cs_pair)


def _dft4(ar, ai):
    s0r, s0i = ar[0] + ar[2], ai[0] + ai[2]
    s1r, s1i = ar[0] - ar[2], ai[0] - ai[2]
    s2r, s2i = ar[1] + ar[3], ai[1] + ai[3]
    s3r, s3i = ar[1] - ar[3], ai[1] - ai[3]
    return ([s0r + s2r, s1r - s3i, s0r - s2r, s1r + s3i],
            [s0i + s2i, s1i + s3r, s0i - s2i, s1i - s3r])


def _dft8(zr, zi):
    er, ei = _dft4(zr[0::2], zi[0::2])
    orr, oi = _dft4(zr[1::2], zi[1::2])
    h = np.float32(np.sqrt(0.5))
    tr = [orr[0], (orr[1] - oi[1]) * h, -oi[2], (-orr[3] - oi[3]) * h]
    ti = [oi[0], (orr[1] + oi[1]) * h, orr[2], (orr[3] - oi[3]) * h]
    xr = [er[k] + tr[k] for k in range(4)] + [er[k] - tr[k] for k in range(4)]
    xi = [ei[k] + ti[k] for k in range(4)] + [ei[k] - ti[k] for k in range(4)]
    return xr, xi


ROW_CHAIN = 256
OUT_CHAIN = 512
ATTN_ROWS = 512
FFT_COLS = 256
FFT_ROWS = 16


def _fft_t_kernel(a_ref, b_ref, cs_ref, fp_ref, fm_ref, g_ref):
    half = FFT_N2
    lane = lax.broadcasted_iota(jnp.int32, (8, 128), 1)
    packed = (lane % FFT_SLOTS) == 0
    for ch in range(a_ref.shape[3] // FFT_COLS):
        c0 = ch * FFT_COLS
        for t1 in range(FFT_N1):
            g_ref[ch, t1, 0] = _dot(cs_ref[t1], a_ref[0, t1, :, c0:c0 + FFT_COLS])
            g_ref[ch, t1, 1] = _dot(cs_ref[t1], b_ref[0, t1, :, c0:c0 + FFT_COLS])
        for r in range(0, FFT_N2, FFT_ROWS):
            for j in range(FFT_COLS // 128):
                lanes = slice(j * 128, (j + 1) * 128)
                plus, minus = [], []
                for r8 in range(r, r + FFT_ROWS, 8):
                    rows = slice(r8, r8 + 8)
                    rows_s = slice(half + r8, half + r8 + 8)
                    pr, pi, mr, mi = [], [], [], []
                    for t1 in range(FFT_N1):
                        p1 = g_ref[ch, t1, 0, rows, lanes]
                        p3 = g_ref[ch, t1, 0, rows_s, lanes]
                        p4 = g_ref[ch, t1, 1, rows, lanes]
                        p2 = g_ref[ch, t1, 1, rows_s, lanes]
                        pr.append(jnp.where(packed, p1, p1 - p2))
                        pi.append(jnp.where(packed, p3, p3 + p4))
                        mr.append(jnp.where(packed, p4, p1 + p2))
                        mi.append(jnp.where(packed, p2, p3 - p4))
                    plus.append(_dft8(pr, pi)[0])
                    minus.append(_dft8(mr, mi)[0])
                for k1 in range(FFT_N1):
                    orow = slice(k1 * FFT_N2 + r, k1 * FFT_N2 + r + FFT_ROWS)
                    ocol = slice(c0 + j * 128, c0 + (j + 1) * 128)
                    fp_ref[0, orow, ocol] = jnp.concatenate([o[k1] for o in plus], axis=0).astype(BF16)
                    fm_ref[0, orow, ocol] = jnp.concatenate([o[k1] for o in minus], axis=0).astype(BF16)


def _fft_t(a, b, cs_t, tn):
    bsz, _, n2, w = a.shape
    t = FFT_N1 * n2
    blk = pl.BlockSpec((1, FFT_N1, n2, tn), lambda bb, j: (bb, 0, 0, j))
    out = pl.BlockSpec((1, t, tn), lambda bb, j: (bb, 0, j))
    return pl.pallas_call(
        _fft_t_kernel,
        grid=(bsz, w // tn),
        in_specs=[blk, blk, _const_spec((FFT_N1, 2 * FFT_N2, FFT_N2))],
        out_specs=[out, out],
        out_shape=[jax.ShapeDtypeStruct((bsz, t, w), BF16)] * 2,
        scratch_shapes=[pltpu.VMEM((tn // FFT_COLS, FFT_N1, 2, 2 * FFT_N2, FFT_COLS), F32)],
        compiler_params=_params("arbitrary", "arbitrary"),
        name="fft_t",
    )(a, b, cs_t)


def _odd_out_kernel(fp_ref, fm_ref, x_ref, sh_ref, sc_ref, gt_ref, wg_ref, wo_ref, pg_ref, pb_ref, o_ref):
    for r0 in range(0, x_ref.shape[1], OUT_CHAIN):
        rows = slice(r0, r0 + OUT_CHAIN)
        x = x_ref[0, rows]
        m = (_ln(x) * (1.0 + sc_ref[0]) + sh_ref[0]).astype(BF16)
        y = None
        for half, f_ref in enumerate((fp_ref, fm_ref)):
            gate = jax.nn.silu(_dot(m, wg_ref[half]))
            part = _dot((f_ref[0, rows].astype(F32) * gate).astype(BF16), wo_ref[half])
            y = part if y is None else y + part
        o_ref[0, rows] = _deepnorm(x, gt_ref[0], y, pg_ref[...], pb_ref[...])


def _odd_out(fp, fm, x, sh, sc, gt, w_g, w_out, pg, pb, tm):
    bsz, t, _ = x.shape
    row = lambda b, i: (b, i, 0)
    per_b = pl.BlockSpec((1, 1, D_MODEL), lambda b, i: (b, 0, 0))
    hw = fp.shape[2]
    return pl.pallas_call(
        _odd_out_kernel,
        grid=(bsz, t // tm),
        in_specs=[
            pl.BlockSpec((1, tm, hw), row), pl.BlockSpec((1, tm, hw), row), pl.BlockSpec((1, tm, D_MODEL), row),
            per_b, per_b, per_b,
            _const_spec((2, D_MODEL, hw)), _const_spec((2, hw, D_MODEL)),
            _const_spec((1, D_MODEL)), _const_spec((1, D_MODEL)),
        ],
        out_specs=pl.BlockSpec((1, tm, D_MODEL), row),
        out_shape=jax.ShapeDtypeStruct((bsz, t, D_MODEL), F32),
        compiler_params=_params("arbitrary", "arbitrary"),
        name="odd_out",
    )(fp, fm, x, sh, sc, gt, w_g, w_out, pg, pb)


def _slot_weights_kernel(wh_ref, wg_ref, wo_ref, pm_ref, pmt_ref, wh2_ref, wg2_ref, wo2_ref):
    wh2_ref[...] = wh_ref[...].astype(BF16)
    h = pm_ref.shape[0] // 2
    g = _dot(wg_ref[...].astype(BF16), pm_ref[...]).astype(BF16)
    wg2_ref[0] = g[:, :h]
    wg2_ref[1] = g[:, h:]
    o = _dot(pmt_ref[...], wo_ref[...].astype(BF16)).astype(BF16)
    wo2_ref[0] = o[:h]
    wo2_ref[1] = o[h:]


def _slot_weights(w_in, w_out):
    pw = 2 * C_GROUP_DIM
    n_pair = D_INNER // pw
    hw = C_GROUPS * FFT_SLOTS
    src = np.zeros(pw, np.int64)
    for grp in range(2):
        for s in range(FFT_SLOTS):
            src[grp * FFT_SLOTS + s] = grp * C_GROUP_DIM + s
            minus = FFT_SLOTS if s == 0 else C_GROUP_DIM - s
            src[2 * FFT_SLOTS + grp * FFT_SLOTS + s] = grp * C_GROUP_DIM + minus
    pm = np.zeros((pw, pw), np.float32)
    pm[src, np.arange(pw)] = 1.0
    return pl.pallas_call(
        _slot_weights_kernel,
        grid=(n_pair,),
        in_specs=[pl.BlockSpec((D_MODEL, pw), lambda p: (0, p)),
                  pl.BlockSpec((D_MODEL, pw), lambda p: (0, D_INNER // pw + p)),
                  pl.BlockSpec((pw, D_MODEL), lambda p: (p, 0)),
                  _const_spec((pw, pw)), _const_spec((pw, pw))],
        out_specs=[pl.BlockSpec((D_MODEL, pw), lambda p: (0, p)),
                   pl.BlockSpec((2, D_MODEL, pw // 2), lambda p: (0, 0, p)),
                   pl.BlockSpec((2, pw // 2, D_MODEL), lambda p: (0, p, 0))],
        out_shape=[jax.ShapeDtypeStruct((D_MODEL, D_INNER), BF16),
                   jax.ShapeDtypeStruct((2, D_MODEL, hw), BF16), jax.ShapeDtypeStruct((2, hw, D_MODEL), BF16)],
        compiler_params=_params("arbitrary"),
        name="slot_weights",
    )(w_in, w_in, w_out, jnp.asarray(pm).astype(BF16), jnp.asarray(pm.T.copy()).astype(BF16))


def _rope_tables(n_tokens):
    tok = np.arange(n_tokens)
    row = (tok // GRID_W).astype(np.float64)
    col = (tok % GRID_W).astype(np.float64)
    n_pairs_axis = HEAD_DIM // 4
    inv_freq = ROPE_THETA ** (-np.arange(n_pairs_axis, dtype=np.float64) / n_pairs_axis)
    ang = np.concatenate([row[:, None] * inv_freq, col[:, None] * inv_freq], axis=-1)
    cs, sn = np.cos(ang), np.sin(ang)
    f = lambda a: jnp.asarray(np.ascontiguousarray(a), dtype=F32)
    return f(np.concatenate([cs, cs], axis=-1)), f(np.concatenate([-sn, sn], axis=-1))


def _dft_tables(n_tokens):
    assert n_tokens == FFT_N1 * FFT_N2 and C_GROUP_DIM == 2 * FFT_SLOTS
    c = np.arange(C_GROUP_DIM)[:, None]
    sl = np.arange(FFT_SLOTS)[None, :]
    ang = 2.0 * np.pi * c * sl / C_GROUP_DIM
    norm = 1.0 / np.sqrt(float(n_tokens * C_GROUP_DIM))
    re = np.cos(ang) * norm
    im = np.sin(ang) * norm
    im[:, 0] = np.cos(np.pi * c[:, 0]) * norm
    zero = np.zeros_like(re)
    cs_pair = np.block([[re, zero, im, zero], [zero, re, zero, im]])
    k2 = np.arange(FFT_N2)[None, :, None]
    t1 = np.arange(FFT_N1)[:, None, None]
    t2 = np.arange(FFT_N2)[None, None, :]
    ang_t = 2.0 * np.pi * k2 * (t1 + FFT_N1 * t2) / n_tokens
    cs_t = np.concatenate([np.cos(ang_t), np.sin(ang_t)], axis=1)
    f = lambda a: jnp.asarray(np.ascontiguousarray(a), dtype=F32)
    return f(cs_pair).astype(BF16), f(cs_t).astype(BF16)


def kernel(x, c, ctx, c_ctx, w_mod, b_mod, post_ln_g, post_ln_b, even_w_in, even_q_norm, even_k_norm,
           even_v_ln_g, even_v_ln_b, even_w_s, even_b_s, even_w_out, odd_w_in, odd_w_out):
    bsz, t, _ = x.shape
    assert DEPTH == 2 and t % CHUNK == 0
    row1 = lambda v: v.reshape(1, -1)

    n_cond = -(-(bsz + 1) // 8) * 8
    cond = jnp.zeros((n_cond, D_MODEL), F32).at[:bsz].set(c).at[bsz].set(c_ctx)
    mod = _adaln(cond, w_mod, b_mod)
    split = lambda l, rows: [mod[l, rows, i * D_MODEL:(i + 1) * D_MODEL] for i in range(3)]
    sh0, sc0, gt0 = [v[:, None, :] for v in split(0, slice(0, bsz))]
    sh0c, sc0c, _ = split(0, slice(bsz, bsz + 1))
    sh1, sc1, gt1 = [v[:, None, :] for v in split(1, slice(0, bsz))]

    cos2, sin2 = _rope_tables(t)
    cs_pair, cs_t = _dft_tables(t)

    w_in0 = even_w_in[0].astype(BF16)
    q, k, v, u, vn, sg = _proj_even(x, sh0, sc0, w_in0, row1(even_q_norm[0]), row1(even_k_norm[0]),
                                    row1(even_v_ln_g[0]), row1(even_v_ln_b[0]), cos2, sin2, tm=1024)
    kc, vc = _ctx_kv(ctx, sh0c, sc0c, w_in0, row1(even_k_norm[0]), nb=4)
    a = _attention(q, kc, vc, k, v, tq=1024)
    b_s = jnp.broadcast_to(even_b_s[0][:, :, None], (B_GROUPS, CHUNK, B_GROUP_DIM))
    x1 = _even_out(a, u, vn, sg, x, gt0, even_w_s[0].astype(BF16), b_s, even_w_out[0].astype(BF16),
                   row1(post_ln_g[0]), row1(post_ln_b[0]), tm=1024)

    w_hb, w_g, w_out1 = _slot_weights(odd_w_in[0], odd_w_out[0])
    fa, fb = _proj_odd(x1, sh1, sc1, w_hb, cs_pair, tm=1024)
    fp, fm = _fft_t(fa, fb, cs_t, tn=512)
    return _odd_out(fp, fm, x1, sh1, sc1, gt1, w_g, w_out1, row1(post_ln_g[1]), row1(post_ln_b[1]), tm=1024)
```

```python
import numpy as np
import jax
import jax.numpy as jnp
from jax import lax
from jax.experimental import pallas as pl
from jax.experimental.pallas import tpu as pltpu

D_MODEL = 1024
DEPTH = 2
GRID_W = 64
D_INNER = 2 * D_MODEL
HEAD_DIM = 128
A_WIDTH = D_INNER // 2
N_Q_HEADS = A_WIDTH // HEAD_DIM
N_KV_HEADS = 2
Q_PER_KV = N_Q_HEADS // N_KV_HEADS
KV_WIDTH = N_KV_HEADS * HEAD_DIM
B_WIDTH = D_INNER - A_WIDTH
CHUNK = 128
B_GROUP_DIM = 128
B_GROUPS = B_WIDTH // B_GROUP_DIM
C_GROUP_DIM = 128
C_GROUPS = D_INNER // C_GROUP_DIM
ROPE_THETA = 10000.0
ALPHA = (2 * DEPTH) ** 0.25
EPS = 1e-6

_Q0, _K0 = 0, A_WIDTH
_U0 = A_WIDTH + 2 * KV_WIDTH
_BV0 = _U0 + B_WIDTH
_G0 = _BV0 + B_WIDTH

FFT_N1 = 8
FFT_N2 = 256
FFT_SLOTS = 64

V7X_VMEM_LIMIT_BYTES = 60000 * 1024

F32 = jnp.float32
BF16 = jnp.bfloat16


def _dot(a, b):
    return jnp.dot(a, b, preferred_element_type=F32)


def _dot_nt(a, b):
    return lax.dot_general(a, b, (((1,), (1,)), ((), ())), preferred_element_type=F32)


def _ln(x):
    mu = jnp.mean(x, axis=-1, keepdims=True)
    xc = x - mu
    var = jnp.mean(xc * xc, axis=-1, keepdims=True)
    return xc * lax.rsqrt(var + EPS)


def _rms_head(z, g):
    return z * lax.rsqrt(jnp.mean(z * z, axis=-1, keepdims=True) + EPS) * g


def _rope(y, cos2, sin2):
    return y * cos2 + pltpu.roll(y, HEAD_DIM // 2, 1) * sin2


def _params(*sem):
    return pltpu.CompilerParams(dimension_semantics=sem, vmem_limit_bytes=V7X_VMEM_LIMIT_BYTES)


def _const_spec(shape):
    nd = len(shape)
    return pl.BlockSpec(shape, lambda *_: (0,) * nd, pipeline_mode=pl.Buffered(1))


def _adaln_kernel(c_ref, w_ref, b_ref, o_ref):
    h = jax.nn.silu(c_ref[...])
    w = w_ref[0]
    h_hi = h.astype(BF16)
    h_lo = (h - h_hi.astype(F32)).astype(BF16)
    w_hi = w.astype(BF16)
    w_lo = (w - w_hi.astype(F32)).astype(BF16)
    o_ref[0] = _dot(h_hi, w_hi) + _dot(h_hi, w_lo) + _dot(h_lo, w_hi) + b_ref[0]


def _adaln(cond, w_mod, b_mod):
    r = cond.shape[0]
    tn = D_MODEL
    return pl.pallas_call(
        _adaln_kernel,
        grid=(DEPTH, 3 * D_MODEL // tn),
        in_specs=[
            pl.BlockSpec((r, D_MODEL), lambda l, j: (0, 0)),
            pl.BlockSpec((1, D_MODEL, tn), lambda l, j: (l, 0, j)),
            pl.BlockSpec((1, 1, tn), lambda l, j: (l, 0, j)),
        ],
        out_specs=pl.BlockSpec((1, r, tn), lambda l, j: (l, 0, j)),
        out_shape=jax.ShapeDtypeStruct((DEPTH, r, 3 * D_MODEL), F32),
        compiler_params=_params("arbitrary", "arbitrary"),
        name="adaln",
    )(cond, w_mod, b_mod.reshape(DEPTH, 1, 3 * D_MODEL))


def _proj_even_kernel(x_ref, sh_ref, sc_ref, w_ref, qg_ref, kg_ref, vg_ref, vb_ref, cos_ref, sin_ref,
                      q_ref, k_ref, v_ref, u_ref, vn_ref):
    qg = qg_ref[...] * (HEAD_DIM ** -0.5)
    kg = kg_ref[...]
    cw = 4 * HEAD_DIM
    for r0 in range(0, x_ref.shape[1], ROW_CHAIN):
        rows = slice(r0, r0 + ROW_CHAIN)
        m = (_ln(x_ref[0, rows]) * (1.0 + sc_ref[0]) + sh_ref[0]).astype(BF16)
        cos2 = cos_ref[rows]
        sin2 = sin_ref[rows]
        for c in range(A_WIDTH // cw):
            z = _dot(m, w_ref[:, _Q0 + c * cw:_Q0 + (c + 1) * cw])
            for j in range(cw // HEAD_DIM):
                zh = z[:, j * HEAD_DIM:(j + 1) * HEAD_DIM]
                q_ref[0, c * (cw // HEAD_DIM) + j, rows] = _rope(_rms_head(zh, qg), cos2, sin2).astype(BF16)
        z = _dot(m, w_ref[:, _K0:_K0 + 2 * KV_WIDTH])
        for j in range(N_KV_HEADS):
            zh = z[:, j * HEAD_DIM:(j + 1) * HEAD_DIM]
            k_ref[0, rows, j * HEAD_DIM:(j + 1) * HEAD_DIM] = _rope(_rms_head(zh, kg), cos2, sin2).astype(BF16)
        v_ref[0, rows] = z[:, KV_WIDTH:].astype(BF16)
        for c in range(B_WIDTH // cw):
            z = _dot(m, w_ref[:, _U0 + c * cw:_U0 + (c + 1) * cw])
            u_ref[0, rows, c * cw:(c + 1) * cw] = jax.nn.gelu(z).astype(BF16)
        gv = jax.nn.gelu(_dot(m, w_ref[:, _BV0:_BV0 + B_WIDTH]))
        vn_ref[0, rows] = (_ln(gv) * vg_ref[...] + vb_ref[...]).astype(BF16)


def _proj_even(x, sh, sc, w_in, q_g, k_g, v_g, v_b, cos2, sin2, tm):
    bsz, t, _ = x.shape
    row = lambda b, i: (b, i, 0)
    per_b = pl.BlockSpec((1, 1, D_MODEL), lambda b, i: (b, 0, 0))
    bf = lambda *s: jax.ShapeDtypeStruct(s, BF16)
    return pl.pallas_call(
        _proj_even_kernel,
        grid=(bsz, t // tm),
        in_specs=[
            pl.BlockSpec((1, tm, D_MODEL), row), per_b, per_b,
            _const_spec((D_MODEL, _G0)),
            _const_spec((1, HEAD_DIM)), _const_spec((1, HEAD_DIM)),
            _const_spec((1, B_WIDTH)), _const_spec((1, B_WIDTH)),
            pl.BlockSpec((tm, HEAD_DIM), lambda b, i: (i, 0)),
            pl.BlockSpec((tm, HEAD_DIM), lambda b, i: (i, 0)),
        ],
        out_specs=[
            pl.BlockSpec((1, N_Q_HEADS, tm, HEAD_DIM), lambda b, i: (b, 0, i, 0)),
            pl.BlockSpec((1, tm, KV_WIDTH), row), pl.BlockSpec((1, tm, KV_WIDTH), row),
            pl.BlockSpec((1, tm, B_WIDTH), row), pl.BlockSpec((1, tm, B_WIDTH), row),
        ],
        out_shape=[bf(bsz, N_Q_HEADS, t, HEAD_DIM), bf(bsz, t, KV_WIDTH), bf(bsz, t, KV_WIDTH),
                   bf(bsz, t, B_WIDTH), bf(bsz, t, B_WIDTH)],
        compiler_params=_params("arbitrary", "arbitrary"),
        name="proj_even",
    )(x, sh, sc, w_in, q_g, k_g, v_g, v_b, cos2, sin2)


def _ctx_kv_kernel(c_ref, sh_ref, sc_ref, w_ref, kg_ref, kc_ref, vc_ref):
    nb, s, _ = c_ref.shape
    c = c_ref[...].reshape(nb * s, D_MODEL)
    m = (_ln(c) * (1.0 + sc_ref[...]) + sh_ref[...]).astype(BF16)
    z = _dot(m, w_ref[...])
    for j in range(N_KV_HEADS):
        zh = z[:, j * HEAD_DIM:(j + 1) * HEAD_DIM]
        kc_ref[:, :, j * HEAD_DIM:(j + 1) * HEAD_DIM] = (
            _rms_head(zh, kg_ref[...]).astype(BF16).reshape(nb, s, HEAD_DIM))
    vc_ref[...] = z[:, KV_WIDTH:].astype(BF16).reshape(nb, s, KV_WIDTH)


def _ctx_kv(ctx, sh_c, sc_c, w_in, k_g, nb):
    bsz, s, _ = ctx.shape
    assert _K0 % (2 * KV_WIDTH) == 0 and bsz % nb == 0
    return pl.pallas_call(
        _ctx_kv_kernel,
        grid=(bsz // nb,),
        in_specs=[
            pl.BlockSpec((nb, s, D_MODEL), lambda b: (b, 0, 0)),
            pl.BlockSpec((1, D_MODEL), lambda b: (0, 0)), pl.BlockSpec((1, D_MODEL), lambda b: (0, 0)),
            pl.BlockSpec((D_MODEL, 2 * KV_WIDTH), lambda b: (0, _K0 // (2 * KV_WIDTH))),
            pl.BlockSpec((1, HEAD_DIM), lambda b: (0, 0)),
        ],
        out_specs=[pl.BlockSpec((nb, s, KV_WIDTH), lambda b: (b, 0, 0))] * 2,
        out_shape=[jax.ShapeDtypeStruct((bsz, s, KV_WIDTH), BF16)] * 2,
        compiler_params=_params("arbitrary"),
        name="ctx_kv",
    )(ctx, sh_c, sc_c, w_in, k_g)


def _attn_scores(q, kc_ref, k_ref):
    return _dot_nt(q, kc_ref[0]), _dot_nt(q, k_ref[0])


def _attn_kernel(q_ref, kc_ref, vc_ref, k_ref, v_ref, qn_ref, kcn_ref, kn_ref, a_ref, s0_ref):
    nc = kc_ref.shape[1]
    first = (pl.program_id(0) == 0) & (pl.program_id(1) == 0) & (pl.program_id(2) == 0)

    @pl.when(first)
    def _():
        s_c, s_x = _attn_scores(q_ref[0, 0, :ATTN_ROWS], kc_ref, k_ref)
        s0_ref[:, :nc] = s_c
        s0_ref[:, nc:] = s_x

    for r0 in range(0, q_ref.shape[2], ATTN_ROWS):
        rows = slice(r0, r0 + ATTN_ROWS)
        for j in range(q_ref.shape[1]):
            if r0 == 0 and j == 0:
                s_c, s_x = s0_ref[:, :nc], s0_ref[:, nc:]
            else:
                s_c, s_x = _attn_scores(q_ref[0, j, rows], kc_ref, k_ref)
            mx = jnp.maximum(jnp.max(s_c, axis=-1, keepdims=True), jnp.max(s_x, axis=-1, keepdims=True))
            p_c = jnp.exp(s_c - mx)
            p_x = jnp.exp(s_x - mx)
            den = jnp.sum(p_c, axis=-1, keepdims=True) + jnp.sum(p_x, axis=-1, keepdims=True)
            o = (_dot(p_c.astype(BF16), vc_ref[0]) + _dot(p_x.astype(BF16), v_ref[0])) / den
            a_ref[0, rows, j * HEAD_DIM:(j + 1) * HEAD_DIM] = o.astype(BF16)
    s_c, s_x = _attn_scores(qn_ref[0, 0], kcn_ref, kn_ref)
    s0_ref[:, :nc] = s_c
    s0_ref[:, nc:] = s_x


def _attention(q, kc, vc, k, v, tq):
    bsz, _, t, _ = q.shape
    s = kc.shape[1]
    nt = t // tq
    last = bsz * N_KV_HEADS * nt - 1

    def nxt(b, h, i):
        n = jnp.minimum((b * N_KV_HEADS + h) * nt + i + 1, last)
        return n // (N_KV_HEADS * nt), (n // nt) % N_KV_HEADS, n % nt

    def qn_map(b, h, i):
        b2, h2, i2 = nxt(b, h, i)
        return b2, h2 * Q_PER_KV, i2 * (tq // ATTN_ROWS), 0

    def kvn_map(b, h, i):
        b2, h2, _ = nxt(b, h, i)
        return b2, 0, h2

    kv_c = pl.BlockSpec((1, s, HEAD_DIM), lambda b, h, i: (b, 0, h))
    kv_x = pl.BlockSpec((1, t, HEAD_DIM), lambda b, h, i: (b, 0, h))
    return pl.pallas_call(
        _attn_kernel,
        grid=(bsz, N_KV_HEADS, nt),
        in_specs=[pl.BlockSpec((1, Q_PER_KV, tq, HEAD_DIM), lambda b, h, i: (b, h, i, 0)),
                  kv_c, kv_c, kv_x, kv_x,
                  pl.BlockSpec((1, 1, ATTN_ROWS, HEAD_DIM), qn_map),
                  pl.BlockSpec((1, s, HEAD_DIM), kvn_map), pl.BlockSpec((1, t, HEAD_DIM), kvn_map)],
        out_specs=pl.BlockSpec((1, tq, Q_PER_KV * HEAD_DIM), lambda b, h, i: (b, i, h)),
        out_shape=jax.ShapeDtypeStruct((bsz, t, A_WIDTH), BF16),
        scratch_shapes=[pltpu.VMEM((ATTN_ROWS, s + t), F32)],
        compiler_params=_params("arbitrary", "arbitrary", "arbitrary"),
        name="attention",
    )(q, kc, vc, k, v, q, kc, k)


def _deepnorm(x, gt, y, pg, pb):
    return _ln(ALPHA * x + gt * y) * pg + pb


def _even_out_kernel(a_ref, u_ref, vn_ref, x_ref, sh_ref, sc_ref, gt_ref, wg_ref, ws_ref, bs_ref, wo_ref,
                     pg_ref, pb_ref, o_ref, comb_ref):
    for r0 in range(0, a_ref.shape[1], OUT_CHAIN):
        rc = slice(r0, r0 + OUT_CHAIN)
        x = x_ref[0, rc]
        m = (_ln(x) * (1.0 + sc_ref[0]) + sh_ref[0]).astype(BF16)
        gate = jax.nn.silu(_dot(m, wg_ref[...]))
        comb_ref[rc, :A_WIDTH] = (a_ref[0, rc].astype(F32) * gate[:, :A_WIDTH]).astype(BF16)
        for n in range(OUT_CHAIN // CHUNK):
            rows = slice(r0 + n * CHUNK, r0 + (n + 1) * CHUNK)
            lrows = slice(n * CHUNK, (n + 1) * CHUNK)
            for g in range(B_GROUPS):
                cols = slice(g * B_GROUP_DIM, (g + 1) * B_GROUP_DIM)
                gcols = slice(A_WIDTH + g * B_GROUP_DIM, A_WIDTH + (g + 1) * B_GROUP_DIM)
                mixed = _dot(ws_ref[g], vn_ref[0, rows, cols]) + bs_ref[g]
                comb_ref[rows, gcols] = (u_ref[0, rows, cols].astype(F32) * mixed * gate[lrows, gcols]).astype(BF16)
        y = _dot(comb_ref[rc], wo_ref[...])
        o_ref[0, rc] = _deepnorm(x, gt_ref[0], y, pg_ref[...], pb_ref[...])


def _even_out(a, u, vn, x, sh, sc, gt, w_g, w_s, b_s, w_out, pg, pb, tm):
    bsz, t, _ = x.shape
    row = lambda b, i: (b, i, 0)
    per_b = pl.BlockSpec((1, 1, D_MODEL), lambda b, i: (b, 0, 0))
    return pl.pallas_call(
        _even_out_kernel,
        grid=(bsz, t // tm),
        in_specs=[
            pl.BlockSpec((1, tm, A_WIDTH), row), pl.BlockSpec((1, tm, B_WIDTH), row),
            pl.BlockSpec((1, tm, B_WIDTH), row), pl.BlockSpec((1, tm, D_MODEL), row),
            per_b, per_b, per_b,
            _const_spec((D_MODEL, D_INNER)),
            _const_spec((B_GROUPS, CHUNK, CHUNK)), _const_spec((B_GROUPS, CHUNK, B_GROUP_DIM)),
            _const_spec((D_INNER, D_MODEL)),
            _const_spec((1, D_MODEL)), _const_spec((1, D_MODEL)),
        ],
        out_specs=pl.BlockSpec((1, tm, D_MODEL), row),
        out_shape=jax.ShapeDtypeStruct((bsz, t, D_MODEL), F32),
        scratch_shapes=[pltpu.VMEM((tm, D_INNER), BF16)],
        compiler_params=_params("arbitrary", "arbitrary"),
        name="even_out",
    )(a, u, vn, x, sh, sc, gt, w_g, w_s, b_s, w_out, pg, pb)


def _proj_odd_kernel(x_ref, sh_ref, sc_ref, w_ref, cs_ref, a_ref, b_ref, m_ref):
    tm = x_ref.shape[1]
    n = tm // FFT_N1
    m32 = _ln(x_ref[0]) * (1.0 + sc_ref[0]) + sh_ref[0]
    n_slab = D_MODEL // 128
    for j in range(n_slab):
        m_ref[j] = m32[:, j * 128:(j + 1) * 128]
    mp = jnp.concatenate(
        [jnp.concatenate([m_ref[j, pl.ds(t1, n, stride=FFT_N1), :] for j in range(n_slab)], axis=1)
         for t1 in range(FFT_N1)], axis=0).astype(BF16)
    cw = 4 * C_GROUP_DIM
    pw = 2 * C_GROUP_DIM
    for c in range(D_INNER // cw):
        z = _dot(mp, w_ref[:, c * cw:(c + 1) * cw]).astype(BF16)
        for j in range(cw // pw):
            pair = c * (cw // pw) + j
            cols = slice(pair * 2 * FFT_SLOTS, (pair + 1) * 2 * FFT_SLOTS)
            ab = _dot(z[:, j * pw:(j + 1) * pw], cs_ref[...])
            for t1 in range(FFT_N1):
                a_ref[0, t1, :, cols] = ab[t1 * n:(t1 + 1) * n, :2 * FFT_SLOTS].astype(BF16)
                b_ref[0, t1, :, cols] = ab[t1 * n:(t1 + 1) * n, 2 * FFT_SLOTS:].astype(BF16)


def _proj_odd(x, sh, sc, w_hb, cs_pair, tm):
    bsz, t, _ = x.shape
    row = lambda b, i: (b, i, 0)
    per_b = pl.BlockSpec((1, 1, D_MODEL), lambda b, i: (b, 0, 0))
    n = tm // FFT_N1
    width = C_GROUPS * FFT_SLOTS
    perm_spec = pl.BlockSpec((1, FFT_N1, n, width), lambda b, i: (b, 0, i, 0))
    perm = jax.ShapeDtypeStruct((bsz, FFT_N1, t // FFT_N1, width), BF16)
    return pl.pallas_call(
        _proj_odd_kernel,
        grid=(bsz, t // tm),
        in_specs=[pl.BlockSpec((1, tm, D_MODEL), row), per_b, per_b,
                  _const_spec((D_MODEL, D_INNER)), _const_spec((2 * C_GROUP_DIM, 2 * C_GROUP_DIM))],
        out_specs=[perm_spec, perm_spec],
        out_shape=[perm, perm],
        scratch_shapes=[pltpu.VMEM((D_MODEL // 128, tm, 128), F32)],
        compiler_params=_params("arbitrary", "arbitrary"),
        name="proj_odd",
    )(x, sh, sc, w_hb, cs_pair)


def _dft4(ar, ai):
    s0r, s0i = ar[0] + ar[2], ai[0] + ai[2]
    s1r, s1i = ar[0] - ar[2], ai[0] - ai[2]
    s2r, s2i = ar[1] + ar[3], ai[1] + ai[3]
    s3r, s3i = ar[1] - ar[3], ai[1] - ai[3]
    return ([s0r + s2r, s1r - s3i, s0r - s2r, s1r + s3i],
            [s0i + s2i, s1i + s3r, s0i - s2i, s1i - s3r])


def _dft8(zr, zi):
    er, ei = _dft4(zr[0::2], zi[0::2])
    orr, oi = _dft4(zr[1::2], zi[1::2])
    h = np.float32(np.sqrt(0.5))
    tr = [orr[0], (orr[1] - oi[1]) * h, -oi[2], (-orr[3] - oi[3]) * h]
    ti = [oi[0], (orr[1] + oi[1]) * h, orr[2], (orr[3] - oi[3]) * h]
    xr = [er[k] + tr[k] for k in range(4)] + [er[k] - tr[k] for k in range(4)]
    xi = [ei[k] + ti[k] for k in range(4)] + [ei[k] - ti[k] for k in range(4)]
    return xr, xi


ROW_CHAIN = 256
OUT_CHAIN = 256
ATTN_ROWS = 512
FFT_COLS = 256
FFT_ROWS = 16


def _fft_t_kernel(a_ref, b_ref, cs_ref, fp_ref, fm_ref, g_ref):
    half = FFT_N2
    lane = lax.broadcasted_iota(jnp.int32, (8, 128), 1)
    packed = (lane % FFT_SLOTS) == 0
    for ch in range(a_ref.shape[3] // FFT_COLS):
        c0 = ch * FFT_COLS
        for t1 in range(FFT_N1):
            g_ref[ch, t1, 0] = _dot(cs_ref[t1], a_ref[0, t1, :, c0:c0 + FFT_COLS])
            g_ref[ch, t1, 1] = _dot(cs_ref[t1], b_ref[0, t1, :, c0:c0 + FFT_COLS])
        for r in range(0, FFT_N2, FFT_ROWS):
            for j in range(FFT_COLS // 128):
                lanes = slice(j * 128, (j + 1) * 128)
                plus, minus = [], []
                for r8 in range(r, r + FFT_ROWS, 8):
                    rows = slice(r8, r8 + 8)
                    rows_s = slice(half + r8, half + r8 + 8)
                    pr, pi, mr, mi = [], [], [], []
                    for t1 in range(FFT_N1):
                        p1 = g_ref[ch, t1, 0, rows, lanes]
                        p3 = g_ref[ch, t1, 0, rows_s, lanes]
                        p4 = g_ref[ch, t1, 1, rows, lanes]
                        p2 = g_ref[ch, t1, 1, rows_s, lanes]
                        pr.append(jnp.where(packed, p1, p1 - p2))
                        pi.append(jnp.where(packed, p3, p3 + p4))
                        mr.append(jnp.where(packed, p4, p1 + p2))
                        mi.append(jnp.where(packed, p2, p3 - p4))
                    plus.append(_dft8(pr, pi)[0])
                    minus.append(_dft8(mr, mi)[0])
                for k1 in range(FFT_N1):
                    orow = slice(k1 * FFT_N2 + r, k1 * FFT_N2 + r + FFT_ROWS)
                    ocol = slice(c0 + j * 128, c0 + (j + 1) * 128)
                    fp_ref[0, orow, ocol] = jnp.concatenate([o[k1] for o in plus], axis=0).astype(BF16)
                    fm_ref[0, orow, ocol] = jnp.concatenate([o[k1] for o in minus], axis=0).astype(BF16)


def _fft_t(a, b, cs_t, tn):
    bsz, _, n2, w = a.shape
    t = FFT_N1 * n2
    blk = pl.BlockSpec((1, FFT_N1, n2, tn), lambda bb, j: (bb, 0, 0, j))
    out = pl.BlockSpec((1, t, tn), lambda bb, j: (bb, 0, j))
    return pl.pallas_call(
        _fft_t_kernel,
        grid=(bsz, w // tn),
        in_specs=[blk, blk, _const_spec((FFT_N1, 2 * FFT_N2, FFT_N2))],
        out_specs=[out, out],
        out_shape=[jax.ShapeDtypeStruct((bsz, t, w), BF16)] * 2,
        scratch_shapes=[pltpu.VMEM((tn // FFT_COLS, FFT_N1, 2, 2 * FFT_N2, FFT_COLS), F32)],
        compiler_params=_params("arbitrary", "arbitrary"),
        name="fft_t",
    )(a, b, cs_t)


def _odd_out_kernel(fp_ref, fm_ref, x_ref, sh_ref, sc_ref, gt_ref, wg_ref, wo_ref, pg_ref, pb_ref, o_ref):
    for r0 in range(0, x_ref.shape[1], OUT_CHAIN):
        rows = slice(r0, r0 + OUT_CHAIN)
        x = x_ref[0, rows]
        m = (_ln(x) * (1.0 + sc_ref[0]) + sh_ref[0]).astype(BF16)
        y = None
        for half, f_ref in enumerate((fp_ref, fm_ref)):
            gate = jax.nn.silu(_dot(m, wg_ref[half]))
            part = _dot((f_ref[0, rows].astype(F32) * gate).astype(BF16), wo_ref[half])
            y = part if y is None else y + part
        o_ref[0, rows] = _deepnorm(x, gt_ref[0], y, pg_ref[...], pb_ref[...])


def _odd_out(fp, fm, x, sh, sc, gt, w_g, w_out, pg, pb, tm):
    bsz, t, _ = x.shape
    row = lambda b, i: (b, i, 0)
    per_b = pl.BlockSpec((1, 1, D_MODEL), lambda b, i: (b, 0, 0))
    hw = fp.shape[2]
    return pl.pallas_call(
        _odd_out_kernel,
        grid=(bsz, t // tm),
        in_specs=[
            pl.BlockSpec((1, tm, hw), row), pl.BlockSpec((1, tm, hw), row), pl.BlockSpec((1, tm, D_MODEL), row),
            per_b, per_b, per_b,
            _const_spec((2, D_MODEL, hw)), _const_spec((2, hw, D_MODEL)),
            _const_spec((1, D_MODEL)), _const_spec((1, D_MODEL)),
        ],
        out_specs=pl.BlockSpec((1, tm, D_MODEL), row),
        out_shape=jax.ShapeDtypeStruct((bsz, t, D_MODEL), F32),
        compiler_params=_params("arbitrary", "arbitrary"),
        name="odd_out",
    )(fp, fm, x, sh, sc, gt, w_g, w_out, pg, pb)


def _slot_weights_kernel(wh_ref, wg_ref, wo_ref, pm_ref, pmt_ref, wh2_ref, wg2_ref, wo2_ref):
    wh2_ref[...] = wh_ref[...].astype(BF16)
    h = pm_ref.shape[0] // 2
    g = _dot(wg_ref[...].astype(BF16), pm_ref[...]).astype(BF16)
    wg2_ref[0] = g[:, :h]
    wg2_ref[1] = g[:, h:]
    o = _dot(pmt_ref[...], wo_ref[...].astype(BF16)).astype(BF16)
    wo2_ref[0] = o[:h]
    wo2_ref[1] = o[h:]


def _slot_weights(w_in, w_out):
    pw = 2 * C_GROUP_DIM
    n_pair = D_INNER // pw
    hw = C_GROUPS * FFT_SLOTS
    src = np.zeros(pw, np.int64)
    for grp in range(2):
        for s in range(FFT_SLOTS):
            src[grp * FFT_SLOTS + s] = grp * C_GROUP_DIM + s
            minus = FFT_SLOTS if s == 0 else C_GROUP_DIM - s
            src[2 * FFT_SLOTS + grp * FFT_SLOTS + s] = grp * C_GROUP_DIM + minus
    pm = np.zeros((pw, pw), np.float32)
    pm[src, np.arange(pw)] = 1.0
    return pl.pallas_call(
        _slot_weights_kernel,
        grid=(n_pair,),
        in_specs=[pl.BlockSpec((D_MODEL, pw), lambda p: (0, p)),
                  pl.BlockSpec((D_MODEL, pw), lambda p: (0, D_INNER // pw + p)),
                  pl.BlockSpec((pw, D_MODEL), lambda p: (p, 0)),
                  _const_spec((pw, pw)), _const_spec((pw, pw))],
        out_specs=[pl.BlockSpec((D_MODEL, pw), lambda p: (0, p)),
                   pl.BlockSpec((2, D_MODEL, pw // 2), lambda p: (0, 0, p)),
                   pl.BlockSpec((2, pw // 2, D_MODEL), lambda p: (0, p, 0))],
        out_shape=[jax.ShapeDtypeStruct((D_MODEL, D_INNER), BF16),
                   jax.ShapeDtypeStruct((2, D_MODEL, hw), BF16), jax.ShapeDtypeStruct((2, hw, D_MODEL), BF16)],
        compiler_params=_params("arbitrary"),
        name="slot_weights",
    )(w_in, w_in, w_out, jnp.asarray(pm).astype(BF16), jnp.asarray(pm.T.copy()).astype(BF16))


def _rope_tables(n_tokens):
    tok = np.arange(n_tokens)
    row = (tok // GRID_W).astype(np.float64)
    col = (tok % GRID_W).astype(np.float64)
    n_pairs_axis = HEAD_DIM // 4
    inv_freq = ROPE_THETA ** (-np.arange(n_pairs_axis, dtype=np.float64) / n_pairs_axis)
    ang = np.concatenate([row[:, None] * inv_freq, col[:, None] * inv_freq], axis=-1)
    cs, sn = np.cos(ang), np.sin(ang)
    f = lambda a: jnp.asarray(np.ascontiguousarray(a), dtype=F32)
    return f(np.concatenate([cs, cs], axis=-1)), f(np.concatenate([-sn, sn], axis=-1))


def _dft_tables(n_tokens):
    assert n_tokens == FFT_N1 * FFT_N2 and C_GROUP_DIM == 2 * FFT_SLOTS
    c = np.arange(C_GROUP_DIM)[:, None]
    sl = np.arange(FFT_SLOTS)[None, :]
    ang = 2.0 * np.pi * c * sl / C_GROUP_DIM
    norm = 1.0 / np.sqrt(float(n_tokens * C_GROUP_DIM))
    re = np.cos(ang) * norm
    im = np.sin(ang) * norm
    im[:, 0] = np.cos(np.pi * c[:, 0]) * norm
    zero = np.zeros_like(re)
    cs_pair = np.block([[re, zero, im, zero], [zero, re, zero, im]])
    k2 = np.arange(FFT_N2)[None, :, None]
    t1 = np.arange(FFT_N1)[:, None, None]
    t2 = np.arange(FFT_N2)[None, None, :]
    ang_t = 2.0 * np.pi * k2 * (t1 + FFT_N1 * t2) / n_tokens
    cs_t = np.concatenate([np.cos(ang_t), np.sin(ang_t)], axis=1)
    f = lambda a: jnp.asarray(np.ascontiguousarray(a), dtype=F32)
    return f(cs_pair).astype(BF16), f(cs_t).astype(BF16)


def kernel(x, c, ctx, c_ctx, w_mod, b_mod, post_ln_g, post_ln_b, even_w_in, even_q_norm, even_k_norm,
           even_v_ln_g, even_v_ln_b, even_w_s, even_b_s, even_w_out, odd_w_in, odd_w_out):
    bsz, t, _ = x.shape
    assert DEPTH == 2 and t % CHUNK == 0
    row1 = lambda v: v.reshape(1, -1)

    n_cond = -(-(bsz + 1) // 8) * 8
    cond = jnp.zeros((n_cond, D_MODEL), F32).at[:bsz].set(c).at[bsz].set(c_ctx)
    mod = _adaln(cond, w_mod, b_mod)
    split = lambda l, rows: [mod[l, rows, i * D_MODEL:(i + 1) * D_MODEL] for i in range(3)]
    sh0, sc0, gt0 = [v[:, None, :] for v in split(0, slice(0, bsz))]
    sh0c, sc0c, _ = split(0, slice(bsz, bsz + 1))
    sh1, sc1, gt1 = [v[:, None, :] for v in split(1, slice(0, bsz))]

    cos2, sin2 = _rope_tables(t)
    cs_pair, cs_t = _dft_tables(t)

    w_in0 = even_w_in[0, :, :_G0].astype(BF16)
    w_g0 = even_w_in[0, :, _G0:].astype(BF16)
    q, k, v, u, vn = _proj_even(x, sh0, sc0, w_in0, row1(even_q_norm[0]), row1(even_k_norm[0]),
                                row1(even_v_ln_g[0]), row1(even_v_ln_b[0]), cos2, sin2, tm=1024)
    kc, vc = _ctx_kv(ctx, sh0c, sc0c, w_in0, row1(even_k_norm[0]), nb=4)
    a = _attention(q, kc, vc, k, v, tq=1024)
    b_s = jnp.broadcast_to(even_b_s[0][:, :, None], (B_GROUPS, CHUNK, B_GROUP_DIM))
    x1 = _even_out(a, u, vn, x, sh0, sc0, gt0, w_g0, even_w_s[0].astype(BF16), b_s, even_w_out[0].astype(BF16),
                   row1(post_ln_g[0]), row1(post_ln_b[0]), tm=1024)

    w_hb, w_g, w_out1 = _slot_weights(odd_w_in[0], odd_w_out[0])
    fa, fb = _proj_odd(x1, sh1, sc1, w_hb, cs_pair, tm=1024)
    fp, fm = _fft_t(fa, fb, cs_t, tn=512)
    return _odd_out(fp, fm, x1, sh1, sc1, gt1, w_g, w_out1, row1(post_ln_g[1]), row1(post_ln_b[1]), tm=1024)
```
